```python
import math
import jax
import jax.numpy as jnp
from jax import lax
import numpy as np

D_MODEL = 2048
BATCH = 2
SEQ = 4096
DEPTH = 4

GRID_W = 64
CTX_LEN = 256
N_BRANCH = 3
BRANCH_WIDTH = D_MODEL
N_HEADS = 16
N_KV_HEADS = 4
HEAD_DIM = 128
KV_GROUP = N_HEADS // N_KV_HEADS
ATTN_WIDTH = N_HEADS * HEAD_DIM
KV_WIDTH = N_KV_HEADS * HEAD_DIM
ROPE_AXIS_DIM = HEAD_DIM // 2
ROPE_THETA = 10000.0
Q_BLOCK = 128
HYENA_ORDER = 2
HYENA_EMB_DIM = 33
HYENA_BANDS = (HYENA_EMB_DIM - 1) // 2
HYENA_FILTER_HIDDEN = 64
HYENA_SHORT_CONV = 3
HYENA_TARGET = 1e-2
HYENA_FAST_DECAY = 0.3
HYENA_SLOW_DECAY = 1.5
POOL_WINDOWS = (2, 4, 8, 16)
POOL_GROUP = BRANCH_WIDTH // len(POOL_WINDOWS)
FFN_HIDDEN = 5632
N_MOD = 9
EPS = 1e-6

_A1 = (HYENA_ORDER + 1) * BRANCH_WIDTH
_B1 = _A1 + BRANCH_WIDTH
_Q1 = _B1 + ATTN_WIDTH
_K1 = _Q1 + KV_WIDTH
_V1 = _K1 + KV_WIDTH
N_IN = _V1 + N_BRANCH * D_MODEL

kernel_name = 'hybrid_hyena_pool_gqa_prefix_dit'

F32 = jnp.float32


def rmsnorm(x, gain):
    x32 = x.astype(F32)
    y = x32 * lax.rsqrt(jnp.mean(x32 * x32, axis=-1, keepdims=True) + EPS)
    return (y * gain.astype(F32)).astype(x.dtype)


def adaln(s, gain, shift, scale):
    return rmsnorm(s, gain) * (1.0 + scale) + shift


def swiglu(h, w_in, w_out):
    gate, up = jnp.split(h @ w_in, 2, axis=-1)
    return (jax.nn.silu(gate) * up) @ w_out


def short_conv(u, w, b):
    L = u.shape[1]
    half = HYENA_SHORT_CONV // 2
    up = jnp.pad(u, ((0, 0), (half, half), (0, 0)))
    out = b
    for j in range(HYENA_SHORT_CONV):
        out = out + up[:, j:j + L] * w[j]
    return out


def hyena_filters(L, w1, b1, w2, b2, w3, b3, w4, freq):
    t = jnp.linspace(0.0, 1.0, L, dtype=F32)[:, None]
    wpos = 2.0 * math.pi * jnp.arange(L, dtype=F32)[:, None] / L
    f = jnp.linspace(1e-4, HYENA_BANDS - 1, HYENA_BANDS, dtype=F32)[None, :]
    z = jnp.concatenate([t, jnp.cos(f * wpos), -jnp.sin(f * wpos)], axis=-1)
    fr = freq.astype(F32)
    h = jnp.sin(fr[0] * (z @ w1.astype(F32) + b1.astype(F32)))
    h = jnp.sin(fr[1] * (h @ w2.astype(F32) + b2.astype(F32)))
    h = jnp.sin(fr[2] * (h @ w3.astype(F32) + b3.astype(F32)))
    k = (h @ w4.astype(F32)).reshape(L, HYENA_ORDER, 2, BRANCH_WIDTH)
    min_decay = math.log(HYENA_TARGET) / HYENA_SLOW_DECAY
    max_decay = math.log(HYENA_TARGET) / HYENA_FAST_DECAY
    deltas = jnp.abs(jnp.linspace(min_decay, max_decay, BRANCH_WIDTH, dtype=F32))
    decay = jnp.exp(-t * deltas[None, :])
    k = k * decay[:, None, None, :]
    kf, kb = k[:, :, 0], k[:, :, 1]
    kk = jnp.concatenate([kf[:1] + kb[:1], kf[1:], jnp.zeros_like(kf[:1]), kb[:0:-1]], axis=0)
    kk = kk * lax.rsqrt(jnp.sum(kk * kk, axis=0, keepdims=True) + EPS)
    return kk


def long_conv(u, kk, bias):
    L = u.shape[1]
    n = 2 * L
    uf = jnp.fft.rfft(u, n=n, axis=1)
    kf = jnp.fft.rfft(kk, n=n, axis=0)
    y = jnp.fft.irfft(uf * kf[None], n=n, axis=1)[:, :L]
    return y + u * bias


def hyena_mixer(pa, lp):
    L = pa.shape[1]
    u = short_conv(pa, lp['conv_w'], lp['conv_b']).astype(F32)
    v, x1, x2 = jnp.split(u, HYENA_ORDER + 1, axis=-1)
    kk = hyena_filters(L, lp['filt_w1'], lp['filt_b1'], lp['filt_w2'], lp['filt_b2'],
                       lp['filt_w3'], lp['filt_b3'], lp['filt_w4'], lp['filt_freq'])
    bias = lp['hyena_bias'].astype(F32)
    z = x1 * long_conv(v, kk[:, 0], bias[0])
    y = x2 * long_conv(z, kk[:, 1], bias[1])
    return y.astype(pa.dtype)


def pool_mixer(pb, w_pool, pool_scale):
    B, L, C = pb.shape
    u = pb.astype(F32)
    S = jnp.concatenate([jnp.zeros((B, 1, C), F32), jnp.cumsum(u, axis=1)], axis=1)
    t = jnp.arange(L)
    outs = []
    for g, w in enumerate(POOL_WINDOWS):
        sl = slice(g * POOL_GROUP, (g + 1) * POOL_GROUP)
        lo = jnp.clip(t - w // 2, 0, L - 1)
        hi = jnp.clip(t + w - 1 - w // 2, 0, L - 1)
        cnt = (hi - lo + 1).astype(F32)[None, :, None]
        sg = S[:, :, sl]
        mean = (sg[:, hi + 1] - sg[:, lo]) / cnt
        outs.append(mean - u[:, :, sl])
    d = jnp.stack(outs, axis=2)
    y = jnp.einsum('blgc,gcd->blgd', d, w_pool.astype(F32)).reshape(B, L, C)
    return (y * pool_scale.astype(F32)).astype(pb.dtype)


def axial_rope_tables(L):
    n_rows = L // GRID_W
    row = jnp.repeat(jnp.arange(n_rows), GRID_W).astype(F32)
    col = jnp.tile(jnp.arange(GRID_W), n_rows).astype(F32)
    inv = ROPE_THETA ** (-jnp.arange(0, ROPE_AXIS_DIM, 2, dtype=F32) / ROPE_AXIS_DIM)
    ang_r = row[:, None] * inv[None, :]
    ang_c = col[:, None] * inv[None, :]
    return (jnp.cos(ang_r), jnp.sin(ang_r), jnp.cos(ang_c), jnp.sin(ang_c))


def _rotate(x, cos, sin):
    x1, x2 = jnp.split(x, 2, axis=-1)
    cos = cos[None, :, None, :]
    sin = sin[None, :, None, :]
    return jnp.concatenate([x1 * cos - x2 * sin, x2 * cos + x1 * sin], axis=-1)


def apply_axial_rope(x, tables):
    cr, sr, cc, sc = tables
    xr, xc = jnp.split(x.astype(F32), 2, axis=-1)
    return jnp.concatenate([_rotate(xr, cr, sr), _rotate(xc, cc, sc)], axis=-1).astype(x.dtype)


def project_kv(pk, pv, k_gain):
    B, L, _ = pk.shape
    k = rmsnorm(pk.reshape(B, L, N_KV_HEADS, HEAD_DIM), k_gain)
    v = pv.reshape(B, L, N_KV_HEADS, HEAD_DIM)
    return k, v


def block_attention(q, k, v):
    B, L = q.shape[:2]
    nb = L // Q_BLOCK
    qb = jnp.moveaxis(q.reshape(B, nb, Q_BLOCK, N_KV_HEADS, KV_GROUP, HEAD_DIM), 1, 0)
    scale = HEAD_DIM ** -0.5

    def attend(qblk):
        s = jnp.einsum('bqhgd,bkhd->bhgqk', qblk, k, preferred_element_type=F32) * scale
        p = jax.nn.softmax(s, axis=-1).astype(v.dtype)
        return jnp.einsum('bhgqk,bkhd->bqhgd', p, v)

    o = lax.map(attend, qb)
    return jnp.moveaxis(o, 0, 1).reshape(B, L, ATTN_WIDTH)


def token_mix(h, lp, rope, prefix_kv):
    B, L, _ = h.shape
    p = h @ lp['w_in']
    pa, pb, pq, pk, pv, pg = jnp.split(p, [_A1, _B1, _Q1, _K1, _V1], axis=-1)
    ya = hyena_mixer(pa, lp)
    yb = pool_mixer(pb, lp['w_pool'], lp['pool_scale'])
    q = rmsnorm(pq.reshape(B, L, N_HEADS, HEAD_DIM), lp['q_gain'])
    k, v = project_kv(pk, pv, lp['k_gain'])
    if rope is not None:
        q = apply_axial_rope(q, rope)
        k = apply_axial_rope(k, rope)
    if prefix_kv is not None:
        k_all = jnp.concatenate([prefix_kv[0], k], axis=1)
        v_all = jnp.concatenate([prefix_kv[1], v], axis=1)
    else:
        k_all, v_all = k, v
    yc = block_attention(q.reshape(B, L, N_KV_HEADS, KV_GROUP, HEAD_DIM), k_all, v_all)
    gates = jax.nn.sigmoid((pg + lp['b_gate']).astype(F32)).astype(h.dtype)
    gates = gates.reshape(B, L, N_BRANCH, D_MODEL)
    w_up = lp['w_up']
    merged = (gates[:, :, 0] * (ya @ w_up[0]) + gates[:, :, 1] * (yb @ w_up[1])
              + gates[:, :, 2] * (yc @ w_up[2]))
    return merged @ lp['w_out'], (k, v)


def context_kv(hc, lp):
    pk, pv = jnp.split(hc @ lp['w_in'][:, _Q1:_V1], 2, axis=-1)
    return project_kv(pk, pv, lp['k_gain'])


def setup_inputs(seed: int = 0) -> dict:
    key = jax.random.key(seed)
    ks = jax.random.split(key, 29)
    D = D_MODEL
    W = BRANCH_WIDTH
    H = HYENA_FILTER_HIDDEN

    def nrm(i, shape, s):
        return jax.random.normal(ks[i], shape, F32) * s

    return {
        'x': nrm(0, (BATCH, SEQ, D), 1.0),
        'c': nrm(1, (BATCH, D), 1.0),
        'ctx': nrm(2, (BATCH, CTX_LEN, D), 1.0),
        'c_ctx': nrm(3, (D,), 1.0),
        'w_mod': nrm(4, (DEPTH, D, N_MOD * D), 0.5 * D ** -0.5),
        'b_mod': nrm(5, (DEPTH, N_MOD * D), 0.02),
        'norm_gain': 1.0 + nrm(6, (DEPTH, 3, D), 0.02),
        'final_gain': 1.0 + nrm(7, (D,), 0.02),
        'ffn_w_in': nrm(8, (DEPTH, 2, D, 2 * FFN_HIDDEN), D ** -0.5),
        'ffn_w_out': nrm(9, (DEPTH, 2, FFN_HIDDEN, D), FFN_HIDDEN ** -0.5),
        'w_in': nrm(10, (DEPTH, D, N_IN), D ** -0.5),
        'b_gate': nrm(11, (DEPTH, N_BRANCH * D), 0.02),
        'conv_w': nrm(12, (DEPTH, HYENA_SHORT_CONV, (HYENA_ORDER + 1) * W), HYENA_SHORT_CONV ** -0.5),
        'conv_b': nrm(13, (DEPTH, (HYENA_ORDER + 1) * W), 0.02),
        'filt_w1': nrm(14, (DEPTH, HYENA_EMB_DIM, H), HYENA_EMB_DIM ** -0.5),
        'filt_b1': nrm(15, (DEPTH, H), 0.1),
        'filt_w2': nrm(16, (DEPTH, H, H), H ** -0.5),
        'filt_b2': nrm(17, (DEPTH, H), 0.1),
        'filt_w3': nrm(18, (DEPTH, H, H), H ** -0.5),
        'filt_b3': nrm(19, (DEPTH, H), 0.1),
        'filt_w4': nrm(20, (DEPTH, H, HYENA_ORDER * 2 * W), H ** -0.5),
        'filt_freq': 1.0 + nrm(21, (DEPTH, 3, H), 0.1),
        'hyena_bias': nrm(22, (DEPTH, HYENA_ORDER, W), 0.3),
        'w_pool': nrm(23, (DEPTH, len(POOL_WINDOWS), POOL_GROUP, POOL_GROUP), POOL_GROUP ** -0.5),
        'pool_scale': 1.0 + nrm(24, (DEPTH, W), 0.1),
        'q_gain': 1.0 + nrm(25, (DEPTH, HEAD_DIM), 0.02),
        'k_gain': 1.0 + nrm(26, (DEPTH, HEAD_DIM), 0.02),
        'w_up': nrm(27, (DEPTH, N_BRANCH, W, D), W ** -0.5),
        'w_out': nrm(28, (DEPTH, D, D), D ** -0.5),
    }


def reference(x, c, ctx, c_ctx, w_mod, b_mod, norm_gain, final_gain, ffn_w_in, ffn_w_out,
              w_in, b_gate, conv_w, conv_b, filt_w1, filt_b1, filt_w2, filt_b2, filt_w3, filt_b3,
              filt_w4, filt_freq, hyena_bias, w_pool, pool_scale, q_gain, k_gain, w_up, w_out):
    B, L, D = x.shape
    rope = axial_rope_tables(L)
    sc = jax.nn.silu(c)
    scc = jax.nn.silu(c_ctx)
    for i in range(DEPTH):
        last = i == DEPTH - 1
        lp = dict(w_in=w_in[i], b_gate=b_gate[i], conv_w=conv_w[i], conv_b=conv_b[i],
                  filt_w1=filt_w1[i], filt_b1=filt_b1[i], filt_w2=filt_w2[i], filt_b2=filt_b2[i],
                  filt_w3=filt_w3[i], filt_b3=filt_b3[i], filt_w4=filt_w4[i], filt_freq=filt_freq[i],
                  hyena_bias=hyena_bias[i], w_pool=w_pool[i], pool_scale=pool_scale[i],
                  q_gain=q_gain[i], k_gain=k_gain[i], w_up=w_up[i], w_out=w_out[i])
        mx = (sc @ w_mod[i] + b_mod[i]).reshape(B, N_MOD, 1, D)
        mc = (scc @ w_mod[i] + b_mod[i]).reshape(N_MOD, D)
        hx = adaln(x, norm_gain[i, 0], mx[:, 0], mx[:, 1])
        hc = adaln(ctx, norm_gain[i, 0], mc[0], mc[1])
        x = x + 0.5 * mx[:, 2] * swiglu(hx, ffn_w_in[i, 0], ffn_w_out[i, 0])
        ctx = ctx + 0.5 * mc[2] * swiglu(hc, ffn_w_in[i, 0], ffn_w_out[i, 0])
        hx = adaln(x, norm_gain[i, 1], mx[:, 3], mx[:, 4])
        hc = adaln(ctx, norm_gain[i, 1], mc[3], mc[4])
        if last:
            kv_ctx = context_kv(hc, lp)
        else:
            oc, kv_ctx = token_mix(hc, lp, None, None)
            ctx = ctx + mc[5] * oc
        ox, _ = token_mix(hx, lp, rope, kv_ctx)
        x = x + mx[:, 5] * ox
        hx = adaln(x, norm_gain[i, 2], mx[:, 6], mx[:, 7])
        x = x + 0.5 * mx[:, 8] * swiglu(hx, ffn_w_in[i, 1], ffn_w_out[i, 1])
        if not last:
            hc = adaln(ctx, norm_gain[i, 2], mc[6], mc[7])
            ctx = ctx + 0.5 * mc[8] * swiglu(hc, ffn_w_in[i, 1], ffn_w_out[i, 1])
    return rmsnorm(x, final_gain)
```

```python
import functools
import math

import numpy as np
import jax
import jax.numpy as jnp
from jax import lax
from jax.experimental import pallas as pl
from jax.experimental.pallas import tpu as pltpu

F32 = jnp.float32
BF16 = jnp.bfloat16

N_HEADS = 16
N_KV_HEADS = 4
GRID_W = 64
ROPE_THETA = 10000.0
HYENA_ORDER = 2
HYENA_TARGET = 1e-2
HYENA_FAST_DECAY = 0.3
HYENA_SLOW_DECAY = 1.5
POOL_WINDOWS = (2, 4, 8, 16)
N_MOD = 9
N_BRANCH = 3
EPS = 1e-6
HIGHEST = lax.Precision.HIGHEST

VMEM_LIMIT_V7X = 56 * 1024 * 1024


def _cparams(n_axes):
    return pltpu.CompilerParams(
        dimension_semantics=("arbitrary",) * n_axes, vmem_limit_bytes=VMEM_LIMIT_V7X)


def _largest_divisor(total, pref, align):
    if total <= pref:
        return total
    t = (pref // align) * align
    while t > align and total % t:
        t -= align
    assert total % t == 0, (total, pref, align)
    return t


def _silu(v):
    return v * jax.nn.sigmoid(v)


def _mod_kernel(c_ref, w_ref, b_ref, o_ref):
    s = _silu(c_ref[...]).astype(BF16)
    w = w_ref[0].astype(BF16)
    o_ref[0] = jnp.dot(s, w, preferred_element_type=F32) + b_ref[0]


def _modulation(cvec, w_mod, b_mod):
    depth, d, nm = w_mod.shape
    tn = _largest_divisor(nm, 1024, 128)
    return pl.pallas_call(
        _mod_kernel,
        grid=(depth, nm // tn),
        in_specs=[pl.BlockSpec((8, d), lambda l, n: (0, 0)),
                  pl.BlockSpec((1, d, tn), lambda l, n: (l, 0, n)),
                  pl.BlockSpec((1, 1, tn), lambda l, n: (l, 0, n))],
        out_specs=pl.BlockSpec((1, 8, tn), lambda l, n: (l, 0, n)),
        out_shape=jax.ShapeDtypeStruct((depth, 8, nm), F32),
        compiler_params=_cparams(2),
        name="modulation",
    )(cvec, w_mod, b_mod.reshape(depth, 1, nm))


def _adaln_kernel(x_ref, g_ref, sh_ref, sc_ref, o_ref):
    x = x_ref[...]
    y = x * lax.rsqrt(jnp.mean(x * x, axis=-1, keepdims=True) + EPS)
    y = y * g_ref[0]
    o_ref[...] = (y * (1.0 + sc_ref[0]) + sh_ref[0]).astype(o_ref.dtype)


def _rmsnorm_kernel(x_ref, g_ref, o_ref):
    x = x_ref[...]
    y = x * lax.rsqrt(jnp.mean(x * x, axis=-1, keepdims=True) + EPS)
    o_ref[...] = y * g_ref[...]


class _Stream:
    def __init__(self, batch, seq, ctx_len):
        self.b, self.l, self.c = batch, seq, ctx_len
        self.rows_lat = batch * seq
        self.rows = batch * (seq + ctx_len)
        self.tm = _largest_divisor(math.gcd(seq, batch * ctx_len), 512, 16)
        self.tiles_per_seq = seq // self.tm
        self.n_lat_tiles = self.rows_lat // self.tm
        self.n_tiles = self.rows // self.tm

    def group(self, i):
        return jnp.where(i < self.n_lat_tiles, i // self.tiles_per_seq, self.b)


def _adaln(st, x, gains, mods, layer, j, m_shift, m_scale):
    d = x.shape[1]
    base = layer * (st.b + 1) * N_MOD

    def mod_map(m):
        return lambda i: (base + st.group(i) * N_MOD + m, 0, 0)

    return pl.pallas_call(
        _adaln_kernel,
        grid=(st.n_tiles,),
        in_specs=[pl.BlockSpec((st.tm, d), lambda i: (i, 0)),
                  pl.BlockSpec((1, 1, d), lambda i: (layer * 3 + j, 0, 0)),
                  pl.BlockSpec((1, 1, d), mod_map(m_shift)),
                  pl.BlockSpec((1, 1, d), mod_map(m_scale))],
        out_specs=pl.BlockSpec((st.tm, d), lambda i: (i, 0)),
        out_shape=jax.ShapeDtypeStruct(x.shape, BF16),
        compiler_params=_cparams(1),
        name="adaln",
    )(x, gains, mods, mods)


def _ffn_in_kernel(h_ref, wg_ref, wu_ref, o_ref, wg_s, wu_s):
    @pl.when(pl.program_id(1) == 0)
    def _():
        wg_s[...] = wg_ref[...].astype(BF16)
        wu_s[...] = wu_ref[...].astype(BF16)

    h = h_ref[...]
    g = jnp.dot(h, wg_s[...], preferred_element_type=F32)
    u = jnp.dot(h, wu_s[...], preferred_element_type=F32)
    o_ref[...] = (_silu(g) * u).astype(o_ref.dtype)


def _ffn_in(st, h, ffn_w_in, layer, j):
    d = h.shape[1]
    f = ffn_w_in.shape[3] // 2
    tn = _largest_divisor(f, 512, 128)
    nt = f // tn
    return pl.pallas_call(
        _ffn_in_kernel,
        grid=(nt, st.n_tiles),
        in_specs=[pl.BlockSpec((st.tm, d), lambda n, m: (m, 0)),
                  pl.BlockSpec((None, None, d, tn), lambda n, m: (layer, j, 0, n)),
                  pl.BlockSpec((None, None, d, tn), lambda n, m: (layer, j, 0, n + nt))],
        out_specs=pl.BlockSpec((st.tm, tn), lambda n, m: (m, n)),
        out_shape=jax.ShapeDtypeStruct((st.rows, f), BF16),
        scratch_shapes=[pltpu.VMEM((d, tn), BF16), pltpu.VMEM((d, tn), BF16)],
        compiler_params=_cparams(2),
        name="ffn_in",
    )(h, ffn_w_in, ffn_w_in)


def _mm_res_kernel(a_ref, w_ref, x_ref, gate_ref, o_ref, w_s, *, gate_scale):
    @pl.when(pl.program_id(1) == 0)
    def _():
        w_s[...] = w_ref[...].astype(BF16)

    y = jnp.dot(a_ref[...], w_s[...], preferred_element_type=F32)
    o_ref[...] = x_ref[...] + (gate_scale * gate_ref[0]) * y


def _mm_residual(st, a, w, w_index, x, mods, layer, m_gate, gate_scale, tn_pref):
    k = a.shape[1]
    d = x.shape[1]
    tn = _largest_divisor(d, tn_pref, 128)
    base = layer * (st.b + 1) * N_MOD
    lead = (None,) * len(w_index)
    return pl.pallas_call(
        functools.partial(_mm_res_kernel, gate_scale=gate_scale),
        grid=(d // tn, st.n_tiles),
        in_specs=[pl.BlockSpec((st.tm, k), lambda n, m: (m, 0)),
                  pl.BlockSpec(lead + (k, tn), lambda n, m: tuple(w_index) + (0, n)),
                  pl.BlockSpec((st.tm, tn), lambda n, m: (m, n)),
                  pl.BlockSpec((1, 1, tn), lambda n, m: (base + st.group(m) * N_MOD + m_gate, 0, n))],
        out_specs=pl.BlockSpec((st.tm, tn), lambda n, m: (m, n)),
        out_shape=jax.ShapeDtypeStruct(x.shape, F32),
        scratch_shapes=[pltpu.VMEM((k, tn), BF16)],
        compiler_params=_cparams(2),
        name="matmul_residual",
    )(a, w, x, mods)


def _mm_kernel(h_ref, w_ref, o_ref, w_s):
    @pl.when(pl.program_id(1) == 0)
    def _():
        w_s[...] = w_ref[...].astype(BF16)

    o_ref[...] = jnp.dot(h_ref[...], w_s[...], preferred_element_type=F32).astype(o_ref.dtype)


def _in_proj(st, h, w_in, layer):
    d = h.shape[1]
    n_in = w_in.shape[2]
    tn = _largest_divisor(n_in, 1024, 128)
    return pl.pallas_call(
        _mm_kernel,
        grid=(n_in // tn, st.n_tiles),
        in_specs=[pl.BlockSpec((st.tm, d), lambda n, m: (m, 0)),
                  pl.BlockSpec((None, d, tn), lambda n, m: (layer, 0, n))],
        out_specs=pl.BlockSpec((st.tm, tn), lambda n, m: (m, n)),
        out_shape=jax.ShapeDtypeStruct((st.rows, n_in), BF16),
        scratch_shapes=[pltpu.VMEM((d, tn), BF16)],
        compiler_params=_cparams(2),
        name="in_proj",
    )(h, w_in)


def _merge_kernel(ya_ref, yb_ref, yc_ref, pa_ref, pb_ref, pc_ref, ba_ref, bb_ref, bc_ref,
                  w_ref, o_ref, w_s):
    @pl.when(pl.program_id(1) == 0)
    def _():
        w_s[...] = w_ref[...].astype(BF16)

    acc = None
    for k, (y_ref, p_ref, b_ref) in enumerate(
            ((ya_ref, pa_ref, ba_ref), (yb_ref, pb_ref, bb_ref), (yc_ref, pc_ref, bc_ref))):
        gate = jax.nn.sigmoid(p_ref[...].astype(F32) + b_ref[0])
        t = gate * jnp.dot(y_ref[...], w_s[k], preferred_element_type=F32)
        acc = t if acc is None else acc + t
    o_ref[...] = acc.astype(o_ref.dtype)


def _merge(st, ya, yb, yc, p, gate_col0, b_gate3, w_up, layer):
    w = ya.shape[1]
    d = w_up.shape[3]
    tn = _largest_divisor(d, 256, 128)
    gb = gate_col0 // tn
    dt = d // tn
    y_spec = pl.BlockSpec((st.tm, w), lambda n, m: (m, 0))

    def p_spec(k):
        return pl.BlockSpec((st.tm, tn), lambda n, m: (m, gb + k * dt + n))

    def b_spec(k):
        return pl.BlockSpec((1, 1, tn), lambda n, m: (layer * N_BRANCH + k, 0, n))

    return pl.pallas_call(
        _merge_kernel,
        grid=(dt, st.n_tiles),
        in_specs=[y_spec, y_spec, y_spec, p_spec(0), p_spec(1), p_spec(2),
                  b_spec(0), b_spec(1), b_spec(2),
                  pl.BlockSpec((None, N_BRANCH, w, tn), lambda n, m: (layer, 0, 0, n))],
        out_specs=pl.BlockSpec((st.tm, tn), lambda n, m: (m, n)),
        out_shape=jax.ShapeDtypeStruct((st.rows, d), BF16),
        scratch_shapes=[pltpu.VMEM((N_BRANCH, w, tn), BF16)],
        compiler_params=_cparams(2),
        name="merge",
    )(ya, yb, yc, p, p, p, b_gate3, b_gate3, b_gate3, w_up)


HALO = 16


def _short_conv_kernel(cur_ref, prev_ref, next_ref, w_ref, b_ref, o_ref, *, blocks_per_seq):
    j = pl.program_id(1) % blocks_per_seq
    u = cur_ref[...].astype(F32)
    t = u.shape[0]
    row = lax.broadcasted_iota(jnp.int32, u.shape, 0)
    prev_row = jnp.where(j == 0, 0.0, prev_ref[HALO - 1:HALO, :].astype(F32))
    next_row = jnp.where(j == blocks_per_seq - 1, 0.0, next_ref[0:1, :].astype(F32))
    up = jnp.where(row == 0, prev_row, pltpu.roll(u, 1, axis=0))
    un = jnp.where(row == t - 1, next_row, pltpu.roll(u, t - 1, axis=0))
    w = w_ref[...]
    o_ref[...] = (b_ref[...] + up * w[0:1] + u * w[1:2] + un * w[2:3]).astype(o_ref.dtype)


def _short_conv_call(p, conv_w, conv_b, width, row0, n_rows, seq_len, out_rows, prior):
    ts = _largest_divisor(seq_len, 512, HALO)
    ct = _largest_divisor(width, 1024, 128)
    blocks_per_seq = seq_len // ts
    rb0 = row0 // ts
    hb = ts // HALO
    n_hblocks = out_rows // HALO
    cpb = width // ct
    n_parts = HYENA_ORDER + 1

    def cur_map(c, r):
        return (rb0 + r, c)

    def prev_map(c, r):
        return (jnp.maximum((rb0 + r) * hb - 1, 0), c)

    def next_map(c, r):
        return (jnp.minimum((rb0 + r + 1) * hb, n_hblocks - 1), c)

    in_specs = [pl.BlockSpec((ts, ct), cur_map),
                pl.BlockSpec((HALO, ct), prev_map),
                pl.BlockSpec((HALO, ct), next_map),
                pl.BlockSpec((3, ct), lambda c, r: (0, c)),
                pl.BlockSpec((1, ct), lambda c, r: (0, c))]
    args = [p, p, p, conv_w, conv_b]
    aliases = {}
    kernel = functools.partial(_short_conv_kernel, blocks_per_seq=blocks_per_seq)
    if prior is not None:
        in_specs.append(pl.BlockSpec(memory_space=pl.ANY))
        args.append(prior)
        aliases = {5: 0}
        body = kernel
        kernel = lambda c, pv, nx, w, b, _prior, o: body(c, pv, nx, w, b, o)
    return pl.pallas_call(
        kernel,
        grid=(n_parts * cpb, n_rows // ts),
        in_specs=in_specs,
        out_specs=pl.BlockSpec((None, ts, ct), lambda c, r: (c // cpb, rb0 + r, c % cpb)),
        out_shape=jax.ShapeDtypeStruct((n_parts, out_rows, width), BF16),
        input_output_aliases=aliases,
        compiler_params=_cparams(2),
        name="short_conv",
    )(*args)


@functools.lru_cache(maxsize=None)
def _filter_positions(seq_len, emb_dim, pad_dim):
    bands = (emb_dim - 1) // 2
    j = np.arange(seq_len, dtype=np.float64)
    t = j / (seq_len - 1)
    wpos = 2.0 * np.pi * j / seq_len
    f = np.linspace(1e-4, bands - 1, bands)
    z = np.concatenate([t[:, None], np.cos(f[None] * wpos[:, None]), -np.sin(f[None] * wpos[:, None])], axis=1)
    lag = np.concatenate([np.arange(seq_len), [0], np.arange(seq_len - 1, 0, -1)])
    z2 = np.zeros((2 * seq_len, pad_dim), np.float64)
    z2[:, :emb_dim] = z[lag]
    return z2.astype(np.float32)


def _filter_mlp_kernel(z_ref, w1_ref, b1_ref, w2_ref, b2_ref, w3_ref, b3_ref, fr_ref, o_ref):
    fr = fr_ref[0]
    h = jnp.dot(z_ref[...], w1_ref[0], precision=HIGHEST, preferred_element_type=F32)
    h = jnp.sin(fr[0:1] * (h + b1_ref[0]))
    h = jnp.dot(h, w2_ref[0], precision=HIGHEST, preferred_element_type=F32)
    h = jnp.sin(fr[1:2] * (h + b2_ref[0]))
    h = jnp.dot(h, w3_ref[0], precision=HIGHEST, preferred_element_type=F32)
    o_ref[0] = jnp.sin(fr[2:3] * (h + b3_ref[0]))


def _filter_mlp(seq_len, w1, b1, w2, b2, w3, b3, freq):
    depth, emb, hid = w1.shape
    pad = -(-emb // 8) * 8
    n = 2 * seq_len
    z2 = jnp.asarray(_filter_positions(seq_len, emb, pad))
    w1p = jnp.pad(w1, ((0, 0), (0, pad - emb), (0, 0)))
    tr = _largest_divisor(n, 1024, 8)
    wspec = lambda k: pl.BlockSpec((1, k, hid), lambda l, r: (l, 0, 0))
    bspec = pl.BlockSpec((1, 1, hid), lambda l, r: (l, 0, 0))
    return pl.pallas_call(
        _filter_mlp_kernel,
        grid=(depth, n // tr),
        in_specs=[pl.BlockSpec((tr, pad), lambda l, r: (r, 0)),
                  wspec(pad), bspec, wspec(hid), bspec, wspec(hid), bspec,
                  pl.BlockSpec((1, 3, hid), lambda l, r: (l, 0, 0))],
        out_specs=pl.BlockSpec((1, tr, hid), lambda l, r: (l, r, 0)),
        out_shape=jax.ShapeDtypeStruct((depth, n, hid), F32),
        compiler_params=_cparams(2),
        name="filter_mlp",
    )(z2, w1p, b1.reshape(depth, 1, hid), w2, b2.reshape(depth, 1, hid), w3, b3.reshape(depth, 1, hid), freq)


def _filter_taps_kernel(h_ref, wf_ref, wb_ref, delta_ref, o_ref, *, seq_len):
    l = seq_len
    delta = delta_ref[...]
    lag = lax.broadcasted_iota(jnp.int32, (l, 1), 0)
    t_top = lag.astype(F32) / (l - 1.0)
    t_bot = jnp.where(lag == 0, 0, l - lag).astype(F32) / (l - 1.0)
    top = jnp.dot(h_ref[0, 0:l, :], wf_ref[0], precision=HIGHEST, preferred_element_type=F32)
    bot = jnp.dot(h_ref[0, l:2 * l, :], wb_ref[0], precision=HIGHEST, preferred_element_type=F32)
    top = top * jnp.exp(-t_top * delta)
    bot = bot * jnp.exp(-t_bot * delta)
    top = top + jnp.where(lag == 0, bot[0:1, :], 0.0)
    bot = jnp.where(lag == 0, 0.0, bot)
    ss = jnp.sum(top * top, axis=0, keepdims=True) + jnp.sum(bot * bot, axis=0, keepdims=True)
    scale = lax.rsqrt(ss + EPS)
    o_ref[0, 0:l, :] = (top * scale).astype(o_ref.dtype)
    o_ref[0, l:2 * l, :] = (bot * scale).astype(o_ref.dtype)


def _filter_taps(h_all, filt_w4, layer, seq_len, width):
    hid = h_all.shape[2]
    n = 2 * seq_len
    ct = 128
    cpb = width // ct
    min_decay = math.log(HYENA_TARGET) / HYENA_SLOW_DECAY
    max_decay = math.log(HYENA_TARGET) / HYENA_FAST_DECAY
    delta = jnp.asarray(np.abs(np.linspace(min_decay, max_decay, width)).astype(np.float32)).reshape(1, width)
    return pl.pallas_call(
        functools.partial(_filter_taps_kernel, seq_len=seq_len),
        grid=(HYENA_ORDER, cpb),
        in_specs=[pl.BlockSpec((1, n, hid), lambda o, c: (layer, 0, 0)),
                  pl.BlockSpec((1, hid, ct), lambda o, c: (layer, 0, (2 * o) * cpb + c)),
                  pl.BlockSpec((1, hid, ct), lambda o, c: (layer, 0, (2 * o + 1) * cpb + c)),
                  pl.BlockSpec((1, ct), lambda o, c: (0, c))],
        out_specs=pl.BlockSpec((1, n, ct), lambda o, c: (o, 0, c)),
        out_shape=jax.ShapeDtypeStruct((HYENA_ORDER, n, width), BF16),
        compiler_params=_cparams(2),
        name="filter_taps",
    )(h_all, filt_w4, filt_w4, delta)


def _real_form(e):
    return np.block([[e.real, -e.imag], [e.imag, e.real]])


@functools.lru_cache(maxsize=None)
def _dft_tables(n1, n2):
    n = n1 * n2
    j2 = np.arange(n2)[:, None, None]
    k1 = np.arange(n1)[None, :, None]
    j1 = np.arange(n1)[None, None, :]
    e = np.exp(-2j * np.pi * ((j1 * k1) / n1 + (j2 * k1) / n))
    eh = e[:, :, :n1 // 2]
    t_data = np.concatenate(
        [np.concatenate([eh.real, -eh.imag], axis=2), np.concatenate([eh.imag, eh.real], axis=2)], axis=1)
    t_filt = np.concatenate([e.real, e.imag], axis=1)
    hinv = np.conj(np.transpose(eh, (0, 2, 1))) / n
    t_inv = np.concatenate(
        [np.concatenate([hinv.real, -hinv.imag], axis=2), np.concatenate([hinv.imag, hinv.real], axis=2)], axis=1)
    f2 = np.exp(-2j * np.pi * np.outer(np.arange(n2), np.arange(n2)) / n2)
    as32 = lambda a: np.ascontiguousarray(a, dtype=np.float32)
    return as32(t_data), as32(t_filt), as32(t_inv), as32(_real_form(f2)), as32(_real_form(np.conj(f2)))


def _split_n(n):
    n2 = 1
    while n2 < 128 and n // (n2 * 2) >= 64:
        n2 *= 2
    return n // n2, n2


def _fft_s1_kernel(z_ref, t_ref, o_ref, *, g_cols, width):
    for g in range(g_cols):
        sl = slice(g * width, (g + 1) * width)
        o_ref[:, sl] = jnp.dot(t_ref[g], z_ref[:, sl], preferred_element_type=F32).astype(o_ref.dtype)


def _fft_mid_kernel(a_ref, f_ref, f2_ref, f2i_ref, o_ref, *, g_rows, n2):
    f2 = f2_ref[...]
    f2i = f2i_ref[...]
    for g in range(g_rows):
        a = jnp.concatenate([a_ref[0, g], a_ref[1, g]], axis=0)
        f = jnp.concatenate([f_ref[0, g], f_ref[1, g]], axis=0)
        u = jnp.dot(f2, a, preferred_element_type=F32)
        k = jnp.dot(f2, f, preferred_element_type=F32)
        ur, ui, kr, ki = u[:n2], u[n2:], k[:n2], k[n2:]
        y = jnp.concatenate([ur * kr - ui * ki, ur * ki + ui * kr], axis=0).astype(BF16)
        c = jnp.dot(f2i, y, preferred_element_type=F32)
        o_ref[0, g] = c[:n2].astype(o_ref.dtype)
        o_ref[1, g] = c[n2:].astype(o_ref.dtype)


def _fft_s3_kernel(c_ref, t_ref, u_ref, x_ref, bias_ref, o_ref, *, g_cols, width):
    bias = bias_ref[...]
    for g in range(g_cols):
        sl = slice(g * width, (g + 1) * width)
        y = jnp.dot(t_ref[g], c_ref[:, sl], preferred_element_type=F32)
        u = u_ref[:, sl].astype(F32)
        o_ref[:, sl] = (x_ref[:, sl].astype(F32) * (y + u * bias)).astype(o_ref.dtype)


def _hyena_small_kernel(u_ref, k_ref, x_ref, bias_ref, td_ref, tf_ref, ti_ref, _latent_rows_ref, o_ref, *, n):
    u = u_ref[...]
    a = jnp.dot(td_ref[...], u, preferred_element_type=F32)
    k = jnp.dot(tf_ref[...], k_ref[...], preferred_element_type=F32)
    ar, ai, kr, ki = a[:n], a[n:], k[:n], k[n:]
    y = jnp.concatenate([ar * kr - ai * ki, ar * ki + ai * kr], axis=0).astype(BF16)
    y = jnp.dot(ti_ref[...], y, preferred_element_type=F32)
    o_ref[...] = (x_ref[...].astype(F32) * (y + u.astype(F32) * bias_ref[...])).astype(o_ref.dtype)


def _bf16_table(a):
    return jnp.asarray(a).astype(BF16)


def _long_conv_gate(st, u3, u_part, x_arr, x_part, kk, kk_ctx, order, bias):
    assert st.b == 2, "the two batch elements are packed as one complex signal"
    _, rows, width = u3.shape
    l, lc = st.l, st.c
    n = 2 * l
    n1, n2 = _split_n(n)
    t_data, t_filt, t_inv, f2, f2i = (_bf16_table(a) for a in _dft_tables(n1, n2))
    g_cols = min(8, n2)
    g_rows = min(4, n1)
    gw = g_cols * width
    bias2 = bias.reshape(1, width)

    kk_v = kk.reshape(HYENA_ORDER, n1, n2 * width)
    a_filt = pl.pallas_call(
        functools.partial(_fft_s1_kernel, g_cols=g_cols, width=width),
        grid=(n2 // g_cols,),
        in_specs=[pl.BlockSpec((None, n1, gw), lambda j: (order, 0, j)),
                  pl.BlockSpec((g_cols, 2 * n1, n1), lambda j: (j, 0, 0))],
        out_specs=pl.BlockSpec((2 * n1, gw), lambda j: (0, j)),
        out_shape=jax.ShapeDtypeStruct((2 * n1, n2 * width), BF16),
        compiler_params=_cparams(1),
        name="fft_stage1_filter",
    )(kk_v, t_filt)

    u_v = u3.reshape(u3.shape[0], rows // n2, n2 * width)
    a_data = pl.pallas_call(
        functools.partial(_fft_s1_kernel, g_cols=g_cols, width=width),
        grid=(n2 // g_cols,),
        in_specs=[pl.BlockSpec((None, n1, gw), lambda j: (u_part, 0, j)),
                  pl.BlockSpec((g_cols, 2 * n1, n1), lambda j: (j, 0, 0))],
        out_specs=pl.BlockSpec((2 * n1, gw), lambda j: (0, j)),
        out_shape=jax.ShapeDtypeStruct((2 * n1, n2 * width), BF16),
        compiler_params=_cparams(1),
        name="fft_stage1_data",
    )(u_v, t_data)

    blk = pl.BlockSpec((2, g_rows, n2, width), lambda k: (0, k, 0, 0))
    c_mid = pl.pallas_call(
        functools.partial(_fft_mid_kernel, g_rows=g_rows, n2=n2),
        grid=(n1 // g_rows,),
        in_specs=[blk, blk,
                  pl.BlockSpec((2 * n2, 2 * n2), lambda k: (0, 0)),
                  pl.BlockSpec((2 * n2, 2 * n2), lambda k: (0, 0))],
        out_specs=blk,
        out_shape=jax.ShapeDtypeStruct((2, n1, n2, width), BF16),
        compiler_params=_cparams(1),
        name="fft_mid",
    )(a_data.reshape(2, n1, n2, width), a_filt.reshape(2, n1, n2, width), f2, f2i)

    x_v = x_arr.reshape(x_arr.shape[0], rows // n2, n2 * width)
    bias_t = bias2
    out = pl.pallas_call(
        functools.partial(_fft_s3_kernel, g_cols=g_cols, width=width),
        grid=(n2 // g_cols,),
        in_specs=[pl.BlockSpec((2 * n1, gw), lambda j: (0, j)),
                  pl.BlockSpec((g_cols, n1, 2 * n1), lambda j: (j, 0, 0)),
                  pl.BlockSpec((None, n1, gw), lambda j: (u_part, 0, j)),
                  pl.BlockSpec((None, n1, gw), lambda j: (x_part, 0, j)),
                  pl.BlockSpec((1, width), lambda j: (0, 0))],
        out_specs=pl.BlockSpec((n1, gw), lambda j: (0, j)),
        out_shape=jax.ShapeDtypeStruct((rows // n2, n2 * width), BF16),
        compiler_params=_cparams(1),
        name="fft_stage3",
    )(c_mid.reshape(2 * n1, n2 * width), t_inv, u_v, x_v, bias_t)
    out = out.reshape(rows, width)

    nc = 2 * lc
    tc_data, tc_filt, tc_inv, _, _ = (_bf16_table(a) for a in _dft_tables(nc, 1))
    ct = _largest_divisor(width, 512, 128)
    rb = (st.rows_lat) // (2 * lc)
    out = pl.pallas_call(
        functools.partial(_hyena_small_kernel, n=nc),
        grid=(width // ct,),
        in_specs=[pl.BlockSpec((None, 2 * lc, ct), lambda c: (u_part, rb, c)),
                  pl.BlockSpec((None, nc, ct), lambda c: (order, 0, c)),
                  pl.BlockSpec((None, 2 * lc, ct), lambda c: (x_part, rb, c)),
                  pl.BlockSpec((1, ct), lambda c: (0, c)),
                  pl.BlockSpec((2 * nc, nc), lambda c: (0, 0)),
                  pl.BlockSpec((2 * nc, nc), lambda c: (0, 0)),
                  pl.BlockSpec((nc, 2 * nc), lambda c: (0, 0)),
                  pl.BlockSpec(memory_space=pl.ANY)],
        out_specs=pl.BlockSpec((2 * lc, ct), lambda c: (rb, c)),
        out_shape=jax.ShapeDtypeStruct((rows, width), BF16),
        input_output_aliases={7: 0},
        compiler_params=_cparams(1),
        name="hyena_context",
    )(u3, kk_ctx, x_arr, bias2, tc_data[0], tc_filt[0], tc_inv[0], out)
    return out


@functools.lru_cache(maxsize=None)
def _pool_tables(t, windows):
    out = np.zeros((4, len(windows), t, 3 * t), np.float64)
    for v in range(4):
        lo_bound = t if v & 1 else 0
        hi_bound = 2 * t - 1 if v & 2 else 3 * t - 1
        for g, w in enumerate(windows):
            for r in range(t):
                pos = t + r
                lo = max(pos - w // 2, lo_bound)
                hi = min(pos + w - 1 - w // 2, hi_bound)
                out[v, g, r, lo:hi + 1] = 1.0 / (hi - lo + 1)
                out[v, g, r, pos] -= 1.0
    return out.astype(np.float32)


def _pool_kernel(prev_ref, cur_ref, next_ref, m_ref, w_ref, s_ref, *rest, n_groups, gw):
    o_ref = rest[-1]
    for g in range(n_groups):
        sl = slice(g * gw, (g + 1) * gw)
        u = jnp.concatenate([prev_ref[:, sl], cur_ref[:, sl], next_ref[:, sl]], axis=0)
        d = jnp.dot(m_ref[0, g], u, preferred_element_type=F32)
        y = jnp.dot(d.astype(BF16), w_ref[g], preferred_element_type=F32)
        o_ref[:, sl] = (y * s_ref[:, sl]).astype(o_ref.dtype)


def _pool_call(p, col_block, w_pool_bf, pool_scale3, layer, width, row0, n_seq, seq_len, out_rows, prior):
    t = _largest_divisor(seq_len, 256, 16)
    nblk = seq_len // t
    rb0 = row0 // t
    n_groups = len(POOL_WINDOWS)
    gw = width // n_groups
    tables = _bf16_table(_pool_tables(t, POOL_WINDOWS))

    def blk(delta):
        return lambda s, j: (rb0 + s * nblk + jnp.clip(j + delta, 0, nblk - 1), col_block)

    def variant(s, j):
        return ((j == 0).astype(jnp.int32) + 2 * (j == nblk - 1).astype(jnp.int32), 0, 0, 0)

    in_specs = [pl.BlockSpec((t, width), blk(-1)),
                pl.BlockSpec((t, width), blk(0)),
                pl.BlockSpec((t, width), blk(1)),
                pl.BlockSpec((1, n_groups, t, 3 * t), variant),
                pl.BlockSpec((None, n_groups, gw, gw), lambda s, j: (layer, 0, 0, 0)),
                pl.BlockSpec((None, 1, width), lambda s, j: (layer, 0, 0))]
    args = [p, p, p, tables, w_pool_bf, pool_scale3]
    aliases = {}
    if prior is not None:
        in_specs.append(pl.BlockSpec(memory_space=pl.ANY))
        args.append(prior)
        aliases = {6: 0}
    return pl.pallas_call(
        functools.partial(_pool_kernel, n_groups=n_groups, gw=gw),
        grid=(n_seq, nblk),
        in_specs=in_specs,
        out_specs=pl.BlockSpec((t, width), lambda s, j: (rb0 + s * nblk + j, 0)),
        out_shape=jax.ShapeDtypeStruct((out_rows, width), BF16),
        input_output_aliases=aliases,
        compiler_params=_cparams(2),
        name="pool",
    )(*args)


@functools.lru_cache(maxsize=None)
def _rope_tables(seq_len, grid_w, hd, pad_rows):
    axis = hd // 2
    pos = np.arange(seq_len)
    inv = ROPE_THETA ** (-np.arange(0, axis, 2, dtype=np.float64) / axis)
    ang_r = (pos // grid_w)[:, None] * inv[None]
    ang_c = (pos % grid_w)[:, None] * inv[None]
    cos = np.concatenate([np.cos(ang_r), np.cos(ang_r), np.cos(ang_c), np.cos(ang_c)], axis=1)
    sin = np.concatenate([-np.sin(ang_r), np.sin(ang_r), -np.sin(ang_c), np.sin(ang_c)], axis=1)
    cos = np.concatenate([cos, np.ones((pad_rows, hd))], axis=0)
    sin = np.concatenate([sin, np.zeros((pad_rows, hd))], axis=0)
    return cos.astype(np.float32), sin.astype(np.float32)


def _qk_prep_kernel(q_ref, k_ref, qg_ref, kg_ref, cos_ref, sin_ref, qo_ref, ko_ref, *, hd, q_scale):
    cos = cos_ref[...]
    sin = sin_ref[...]
    lane = lax.broadcasted_iota(jnp.int32, cos.shape, 1)
    first = (lane % (hd // 2)) < (hd // 4)

    def prep(x_ref, g_ref, o_ref, mul):
        for h in range(x_ref.shape[1] // hd):
            sl = slice(h * hd, (h + 1) * hd)
            x = x_ref[:, sl].astype(F32)
            y = x * lax.rsqrt(jnp.mean(x * x, axis=-1, keepdims=True) + EPS) * g_ref[...]
            partner = jnp.where(first, pltpu.roll(y, hd - hd // 4, axis=1), pltpu.roll(y, hd // 4, axis=1))
            o_ref[:, sl] = ((y * cos + partner * sin) * mul).astype(o_ref.dtype)

    prep(q_ref, qg_ref, qo_ref, q_scale)
    prep(k_ref, kg_ref, ko_ref, 1.0)


def _qk_prep(st, p, q_col0, k_col0, aw, kvw, q_gain, k_gain, layer, hd):
    cos_np, sin_np = _rope_tables(st.l, GRID_W, hd, st.tm)
    tab_map = lambda i: (jnp.where(i < st.n_lat_tiles, i % st.tiles_per_seq, st.tiles_per_seq), 0)
    return pl.pallas_call(
        functools.partial(_qk_prep_kernel, hd=hd, q_scale=hd ** -0.5),
        grid=(st.n_tiles,),
        in_specs=[pl.BlockSpec((st.tm, aw), lambda i: (i, q_col0 // aw)),
                  pl.BlockSpec((st.tm, kvw), lambda i: (i, k_col0 // kvw)),
                  pl.BlockSpec((None, 1, hd), lambda i: (layer, 0, 0)),
                  pl.BlockSpec((None, 1, hd), lambda i: (layer, 0, 0)),
                  pl.BlockSpec((st.tm, hd), tab_map),
                  pl.BlockSpec((st.tm, hd), tab_map)],
        out_specs=[pl.BlockSpec((st.tm, aw), lambda i: (i, 0)),
                   pl.BlockSpec((st.tm, kvw), lambda i: (i, 0))],
        out_shape=[jax.ShapeDtypeStruct((st.rows, aw), BF16),
                   jax.ShapeDtypeStruct((st.rows, kvw), BF16)],
        compiler_params=_cparams(1),
        name="qk_prep",
    )(p, p, q_gain.reshape(-1, 1, hd), k_gain.reshape(-1, 1, hd), jnp.asarray(cos_np), jnp.asarray(sin_np))


def _attn_kernel(*refs, group, hd, tk, n_chunks):
    q_ref, kc_ref, vc_ref = refs[:3]
    o_ref = refs[-1]
    q = q_ref[...]
    tq = q.shape[0]
    qs = jnp.concatenate([q[:, g * hd:(g + 1) * hd] for g in range(group)], axis=0)

    def update(carry, k, v):
        m, l, acc = carry
        s = lax.dot_general(qs, k, (((1,), (1,)), ((), ())), preferred_element_type=F32)
        m_new = jnp.maximum(m, jnp.max(s, axis=-1, keepdims=True))
        alpha = jnp.exp(m - m_new)
        pexp = jnp.exp(s - m_new)
        l = alpha * l + jnp.sum(pexp, axis=-1, keepdims=True)
        acc = alpha * acc + jnp.dot(pexp.astype(BF16), v, preferred_element_type=F32)
        return m_new, l, acc

    rows = group * tq
    carry = (jnp.full((rows, 1), -1e30, F32), jnp.zeros((rows, 1), F32), jnp.zeros((rows, hd), F32))
    carry = update(carry, kc_ref[...], vc_ref[...])
    if n_chunks:
        kl_ref, vl_ref = refs[3:5]

        def body(i, c):
            start = pl.multiple_of(i * tk, tk)
            return update(c, kl_ref[pl.ds(start, tk), :], vl_ref[pl.ds(start, tk), :])

        carry = lax.fori_loop(0, n_chunks, body, carry)
    _, l, acc = carry
    o = acc / l
    o_ref[...] = jnp.concatenate([o[g * tq:(g + 1) * tq] for g in range(group)], axis=1).astype(o_ref.dtype)


def _attention(st, q, k, p, v_col0, hd):
    group = N_HEADS // N_KV_HEADS
    gw = group * hd
    aw = N_HEADS * hd
    vb = v_col0 // hd
    tq = _largest_divisor(st.l, 256, 16)
    tk = _largest_divisor(st.l, 512, 128)
    qt = st.l // tq
    cb0 = st.rows_lat // st.c
    yc = pl.pallas_call(
        functools.partial(_attn_kernel, group=group, hd=hd, tk=tk, n_chunks=st.l // tk),
        grid=(st.b, N_KV_HEADS, qt),
        in_specs=[pl.BlockSpec((tq, gw), lambda b, h, t: (b * qt + t, h)),
                  pl.BlockSpec((st.c, hd), lambda b, h, t: (cb0 + b, h)),
                  pl.BlockSpec((st.c, hd), lambda b, h, t: (cb0 + b, vb + h)),
                  pl.BlockSpec((st.l, hd), lambda b, h, t: (b, h)),
                  pl.BlockSpec((st.l, hd), lambda b, h, t: (b, vb + h))],
        out_specs=pl.BlockSpec((tq, gw), lambda b, h, t: (b * qt + t, h)),
        out_shape=jax.ShapeDtypeStruct((st.rows, aw), BF16),
        compiler_params=_cparams(3),
        name="attention_latent",
    )(q, k, p, k, p)
    return pl.pallas_call(
        functools.partial(_attn_kernel, group=group, hd=hd, tk=tk, n_chunks=0),
        grid=(st.b, N_KV_HEADS),
        in_specs=[pl.BlockSpec((st.c, gw), lambda b, h: (cb0 + b, h)),
                  pl.BlockSpec((st.c, hd), lambda b, h: (cb0 + b, h)),
                  pl.BlockSpec((st.c, hd), lambda b, h: (cb0 + b, vb + h)),
                  pl.BlockSpec(memory_space=pl.ANY)],
        out_specs=pl.BlockSpec((st.c, gw), lambda b, h: (cb0 + b, h)),
        out_shape=jax.ShapeDtypeStruct((st.rows, aw), BF16),
        input_output_aliases={3: 0},
        compiler_params=_cparams(2),
        name="attention_context",
    )(q, k, p, yc)


def kernel(x, c, ctx, c_ctx, w_mod, b_mod, norm_gain, final_gain, ffn_w_in, ffn_w_out, w_in, b_gate, conv_w, conv_b, filt_w1, filt_b1, filt_w2, filt_b2, filt_w3, filt_b3, filt_w4, filt_freq, hyena_bias, w_pool, pool_scale, q_gain, k_gain, w_up, w_out):
    b, l, d = x.shape
    lc = ctx.shape[1]
    depth = w_mod.shape[0]
    width = hyena_bias.shape[2]
    hd = q_gain.shape[1]
    aw, kvw = N_HEADS * hd, N_KV_HEADS * hd
    a1 = (HYENA_ORDER + 1) * width
    q0 = a1 + width
    k0 = q0 + aw
    v0 = k0 + kvw
    g0 = v0 + kvw
    st = _Stream(b, l, lc)

    xs = jnp.concatenate([x.reshape(b * l, d), ctx.reshape(b * lc, d)], axis=0)
    cvec = jnp.zeros((8, d), F32).at[:b].set(c).at[b].set(c_ctx)
    mods = _modulation(cvec, w_mod, b_mod).reshape(depth, 8, N_MOD, d)[:, :b + 1]
    mods = mods.reshape(depth * (b + 1) * N_MOD, 1, d)
    gains = norm_gain.reshape(depth * 3, 1, d)
    b_gate3 = b_gate.reshape(depth * N_BRANCH, 1, d)
    pool_scale3 = pool_scale.reshape(depth, 1, width)
    w_pool_bf = w_pool.astype(BF16)
    h_lat = _filter_mlp(l, filt_w1, filt_b1, filt_w2, filt_b2, filt_w3, filt_b3, filt_freq)
    h_ctx = _filter_mlp(lc, filt_w1, filt_b1, filt_w2, filt_b2, filt_w3, filt_b3, filt_freq)

    for i in range(depth):
        h = _adaln(st, xs, gains, mods, i, 0, 0, 1)
        a = _ffn_in(st, h, ffn_w_in, i, 0)
        xs = _mm_residual(st, a, ffn_w_out, (i, 0), xs, mods, i, 2, 0.5, 512)
        h = _adaln(st, xs, gains, mods, i, 1, 3, 4)
        p = _in_proj(st, h, w_in, i)
        cw, cb = conv_w[i], conv_b[i].reshape(1, -1)
        u3 = _short_conv_call(p, cw, cb, width, 0, st.rows_lat, l, st.rows, None)
        u3 = _short_conv_call(p, cw, cb, width, st.rows_lat, b * lc, lc, st.rows, u3)
        kk = _filter_taps(h_lat, filt_w4, i, l, width)
        kk_ctx = _filter_taps(h_ctx, filt_w4, i, lc, width)
        z = _long_conv_gate(st, u3, 0, u3, 1, kk, kk_ctx, 0, hyena_bias[i, 0])
        ya = _long_conv_gate(st, z[None], 0, u3, 2, kk, kk_ctx, 1, hyena_bias[i, 1])
        yb = _pool_call(p, a1 // width, w_pool_bf, pool_scale3, i, width, 0, b, l, st.rows, None)
        yb = _pool_call(p, a1 // width, w_pool_bf, pool_scale3, i, width, st.rows_lat, b, lc, st.rows, yb)
        q, k = _qk_prep(st, p, q0, k0, aw, kvw, q_gain, k_gain, i, hd)
        yc = _attention(st, q, k, p, v0, hd)
        merged = _merge(st, ya, yb, yc, p, g0, b_gate3, w_up, i)
        xs = _mm_residual(st, merged, w_out, (i,), xs, mods, i, 5, 1.0, 512)
        h = _adaln(st, xs, gains, mods, i, 2, 6, 7)
        a = _ffn_in(st, h, ffn_w_in, i, 1)
        xs = _mm_residual(st, a, ffn_w_out, (i, 1), xs, mods, i, 8, 0.5, 512)

    out = pl.pallas_call(
        _rmsnorm_kernel,
        grid=(st.n_lat_tiles,),
        in_specs=[pl.BlockSpec((st.tm, d), lambda i: (i, 0)),
                  pl.BlockSpec((1, d), lambda i: (0, 0))],
        out_specs=pl.BlockSpec((st.tm, d), lambda i: (i, 0)),
        out_shape=jax.ShapeDtypeStruct((b * l, d), F32),
        compiler_params=_cparams(1),
        name="final_rmsnorm",
    )(xs, final_gain.reshape(1, d))
    return out.reshape(b, l, d)
```

```python
import functools
import math

import numpy as np
import jax
import jax.numpy as jnp
from jax import lax
from jax.experimental import pallas as pl
from jax.experimental.pallas import tpu as pltpu

F32 = jnp.float32
BF16 = jnp.bfloat16

N_HEADS = 16
N_KV_HEADS = 4
GRID_W = 64
ROPE_THETA = 10000.0
HYENA_ORDER = 2
HYENA_TARGET = 1e-2
HYENA_FAST_DECAY = 0.3
HYENA_SLOW_DECAY = 1.5
POOL_WINDOWS = (2, 4, 8, 16)
N_MOD = 9
N_BRANCH = 3
EPS = 1e-6
HIGHEST = lax.Precision.HIGHEST

VMEM_LIMIT_V7X = 56 * 1024 * 1024


def _cparams(n_axes):
    return pltpu.CompilerParams(
        dimension_semantics=("arbitrary",) * n_axes, vmem_limit_bytes=VMEM_LIMIT_V7X)


def _largest_divisor(total, pref, align):
    if total <= pref:
        return total
    t = (pref // align) * align
    while t > align and total % t:
        t -= align
    assert total % t == 0, (total, pref, align)
    return t


def _silu(v):
    return v * jax.nn.sigmoid(v)


def _mod_kernel(c_ref, w_ref, b_ref, o_ref):
    s = _silu(c_ref[...]).astype(BF16)
    w = w_ref[0].astype(BF16)
    o_ref[0] = jnp.dot(s, w, preferred_element_type=F32) + b_ref[0]


def _modulation(cvec, w_mod, b_mod):
    depth, d, nm = w_mod.shape
    tn = _largest_divisor(nm, 1024, 128)
    return pl.pallas_call(
        _mod_kernel,
        grid=(depth, nm // tn),
        in_specs=[pl.BlockSpec((8, d), lambda l, n: (0, 0)),
                  pl.BlockSpec((1, d, tn), lambda l, n: (l, 0, n)),
                  pl.BlockSpec((1, 1, tn), lambda l, n: (l, 0, n))],
        out_specs=pl.BlockSpec((1, 8, tn), lambda l, n: (l, 0, n)),
        out_shape=jax.ShapeDtypeStruct((depth, 8, nm), F32),
        compiler_params=_cparams(2),
        name="modulation",
    )(cvec, w_mod, b_mod.reshape(depth, 1, nm))


def _adaln_kernel(x_ref, g_ref, sh_ref, sc_ref, o_ref):
    x = x_ref[...]
    y = x * lax.rsqrt(jnp.mean(x * x, axis=-1, keepdims=True) + EPS)
    y = y * g_ref[0]
    o_ref[...] = (y * (1.0 + sc_ref[0]) + sh_ref[0]).astype(o_ref.dtype)


def _rmsnorm_kernel(x_ref, g_ref, o_ref):
    x = x_ref[...]
    y = x * lax.rsqrt(jnp.mean(x * x, axis=-1, keepdims=True) + EPS)
    o_ref[...] = y * g_ref[...]


class _Stream:
    def __init__(self, batch, seq, ctx_len):
        self.b, self.l, self.c = batch, seq, ctx_len
        self.rows_lat = batch * seq
        self.rows = batch * (seq + ctx_len)
        self.tm = _largest_divisor(math.gcd(seq, batch * ctx_len), 512, 16)
        self.tiles_per_seq = seq // self.tm
        self.n_lat_tiles = self.rows_lat // self.tm
        self.n_tiles = self.rows // self.tm

    def group(self, i):
        return jnp.where(i < self.n_lat_tiles, i // self.tiles_per_seq, self.b)


def _adaln(st, x, gains, mods, layer, j, m_shift, m_scale):
    d = x.shape[1]
    base = layer * (st.b + 1) * N_MOD

    def mod_map(m):
        return lambda i: (base + st.group(i) * N_MOD + m, 0, 0)

    return pl.pallas_call(
        _adaln_kernel,
        grid=(st.n_tiles,),
        in_specs=[pl.BlockSpec((st.tm, d), lambda i: (i, 0)),
                  pl.BlockSpec((1, 1, d), lambda i: (layer * 3 + j, 0, 0)),
                  pl.BlockSpec((1, 1, d), mod_map(m_shift)),
                  pl.BlockSpec((1, 1, d), mod_map(m_scale))],
        out_specs=pl.BlockSpec((st.tm, d), lambda i: (i, 0)),
        out_shape=jax.ShapeDtypeStruct(x.shape, BF16),
        compiler_params=_cparams(1),
        name="adaln",
    )(x, gains, mods, mods)


def _ffn_in_kernel(h_ref, wg_ref, wu_ref, o_ref, wg_s, wu_s):
    @pl.when(pl.program_id(1) == 0)
    def _():
        wg_s[...] = wg_ref[...].astype(BF16)
        wu_s[...] = wu_ref[...].astype(BF16)

    h = h_ref[...]
    g = jnp.dot(h, wg_s[...], preferred_element_type=F32)
    u = jnp.dot(h, wu_s[...], preferred_element_type=F32)
    o_ref[...] = (_silu(g) * u).astype(o_ref.dtype)


def _ffn_in(st, h, ffn_w_in, layer, j):
    d = h.shape[1]
    f = ffn_w_in.shape[3] // 2
    tn = _largest_divisor(f, 512, 128)
    nt = f // tn
    return pl.pallas_call(
        _ffn_in_kernel,
        grid=(nt, st.n_tiles),
        in_specs=[pl.BlockSpec((st.tm, d), lambda n, m: (m, 0)),
                  pl.BlockSpec((None, None, d, tn), lambda n, m: (layer, j, 0, n)),
                  pl.BlockSpec((None, None, d, tn), lambda n, m: (layer, j, 0, n + nt))],
        out_specs=pl.BlockSpec((st.tm, tn), lambda n, m: (m, n)),
        out_shape=jax.ShapeDtypeStruct((st.rows, f), BF16),
        scratch_shapes=[pltpu.VMEM((d, tn), BF16), pltpu.VMEM((d, tn), BF16)],
        compiler_params=_cparams(2),
        name="ffn_in",
    )(h, ffn_w_in, ffn_w_in)


def _mm_res_kernel(a_ref, w_ref, x_ref, gate_ref, o_ref, w_s, *, gate_scale):
    @pl.when(pl.program_id(1) == 0)
    def _():
        w_s[...] = w_ref[...].astype(BF16)

    y = jnp.dot(a_ref[...], w_s[...], preferred_element_type=F32)
    o_ref[...] = x_ref[...] + (gate_scale * gate_ref[0]) * y


def _mm_residual(st, a, w, w_index, x, mods, layer, m_gate, gate_scale, tn_pref):
    k = a.shape[1]
    d = x.shape[1]
    tn = _largest_divisor(d, tn_pref, 128)
    base = layer * (st.b + 1) * N_MOD
    lead = (None,) * len(w_index)
    return pl.pallas_call(
        functools.partial(_mm_res_kernel, gate_scale=gate_scale),
        grid=(d // tn, st.n_tiles),
        in_specs=[pl.BlockSpec((st.tm, k), lambda n, m: (m, 0)),
                  pl.BlockSpec(lead + (k, tn), lambda n, m: tuple(w_index) + (0, n)),
                  pl.BlockSpec((st.tm, tn), lambda n, m: (m, n)),
                  pl.BlockSpec((1, 1, tn), lambda n, m: (base + st.group(m) * N_MOD + m_gate, 0, n))],
        out_specs=pl.BlockSpec((st.tm, tn), lambda n, m: (m, n)),
        out_shape=jax.ShapeDtypeStruct(x.shape, F32),
        scratch_shapes=[pltpu.VMEM((k, tn), BF16)],
        compiler_params=_cparams(2),
        name="matmul_residual",
    )(a, w, x, mods)


def _mm_kernel(h_ref, w_ref, o_ref, w_s):
    @pl.when(pl.program_id(1) == 0)
    def _():
        w_s[...] = w_ref[...].astype(BF16)

    o_ref[...] = jnp.dot(h_ref[...], w_s[...], preferred_element_type=F32).astype(o_ref.dtype)


def _in_proj(st, h, w_in, layer):
    d = h.shape[1]
    n_in = w_in.shape[2]
    tn = _largest_divisor(n_in, 1024, 128)
    return pl.pallas_call(
        _mm_kernel,
        grid=(n_in // tn, st.n_tiles),
        in_specs=[pl.BlockSpec((st.tm, d), lambda n, m: (m, 0)),
                  pl.BlockSpec((None, d, tn), lambda n, m: (layer, 0, n))],
        out_specs=pl.BlockSpec((st.tm, tn), lambda n, m: (m, n)),
        out_shape=jax.ShapeDtypeStruct((st.rows, n_in), BF16),
        scratch_shapes=[pltpu.VMEM((d, tn), BF16)],
        compiler_params=_cparams(2),
        name="in_proj",
    )(h, w_in)


def _merge_kernel(ya_ref, yb_ref, yc_ref, pa_ref, pb_ref, pc_ref, ba_ref, bb_ref, bc_ref,
                  w_ref, o_ref, w_s):
    @pl.when(pl.program_id(1) == 0)
    def _():
        w_s[...] = w_ref[...].astype(BF16)

    acc = None
    for k, (y_ref, p_ref, b_ref) in enumerate(
            ((ya_ref, pa_ref, ba_ref), (yb_ref, pb_ref, bb_ref), (yc_ref, pc_ref, bc_ref))):
        gate = jax.nn.sigmoid(p_ref[...].astype(F32) + b_ref[0])
        t = gate * jnp.dot(y_ref[...], w_s[k], preferred_element_type=F32)
        acc = t if acc is None else acc + t
    o_ref[...] = acc.astype(o_ref.dtype)


def _merge(st, ya, yb, yc, p, gate_col0, b_gate3, w_up, layer):
    w = ya.shape[1]
    d = w_up.shape[3]
    tn = _largest_divisor(d, 256, 128)
    gb = gate_col0 // tn
    dt = d // tn
    y_spec = pl.BlockSpec((st.tm, w), lambda n, m: (m, 0))

    def p_spec(k):
        return pl.BlockSpec((st.tm, tn), lambda n, m: (m, gb + k * dt + n))

    def b_spec(k):
        return pl.BlockSpec((1, 1, tn), lambda n, m: (layer * N_BRANCH + k, 0, n))

    return pl.pallas_call(
        _merge_kernel,
        grid=(dt, st.n_tiles),
        in_specs=[y_spec, y_spec, y_spec, p_spec(0), p_spec(1), p_spec(2),
                  b_spec(0), b_spec(1), b_spec(2),
                  pl.BlockSpec((None, N_BRANCH, w, tn), lambda n, m: (layer, 0, 0, n))],
        out_specs=pl.BlockSpec((st.tm, tn), lambda n, m: (m, n)),
        out_shape=jax.ShapeDtypeStruct((st.rows, d), BF16),
        scratch_shapes=[pltpu.VMEM((N_BRANCH, w, tn), BF16)],
        compiler_params=_cparams(2),
        name="merge",
    )(ya, yb, yc, p, p, p, b_gate3, b_gate3, b_gate3, w_up)


HALO = 16


def _short_conv_kernel(cur_ref, prev_ref, next_ref, w_ref, b_ref, o_ref, *, blocks_per_seq):
    j = pl.program_id(1) % blocks_per_seq
    u = cur_ref[...].astype(F32)
    t = u.shape[0]
    row = lax.broadcasted_iota(jnp.int32, u.shape, 0)
    prev_row = jnp.where(j == 0, 0.0, prev_ref[HALO - 1:HALO, :].astype(F32))
    next_row = jnp.where(j == blocks_per_seq - 1, 0.0, next_ref[0:1, :].astype(F32))
    up = jnp.where(row == 0, prev_row, pltpu.roll(u, 1, axis=0))
    un = jnp.where(row == t - 1, next_row, pltpu.roll(u, t - 1, axis=0))
    w = w_ref[...]
    o_ref[...] = (b_ref[...] + up * w[0:1] + u * w[1:2] + un * w[2:3]).astype(o_ref.dtype)


def _short_conv_call(p, conv_w, conv_b, width, row0, n_rows, seq_len, out_rows, prior):
    ts = _largest_divisor(seq_len, 512, HALO)
    ct = _largest_divisor(width, 1024, 128)
    blocks_per_seq = seq_len // ts
    rb0 = row0 // ts
    hb = ts // HALO
    n_hblocks = out_rows // HALO
    cpb = width // ct
    n_parts = HYENA_ORDER + 1

    def cur_map(c, r):
        return (rb0 + r, c)

    def prev_map(c, r):
        return (jnp.maximum((rb0 + r) * hb - 1, 0), c)

    def next_map(c, r):
        return (jnp.minimum((rb0 + r + 1) * hb, n_hblocks - 1), c)

    in_specs = [pl.BlockSpec((ts, ct), cur_map),
                pl.BlockSpec((HALO, ct), prev_map),
                pl.BlockSpec((HALO, ct), next_map),
                pl.BlockSpec((3, ct), lambda c, r: (0, c)),
                pl.BlockSpec((1, ct), lambda c, r: (0, c))]
    args = [p, p, p, conv_w, conv_b]
    aliases = {}
    kernel = functools.partial(_short_conv_kernel, blocks_per_seq=blocks_per_seq)
    if prior is not None:
        in_specs.append(pl.BlockSpec(memory_space=pl.ANY))
        args.append(prior)
        aliases = {5: 0}
        body = kernel
        kernel = lambda c, pv, nx, w, b, _prior, o: body(c, pv, nx, w, b, o)
    return pl.pallas_call(
        kernel,
        grid=(n_parts * cpb, n_rows // ts),
        in_specs=in_specs,
        out_specs=pl.BlockSpec((None, ts, ct), lambda c, r: (c // cpb, rb0 + r, c % cpb)),
        out_shape=jax.ShapeDtypeStruct((n_parts, out_rows, width), BF16),
        input_output_aliases=aliases,
        compiler_params=_cparams(2),
        name="short_conv",
    )(*args)


@functools.lru_cache(maxsize=None)
def _filter_positions(seq_len, emb_dim, pad_dim):
    bands = (emb_dim - 1) // 2
    j = np.arange(seq_len, dtype=np.float64)
    t = j / (seq_len - 1)
    wpos = 2.0 * np.pi * j / seq_len
    f = np.linspace(1e-4, bands - 1, bands)
    z = np.concatenate([t[:, None], np.cos(f[None] * wpos[:, None]), -np.sin(f[None] * wpos[:, None])], axis=1)
    lag = np.concatenate([np.arange(seq_len), [0], np.arange(seq_len - 1, 0, -1)])
    z2 = np.zeros((2 * seq_len, pad_dim), np.float64)
    z2[:, :emb_dim] = z[lag]
    return z2.astype(np.float32)


def _filter_mlp_kernel(z_ref, w1_ref, b1_ref, w2_ref, b2_ref, w3_ref, b3_ref, fr_ref, o_ref):
    fr = fr_ref[0]
    h = jnp.dot(z_ref[...], w1_ref[0], precision=HIGHEST, preferred_element_type=F32)
    h = jnp.sin(fr[0:1] * (h + b1_ref[0]))
    h = jnp.dot(h, w2_ref[0], precision=HIGHEST, preferred_element_type=F32)
    h = jnp.sin(fr[1:2] * (h + b2_ref[0]))
    h = jnp.dot(h, w3_ref[0], precision=HIGHEST, preferred_element_type=F32)
    o_ref[0] = jnp.sin(fr[2:3] * (h + b3_ref[0]))


def _filter_mlp(seq_len, w1, b1, w2, b2, w3, b3, freq):
    depth, emb, hid = w1.shape
    pad = -(-emb // 8) * 8
    n = 2 * seq_len
    z2 = jnp.asarray(_filter_positions(seq_len, emb, pad))
    w1p = jnp.pad(w1, ((0, 0), (0, pad - emb), (0, 0)))
    tr = _largest_divisor(n, 1024, 8)
    wspec = lambda k: pl.BlockSpec((1, k, hid), lambda l, r: (l, 0, 0))
    bspec = pl.BlockSpec((1, 1, hid), lambda l, r: (l, 0, 0))
    return pl.pallas_call(
        _filter_mlp_kernel,
        grid=(depth, n // tr),
        in_specs=[pl.BlockSpec((tr, pad), lambda l, r: (r, 0)),
                  wspec(pad), bspec, wspec(hid), bspec, wspec(hid), bspec,
                  pl.BlockSpec((1, 3, hid), lambda l, r: (l, 0, 0))],
        out_specs=pl.BlockSpec((1, tr, hid), lambda l, r: (l, r, 0)),
        out_shape=jax.ShapeDtypeStruct((depth, n, hid), F32),
        compiler_params=_cparams(2),
        name="filter_mlp",
    )(z2, w1p, b1.reshape(depth, 1, hid), w2, b2.reshape(depth, 1, hid), w3, b3.reshape(depth, 1, hid), freq)


def _filter_taps_kernel(h_ref, wf_ref, wb_ref, delta_ref, o_ref, *, seq_len):
    l = seq_len
    delta = delta_ref[...]
    lag = lax.broadcasted_iota(jnp.int32, (l, 1), 0)
    t_top = lag.astype(F32) / (l - 1.0)
    t_bot = jnp.where(lag == 0, 0, l - lag).astype(F32) / (l - 1.0)
    top = jnp.dot(h_ref[0, 0:l, :], wf_ref[0], precision=HIGHEST, preferred_element_type=F32)
    bot = jnp.dot(h_ref[0, l:2 * l, :], wb_ref[0], precision=HIGHEST, preferred_element_type=F32)
    top = top * jnp.exp(-t_top * delta)
    bot = bot * jnp.exp(-t_bot * delta)
    top = top + jnp.where(lag == 0, bot[0:1, :], 0.0)
    bot = jnp.where(lag == 0, 0.0, bot)
    ss = jnp.sum(top * top, axis=0, keepdims=True) + jnp.sum(bot * bot, axis=0, keepdims=True)
    scale = lax.rsqrt(ss + EPS)
    o_ref[0, 0:l, :] = (top * scale).astype(o_ref.dtype)
    o_ref[0, l:2 * l, :] = (bot * scale).astype(o_ref.dtype)


def _filter_taps(h_all, filt_w4, layer, seq_len, width):
    hid = h_all.shape[2]
    n = 2 * seq_len
    ct = 128
    cpb = width // ct
    min_decay = math.log(HYENA_TARGET) / HYENA_SLOW_DECAY
    max_decay = math.log(HYENA_TARGET) / HYENA_FAST_DECAY
    delta = jnp.asarray(np.abs(np.linspace(min_decay, max_decay, width)).astype(np.float32)).reshape(1, width)
    return pl.pallas_call(
        functools.partial(_filter_taps_kernel, seq_len=seq_len),
        grid=(HYENA_ORDER, cpb),
        in_specs=[pl.BlockSpec((1, n, hid), lambda o, c: (layer, 0, 0)),
                  pl.BlockSpec((1, hid, ct), lambda o, c: (layer, 0, (2 * o) * cpb + c)),
                  pl.BlockSpec((1, hid, ct), lambda o, c: (layer, 0, (2 * o + 1) * cpb + c)),
                  pl.BlockSpec((1, ct), lambda o, c: (0, c))],
        out_specs=pl.BlockSpec((1, n, ct), lambda o, c: (o, 0, c)),
        out_shape=jax.ShapeDtypeStruct((HYENA_ORDER, n, width), BF16),
        compiler_params=_cparams(2),
        name="filter_taps",
    )(h_all, filt_w4, filt_w4, delta)


def _real_form(e):
    return np.block([[e.real, -e.imag], [e.imag, e.real]])


@functools.lru_cache(maxsize=None)
def _dft_tables(n1, n2):
    n = n1 * n2
    j2 = np.arange(n2)[:, None, None]
    k1 = np.arange(n1)[None, :, None]
    j1 = np.arange(n1)[None, None, :]
    e = np.exp(-2j * np.pi * ((j1 * k1) / n1 + (j2 * k1) / n))
    eh = e[:, :, :n1 // 2]
    t_data = np.concatenate(
        [np.concatenate([eh.real, -eh.imag], axis=2), np.concatenate([eh.imag, eh.real], axis=2)], axis=1)
    t_filt = np.concatenate([e.real, e.imag], axis=1)
    hinv = np.conj(np.transpose(eh, (0, 2, 1))) / n
    t_inv = np.concatenate(
        [np.concatenate([hinv.real, -hinv.imag], axis=2), np.concatenate([hinv.imag, hinv.real], axis=2)], axis=1)
    f2 = np.exp(-2j * np.pi * np.outer(np.arange(n2), np.arange(n2)) / n2)
    as32 = lambda a: np.ascontiguousarray(a, dtype=np.float32)
    return as32(t_data), as32(t_filt), as32(t_inv), as32(_real_form(f2)), as32(_real_form(np.conj(f2)))


def _split_n(n):
    n2 = 1
    while n2 < 128 and n // (n2 * 2) >= 64:
        n2 *= 2
    return n // n2, n2


GS = 16
LANES = 128


def _spread(src_ref, dst_ref):
    a, gs, _ = src_ref.shape
    for c in range(dst_ref.shape[0]):
        dst_ref[c] = src_ref[:, :, c * LANES:(c + 1) * LANES].reshape(a * gs, LANES).astype(F32)


def _gather(s_ref, g, count, stride):
    return jnp.concatenate(
        [s_ref[c, pl.ds(g, count, stride=stride), :] for c in range(s_ref.shape[0])], axis=1)


def _fft_s1_kernel(z_ref, t_ref, o_ref, s_ref):
    n1, gs, _ = z_ref.shape
    _spread(z_ref, s_ref)
    for g in range(gs):
        z = _gather(s_ref, g, n1, gs).astype(BF16)
        o_ref[g] = jnp.dot(t_ref[g], z, preferred_element_type=F32).astype(o_ref.dtype)


def _fft_mid_kernel(ar_ref, ai_ref, fr_ref, fi_ref, f2_ref, f2i_ref, o_ref, sar, sai, sfr, sfi):
    n2, gs, _ = ar_ref.shape
    for src, dst in ((ar_ref, sar), (ai_ref, sai), (fr_ref, sfr), (fi_ref, sfi)):
        _spread(src, dst)
    f2 = f2_ref[...]
    f2i = f2i_ref[...]
    for g in range(gs):
        a = jnp.concatenate([_gather(sar, g, n2, gs), _gather(sai, g, n2, gs)], axis=0).astype(BF16)
        f = jnp.concatenate([_gather(sfr, g, n2, gs), _gather(sfi, g, n2, gs)], axis=0).astype(BF16)
        u = jnp.dot(f2, a, preferred_element_type=F32)
        k = jnp.dot(f2, f, preferred_element_type=F32)
        ur, ui, kr, ki = u[:n2], u[n2:], k[:n2], k[n2:]
        y = jnp.concatenate([ur * kr - ui * ki, ur * ki + ui * kr], axis=0).astype(BF16)
        o_ref[g] = jnp.dot(f2i, y, preferred_element_type=F32).astype(o_ref.dtype)


def _fft_s3_kernel(cr_ref, ci_ref, t_ref, u_ref, x_ref, bias_ref, o_ref, scr, sci, sy):
    n1, gs, _ = cr_ref.shape
    _spread(cr_ref, scr)
    _spread(ci_ref, sci)
    for g in range(gs):
        c = jnp.concatenate([_gather(scr, g, n1, gs), _gather(sci, g, n1, gs)], axis=0).astype(BF16)
        y = jnp.dot(t_ref[g], c, preferred_element_type=F32)
        for cc in range(sy.shape[0]):
            sy[cc, g * n1:(g + 1) * n1, :] = y[:, cc * LANES:(cc + 1) * LANES]
    bias = bias_ref[...]
    for r in range(n1):
        y = _gather(sy, r, gs, n1)
        u = u_ref[r].astype(F32)
        o_ref[r] = (x_ref[r].astype(F32) * (y + u * bias)).astype(o_ref.dtype)


def _hyena_small_kernel(u_ref, k_ref, x_ref, bias_ref, td_ref, tf_ref, ti_ref, _latent_rows_ref, o_ref, *, n):
    u = u_ref[...]
    a = jnp.dot(td_ref[...], u, preferred_element_type=F32)
    k = jnp.dot(tf_ref[...], k_ref[...], preferred_element_type=F32)
    ar, ai, kr, ki = a[:n], a[n:], k[:n], k[n:]
    y = jnp.concatenate([ar * kr - ai * ki, ar * ki + ai * kr], axis=0).astype(BF16)
    y = jnp.dot(ti_ref[...], y, preferred_element_type=F32)
    o_ref[...] = (x_ref[...].astype(F32) * (y + u.astype(F32) * bias_ref[...])).astype(o_ref.dtype)


def _bf16_table(a):
    return jnp.asarray(a).astype(BF16)


def _long_conv_gate(st, u3, u_part, x_arr, x_part, kk, kk_ctx, order, bias):
    assert st.b == 2, "the two batch elements are packed as one complex signal"
    _, rows, width = u3.shape
    l, lc = st.l, st.c
    n = 2 * l
    n1, n2 = _split_n(n)
    t_data, t_filt, t_inv, f2, f2i = (_bf16_table(a) for a in _dft_tables(n1, n2))
    assert n1 % GS == 0 and n2 % GS == 0 and width % LANES == 0
    bias2 = bias.reshape(1, width)
    slabs = rows // n2
    nl = width // LANES

    def stage1(z4, part, table, name):
        return pl.pallas_call(
            _fft_s1_kernel,
            grid=(n2 // GS,),
            in_specs=[pl.BlockSpec((None, n1, GS, width), lambda j: (part, 0, j, 0)),
                      pl.BlockSpec((GS, 2 * n1, n1), lambda j: (j, 0, 0))],
            out_specs=pl.BlockSpec((GS, 2 * n1, width), lambda j: (j, 0, 0)),
            out_shape=jax.ShapeDtypeStruct((n2, 2 * n1, width), BF16),
            scratch_shapes=[pltpu.VMEM((nl, n1 * GS, LANES), F32)],
            compiler_params=_cparams(1),
            name=name,
        )(z4, table)

    a_filt = stage1(kk.reshape(HYENA_ORDER, n1, n2, width), order, t_filt, "fft_stage1_filter")
    u_v = u3.reshape(u3.shape[0], slabs, n2, width)
    a_data = stage1(u_v, u_part, t_data, "fft_stage1_data")

    ctm = _largest_divisor(width, 512, LANES)
    kb_im = n1 // GS
    re_spec = pl.BlockSpec((n2, GS, ctm), lambda c, k: (0, k, c))
    im_spec = pl.BlockSpec((n2, GS, ctm), lambda c, k: (0, kb_im + k, c))
    mat_spec = pl.BlockSpec((2 * n2, 2 * n2), lambda c, k: (0, 0))
    c_mid = pl.pallas_call(
        _fft_mid_kernel,
        grid=(width // ctm, n1 // GS),
        in_specs=[re_spec, im_spec, re_spec, im_spec, mat_spec, mat_spec],
        out_specs=pl.BlockSpec((GS, 2 * n2, ctm), lambda c, k: (k, 0, c)),
        out_shape=jax.ShapeDtypeStruct((n1, 2 * n2, width), BF16),
        scratch_shapes=[pltpu.VMEM((ctm // LANES, n2 * GS, LANES), F32)] * 4,
        compiler_params=_cparams(2),
        name="fft_mid",
    )(a_data, a_data, a_filt, a_filt, f2, f2i)

    cts = _largest_divisor(width, 1024, LANES)
    jb_im = n2 // GS
    x_v = x_arr.reshape(x_arr.shape[0], slabs, n2, width)
    out = pl.pallas_call(
        _fft_s3_kernel,
        grid=(width // cts, n2 // GS),
        in_specs=[pl.BlockSpec((n1, GS, cts), lambda c, j: (0, j, c)),
                  pl.BlockSpec((n1, GS, cts), lambda c, j: (0, jb_im + j, c)),
                  pl.BlockSpec((GS, n1, 2 * n1), lambda c, j: (j, 0, 0)),
                  pl.BlockSpec((None, n1, GS, cts), lambda c, j: (u_part, 0, j, c)),
                  pl.BlockSpec((None, n1, GS, cts), lambda c, j: (x_part, 0, j, c)),
                  pl.BlockSpec((1, cts), lambda c, j: (0, c))],
        out_specs=pl.BlockSpec((n1, GS, cts), lambda c, j: (0, j, c)),
        out_shape=jax.ShapeDtypeStruct((slabs, n2, width), BF16),
        scratch_shapes=[pltpu.VMEM((cts // LANES, n1 * GS, LANES), F32)] * 3,
        compiler_params=_cparams(2),
        name="fft_stage3",
    )(c_mid, c_mid, t_inv, u_v, x_v, bias2)
    out = out.reshape(rows, width)

    nc = 2 * lc
    tc_data, tc_filt, tc_inv, _, _ = (_bf16_table(a) for a in _dft_tables(nc, 1))
    ct = _largest_divisor(width, 512, 128)
    rb = (st.rows_lat) // (2 * lc)
    out = pl.pallas_call(
        functools.partial(_hyena_small_kernel, n=nc),
        grid=(width // ct,),
        in_specs=[pl.BlockSpec((None, 2 * lc, ct), lambda c: (u_part, rb, c)),
                  pl.BlockSpec((None, nc, ct), lambda c: (order, 0, c)),
                  pl.BlockSpec((None, 2 * lc, ct), lambda c: (x_part, rb, c)),
                  pl.BlockSpec((1, ct), lambda c: (0, c)),
                  pl.BlockSpec((2 * nc, nc), lambda c: (0, 0)),
                  pl.BlockSpec((2 * nc, nc), lambda c: (0, 0)),
                  pl.BlockSpec((nc, 2 * nc), lambda c: (0, 0)),
                  pl.BlockSpec(memory_space=pl.ANY)],
        out_specs=pl.BlockSpec((2 * lc, ct), lambda c: (rb, c)),
        out_shape=jax.ShapeDtypeStruct((rows, width), BF16),
        input_output_aliases={7: 0},
        compiler_params=_cparams(1),
        name="hyena_context",
    )(u3, kk_ctx, x_arr, bias2, tc_data[0], tc_filt[0], tc_inv[0], out)
    return out


@functools.lru_cache(maxsize=None)
def _pool_tables(t, windows):
    out = np.zeros((4, len(windows), t, 3 * t), np.float64)
    for v in range(4):
        lo_bound = t if v & 1 else 0
        hi_bound = 2 * t - 1 if v & 2 else 3 * t - 1
        for g, w in enumerate(windows):
            for r in range(t):
                pos = t + r
                lo = max(pos - w // 2, lo_bound)
                hi = min(pos + w - 1 - w // 2, hi_bound)
                out[v, g, r, lo:hi + 1] = 1.0 / (hi - lo + 1)
                out[v, g, r, pos] -= 1.0
    return out.astype(np.float32)


def _pool_kernel(prev_ref, cur_ref, next_ref, m_ref, w_ref, s_ref, *rest, n_groups, gw):
    o_ref = rest[-1]
    for g in range(n_groups):
        sl = slice(g * gw, (g + 1) * gw)
        u = jnp.concatenate([prev_ref[:, sl], cur_ref[:, sl], next_ref[:, sl]], axis=0)
        d = jnp.dot(m_ref[0, g], u, preferred_element_type=F32)
        y = jnp.dot(d.astype(BF16), w_ref[g], preferred_element_type=F32)
        o_ref[:, sl] = (y * s_ref[:, sl]).astype(o_ref.dtype)


def _pool_call(p, col_block, w_pool_bf, pool_scale3, layer, width, row0, n_seq, seq_len, out_rows, prior):
    t = _largest_divisor(seq_len, 256, 16)
    nblk = seq_len // t
    rb0 = row0 // t
    n_groups = len(POOL_WINDOWS)
    gw = width // n_groups
    tables = _bf16_table(_pool_tables(t, POOL_WINDOWS))

    def blk(delta):
        return lambda s, j: (rb0 + s * nblk + jnp.clip(j + delta, 0, nblk - 1), col_block)

    def variant(s, j):
        return ((j == 0).astype(jnp.int32) + 2 * (j == nblk - 1).astype(jnp.int32), 0, 0, 0)

    in_specs = [pl.BlockSpec((t, width), blk(-1)),
                pl.BlockSpec((t, width), blk(0)),
                pl.BlockSpec((t, width), blk(1)),
                pl.BlockSpec((1, n_groups, t, 3 * t), variant),
                pl.BlockSpec((None, n_groups, gw, gw), lambda s, j: (layer, 0, 0, 0)),
                pl.BlockSpec((None, 1, width), lambda s, j: (layer, 0, 0))]
    args = [p, p, p, tables, w_pool_bf, pool_scale3]
    aliases = {}
    if prior is not None:
        in_specs.append(pl.BlockSpec(memory_space=pl.ANY))
        args.append(prior)
        aliases = {6: 0}
    return pl.pallas_call(
        functools.partial(_pool_kernel, n_groups=n_groups, gw=gw),
        grid=(n_seq, nblk),
        in_specs=in_specs,
        out_specs=pl.BlockSpec((t, width), lambda s, j: (rb0 + s * nblk + j, 0)),
        out_shape=jax.ShapeDtypeStruct((out_rows, width), BF16),
        input_output_aliases=aliases,
        compiler_params=_cparams(2),
        name="pool",
    )(*args)


@functools.lru_cache(maxsize=None)
def _rope_tables(seq_len, grid_w, hd, pad_rows):
    axis = hd // 2
    pos = np.arange(seq_len)
    inv = ROPE_THETA ** (-np.arange(0, axis, 2, dtype=np.float64) / axis)
    ang_r = (pos // grid_w)[:, None] * inv[None]
    ang_c = (pos % grid_w)[:, None] * inv[None]
    cos = np.concatenate([np.cos(ang_r), np.cos(ang_r), np.cos(ang_c), np.cos(ang_c)], axis=1)
    sin = np.concatenate([-np.sin(ang_r), np.sin(ang_r), -np.sin(ang_c), np.sin(ang_c)], axis=1)
    cos = np.concatenate([cos, np.ones((pad_rows, hd))], axis=0)
    sin = np.concatenate([sin, np.zeros((pad_rows, hd))], axis=0)
    return cos.astype(np.float32), sin.astype(np.float32)


def _qk_prep_kernel(q_ref, k_ref, qg_ref, kg_ref, cos_ref, sin_ref, qo_ref, ko_ref, *, hd, q_scale):
    cos = cos_ref[...]
    sin = sin_ref[...]
    lane = lax.broadcasted_iota(jnp.int32, cos.shape, 1)
    first = (lane % (hd // 2)) < (hd // 4)

    def prep(x_ref, g_ref, o_ref, mul):
        for h in range(x_ref.shape[1] // hd):
            sl = slice(h * hd, (h + 1) * hd)
            x = x_ref[:, sl].astype(F32)
            y = x * lax.rsqrt(jnp.mean(x * x, axis=-1, keepdims=True) + EPS) * g_ref[...]
            partner = jnp.where(first, pltpu.roll(y, hd - hd // 4, axis=1), pltpu.roll(y, hd // 4, axis=1))
            o_ref[:, sl] = ((y * cos + partner * sin) * mul).astype(o_ref.dtype)

    prep(q_ref, qg_ref, qo_ref, q_scale)
    prep(k_ref, kg_ref, ko_ref, 1.0)


def _qk_prep(st, p, q_col0, k_col0, aw, kvw, q_gain, k_gain, layer, hd):
    cos_np, sin_np = _rope_tables(st.l, GRID_W, hd, st.tm)
    tab_map = lambda i: (jnp.where(i < st.n_lat_tiles, i % st.tiles_per_seq, st.tiles_per_seq), 0)
    return pl.pallas_call(
        functools.partial(_qk_prep_kernel, hd=hd, q_scale=hd ** -0.5),
        grid=(st.n_tiles,),
        in_specs=[pl.BlockSpec((st.tm, aw), lambda i: (i, q_col0 // aw)),
                  pl.BlockSpec((st.tm, kvw), lambda i: (i, k_col0 // kvw)),
                  pl.BlockSpec((None, 1, hd), lambda i: (layer, 0, 0)),
                  pl.BlockSpec((None, 1, hd), lambda i: (layer, 0, 0)),
                  pl.BlockSpec((st.tm, hd), tab_map),
                  pl.BlockSpec((st.tm, hd), tab_map)],
        out_specs=[pl.BlockSpec((st.tm, aw), lambda i: (i, 0)),
                   pl.BlockSpec((st.tm, kvw), lambda i: (i, 0))],
        out_shape=[jax.ShapeDtypeStruct((st.rows, aw), BF16),
                   jax.ShapeDtypeStruct((st.rows, kvw), BF16)],
        compiler_params=_cparams(1),
        name="qk_prep",
    )(p, p, q_gain.reshape(-1, 1, hd), k_gain.reshape(-1, 1, hd), jnp.asarray(cos_np), jnp.asarray(sin_np))


def _attn_kernel(*refs, group, hd, tk, n_chunks):
    q_ref, kc_ref, vc_ref = refs[:3]
    o_ref = refs[-1]
    q = q_ref[...]
    tq = q.shape[0]
    qs = jnp.concatenate([q[:, g * hd:(g + 1) * hd] for g in range(group)], axis=0)

    def update(carry, k, v):
        m, l, acc = carry
        s = lax.dot_general(qs, k, (((1,), (1,)), ((), ())), preferred_element_type=F32)
        m_new = jnp.maximum(m, jnp.max(s, axis=-1, keepdims=True))
        alpha = jnp.exp(m - m_new)
        pexp = jnp.exp(s - m_new)
        l = alpha * l + jnp.sum(pexp, axis=-1, keepdims=True)
        acc = alpha * acc + jnp.dot(pexp.astype(BF16), v, preferred_element_type=F32)
        return m_new, l, acc

    rows = group * tq
    carry = (jnp.full((rows, 1), -1e30, F32), jnp.zeros((rows, 1), F32), jnp.zeros((rows, hd), F32))
    carry = update(carry, kc_ref[...], vc_ref[...])
    if n_chunks:
        kl_ref, vl_ref = refs[3:5]

        def body(i, c):
            start = pl.multiple_of(i * tk, tk)
            return update(c, kl_ref[pl.ds(start, tk), :], vl_ref[pl.ds(start, tk), :])

        carry = lax.fori_loop(0, n_chunks, body, carry)
    _, l, acc = carry
    o = acc / l
    o_ref[...] = jnp.concatenate([o[g * tq:(g + 1) * tq] for g in range(group)], axis=1).astype(o_ref.dtype)


def _attention(st, q, k, p, v_col0, hd):
    group = N_HEADS // N_KV_HEADS
    gw = group * hd
    aw = N_HEADS * hd
    vb = v_col0 // hd
    tq = _largest_divisor(st.l, 256, 16)
    tk = _largest_divisor(st.l, 512, 128)
    qt = st.l // tq
    cb0 = st.rows_lat // st.c
    yc = pl.pallas_call(
        functools.partial(_attn_kernel, group=group, hd=hd, tk=tk, n_chunks=st.l // tk),
        grid=(st.b, N_KV_HEADS, qt),
        in_specs=[pl.BlockSpec((tq, gw), lambda b, h, t: (b * qt + t, h)),
                  pl.BlockSpec((st.c, hd), lambda b, h, t: (cb0 + b, h)),
                  pl.BlockSpec((st.c, hd), lambda b, h, t: (cb0 + b, vb + h)),
                  pl.BlockSpec((st.l, hd), lambda b, h, t: (b, h)),
                  pl.BlockSpec((st.l, hd), lambda b, h, t: (b, vb + h))],
        out_specs=pl.BlockSpec((tq, gw), lambda b, h, t: (b * qt + t, h)),
        out_shape=jax.ShapeDtypeStruct((st.rows, aw), BF16),
        compiler_params=_cparams(3),
        name="attention_latent",
    )(q, k, p, k, p)
    return pl.pallas_call(
        functools.partial(_attn_kernel, group=group, hd=hd, tk=tk, n_chunks=0),
        grid=(st.b, N_KV_HEADS),
        in_specs=[pl.BlockSpec((st.c, gw), lambda b, h: (cb0 + b, h)),
                  pl.BlockSpec((st.c, hd), lambda b, h: (cb0 + b, h)),
                  pl.BlockSpec((st.c, hd), lambda b, h: (cb0 + b, vb + h)),
                  pl.BlockSpec(memory_space=pl.ANY)],
        out_specs=pl.BlockSpec((st.c, gw), lambda b, h: (cb0 + b, h)),
        out_shape=jax.ShapeDtypeStruct((st.rows, aw), BF16),
        input_output_aliases={3: 0},
        compiler_params=_cparams(2),
        name="attention_context",
    )(q, k, p, yc)


def kernel(x, c, ctx, c_ctx, w_mod, b_mod, norm_gain, final_gain, ffn_w_in, ffn_w_out, w_in, b_gate, conv_w, conv_b, filt_w1, filt_b1, filt_w2, filt_b2, filt_w3, filt_b3, filt_w4, filt_freq, hyena_bias, w_pool, pool_scale, q_gain, k_gain, w_up, w_out):
    b, l, d = x.shape
    lc = ctx.shape[1]
    depth = w_mod.shape[0]
    width = hyena_bias.shape[2]
    hd = q_gain.shape[1]
    aw, kvw = N_HEADS * hd, N_KV_HEADS * hd
    a1 = (HYENA_ORDER + 1) * width
    q0 = a1 + width
    k0 = q0 + aw
    v0 = k0 + kvw
    g0 = v0 + kvw
    st = _Stream(b, l, lc)

    xs = jnp.concatenate([x.reshape(b * l, d), ctx.reshape(b * lc, d)], axis=0)
    cvec = jnp.zeros((8, d), F32).at[:b].set(c).at[b].set(c_ctx)
    mods = _modulation(cvec, w_mod, b_mod).reshape(depth, 8, N_MOD, d)[:, :b + 1]
    mods = mods.reshape(depth * (b + 1) * N_MOD, 1, d)
    gains = norm_gain.reshape(depth * 3, 1, d)
    b_gate3 = b_gate.reshape(depth * N_BRANCH, 1, d)
    pool_scale3 = pool_scale.reshape(depth, 1, width)
    w_pool_bf = w_pool.astype(BF16)
    h_lat = _filter_mlp(l, filt_w1, filt_b1, filt_w2, filt_b2, filt_w3, filt_b3, filt_freq)
    h_ctx = _filter_mlp(lc, filt_w1, filt_b1, filt_w2, filt_b2, filt_w3, filt_b3, filt_freq)

    for i in range(depth):
        h = _adaln(st, xs, gains, mods, i, 0, 0, 1)
        a = _ffn_in(st, h, ffn_w_in, i, 0)
        xs = _mm_residual(st, a, ffn_w_out, (i, 0), xs, mods, i, 2, 0.5, 512)
        h = _adaln(st, xs, gains, mods, i, 1, 3, 4)
        p = _in_proj(st, h, w_in, i)
        cw, cb = conv_w[i], conv_b[i].reshape(1, -1)
        u3 = _short_conv_call(p, cw, cb, width, 0, st.rows_lat, l, st.rows, None)
        u3 = _short_conv_call(p, cw, cb, width, st.rows_lat, b * lc, lc, st.rows, u3)
        kk = _filter_taps(h_lat, filt_w4, i, l, width)
        kk_ctx = _filter_taps(h_ctx, filt_w4, i, lc, width)
        z = _long_conv_gate(st, u3, 0, u3, 1, kk, kk_ctx, 0, hyena_bias[i, 0])
        ya = _long_conv_gate(st, z[None], 0, u3, 2, kk, kk_ctx, 1, hyena_bias[i, 1])
        yb = _pool_call(p, a1 // width, w_pool_bf, pool_scale3, i, width, 0, b, l, st.rows, None)
        yb = _pool_call(p, a1 // width, w_pool_bf, pool_scale3, i, width, st.rows_lat, b, lc, st.rows, yb)
        q, k = _qk_prep(st, p, q0, k0, aw, kvw, q_gain, k_gain, i, hd)
        yc = _attention(st, q, k, p, v0, hd)
        merged = _merge(st, ya, yb, yc, p, g0, b_gate3, w_up, i)
        xs = _mm_residual(st, merged, w_out, (i,), xs, mods, i, 5, 1.0, 512)
        h = _adaln(st, xs, gains, mods, i, 2, 6, 7)
        a = _ffn_in(st, h, ffn_w_in, i, 1)
        xs = _mm_residual(st, a, ffn_w_out, (i, 1), xs, mods, i, 8, 0.5, 512)

    out = pl.pallas_call(
        _rmsnorm_kernel,
        grid=(st.n_lat_tiles,),
        in_specs=[pl.BlockSpec((st.tm, d), lambda i: (i, 0)),
                  pl.BlockSpec((1, d), lambda i: (0, 0))],
        out_specs=pl.BlockSpec((st.tm, d), lambda i: (i, 0)),
        out_shape=jax.ShapeDtypeStruct((b * l, d), F32),
        compiler_params=_cparams(1),
        name="final_rmsnorm",
    )(xs, final_gain.reshape(1, d))
    return out.reshape(b, l, d)
```

```python
import functools
import math

import numpy as np
import jax
import jax.numpy as jnp
from jax import lax
from jax.experimental import pallas as pl
from jax.experimental.pallas import tpu as pltpu

F32 = jnp.float32
BF16 = jnp.bfloat16

N_HEADS = 16
N_KV_HEADS = 4
GRID_W = 64
ROPE_THETA = 10000.0
HYENA_ORDER = 2
HYENA_TARGET = 1e-2
HYENA_FAST_DECAY = 0.3
HYENA_SLOW_DECAY = 1.5
POOL_WINDOWS = (2, 4, 8, 16)
N_MOD = 9
N_BRANCH = 3
EPS = 1e-6
HIGHEST = lax.Precision.HIGHEST

VMEM_LIMIT_V7X = 56 * 1024 * 1024


def _cparams(n_axes):
    return pltpu.CompilerParams(
        dimension_semantics=("arbitrary",) * n_axes, vmem_limit_bytes=VMEM_LIMIT_V7X)


def _largest_divisor(total, pref, align):
    if total <= pref:
        return total
    t = (pref // align) * align
    while t > align and total % t:
        t -= align
    assert total % t == 0, (total, pref, align)
    return t


def _silu(v):
    return v * jax.nn.sigmoid(v)


def _mod_kernel(c_ref, w_ref, b_ref, o_ref):
    s = _silu(c_ref[...]).astype(BF16)
    w = w_ref[0].astype(BF16)
    o_ref[0] = jnp.dot(s, w, preferred_element_type=F32) + b_ref[0]


def _modulation(cvec, w_mod, b_mod):
    depth, d, nm = w_mod.shape
    tn = _largest_divisor(nm, 1024, 128)
    return pl.pallas_call(
        _mod_kernel,
        grid=(depth, nm // tn),
        in_specs=[pl.BlockSpec((8, d), lambda l, n: (0, 0)),
                  pl.BlockSpec((1, d, tn), lambda l, n: (l, 0, n)),
                  pl.BlockSpec((1, 1, tn), lambda l, n: (l, 0, n))],
        out_specs=pl.BlockSpec((1, 8, tn), lambda l, n: (l, 0, n)),
        out_shape=jax.ShapeDtypeStruct((depth, 8, nm), F32),
        compiler_params=_cparams(2),
        name="modulation",
    )(cvec, w_mod, b_mod.reshape(depth, 1, nm))


def _adaln_kernel(x_ref, g_ref, sh_ref, sc_ref, o_ref):
    x = x_ref[...]
    y = x * lax.rsqrt(jnp.mean(x * x, axis=-1, keepdims=True) + EPS)
    y = y * g_ref[0]
    o_ref[...] = (y * (1.0 + sc_ref[0]) + sh_ref[0]).astype(o_ref.dtype)


def _rmsnorm_kernel(x_ref, g_ref, o_ref):
    x = x_ref[...]
    y = x * lax.rsqrt(jnp.mean(x * x, axis=-1, keepdims=True) + EPS)
    o_ref[...] = y * g_ref[...]


class _Stream:
    def __init__(self, batch, seq, ctx_len):
        self.b, self.l, self.c = batch, seq, ctx_len
        self.rows_lat = batch * seq
        self.rows = batch * (seq + ctx_len)
        self.tm = _largest_divisor(math.gcd(seq, batch * ctx_len), 512, 16)
        self.tiles_per_seq = seq // self.tm
        self.n_lat_tiles = self.rows_lat // self.tm
        self.n_tiles = self.rows // self.tm
        self.tm_mm = _largest_divisor(self.rows, 1088, 16)
        self.n_mm_tiles = self.rows // self.tm_mm

    def group(self, i):
        return jnp.where(i < self.n_lat_tiles, i // self.tiles_per_seq, self.b)

    def per_row(self, tile, tile_rows, group_vals):
        row = tile * tile_rows + lax.broadcasted_iota(jnp.int32, (tile_rows, 1), 0)
        out = group_vals[self.b]
        for g in reversed(range(self.b)):
            out = jnp.where(row < (g + 1) * self.l, group_vals[g], out)
        return out


def _adaln(st, x, gains, mods, layer, j, m_shift, m_scale):
    d = x.shape[1]
    base = layer * (st.b + 1) * N_MOD

    def mod_map(m):
        return lambda i: (base + st.group(i) * N_MOD + m, 0, 0)

    return pl.pallas_call(
        _adaln_kernel,
        grid=(st.n_tiles,),
        in_specs=[pl.BlockSpec((st.tm, d), lambda i: (i, 0)),
                  pl.BlockSpec((1, 1, d), lambda i: (layer * 3 + j, 0, 0)),
                  pl.BlockSpec((1, 1, d), mod_map(m_shift)),
                  pl.BlockSpec((1, 1, d), mod_map(m_scale))],
        out_specs=pl.BlockSpec((st.tm, d), lambda i: (i, 0)),
        out_shape=jax.ShapeDtypeStruct(x.shape, BF16),
        compiler_params=_cparams(1),
        name="adaln",
    )(x, gains, mods, mods)


def _ffn_in_kernel(h_ref, wg_ref, wu_ref, o_ref, wg_s, wu_s):
    @pl.when(pl.program_id(1) == 0)
    def _():
        wg_s[...] = wg_ref[...].astype(BF16)
        wu_s[...] = wu_ref[...].astype(BF16)

    h = h_ref[...]
    g = jnp.dot(h, wg_s[...], preferred_element_type=F32)
    u = jnp.dot(h, wu_s[...], preferred_element_type=F32)
    o_ref[...] = (_silu(g) * u).astype(o_ref.dtype)


def _ffn_in(st, h, ffn_w_in, layer, j):
    d = h.shape[1]
    f = ffn_w_in.shape[3] // 2
    tn = _largest_divisor(f, 512, 128)
    nt = f // tn
    return pl.pallas_call(
        _ffn_in_kernel,
        grid=(nt, st.n_mm_tiles),
        in_specs=[pl.BlockSpec((st.tm_mm, d), lambda n, m: (m, 0)),
                  pl.BlockSpec((None, None, d, tn), lambda n, m: (layer, j, 0, n)),
                  pl.BlockSpec((None, None, d, tn), lambda n, m: (layer, j, 0, n + nt))],
        out_specs=pl.BlockSpec((st.tm_mm, tn), lambda n, m: (m, n)),
        out_shape=jax.ShapeDtypeStruct((st.rows, f), BF16),
        scratch_shapes=[pltpu.VMEM((d, tn), BF16), pltpu.VMEM((d, tn), BF16)],
        compiler_params=_cparams(2),
        name="ffn_in",
    )(h, ffn_w_in, ffn_w_in)


def _mm_res_kernel(a_ref, w_ref, x_ref, *rest, st, gate_scale):
    gate_refs, o_ref = rest[:-1], rest[-1]
    gate = st.per_row(pl.program_id(0), a_ref.shape[0], [g[0] for g in gate_refs])
    y = jnp.dot(a_ref[...], w_ref[...], preferred_element_type=F32)
    o_ref[...] = x_ref[...] + (gate_scale * gate) * y


def _mm_residual(st, a, w_bf, w_index, x, mods, layer, m_gate, gate_scale, tn_pref):
    k = a.shape[1]
    d = x.shape[1]
    tn = _largest_divisor(d, tn_pref, 128)
    base = layer * (st.b + 1) * N_MOD
    lead = (None,) * len(w_index)

    def gate_spec(g):
        return pl.BlockSpec((1, 1, tn), lambda m, n: (base + g * N_MOD + m_gate, 0, n))

    return pl.pallas_call(
        functools.partial(_mm_res_kernel, st=st, gate_scale=gate_scale),
        grid=(st.n_mm_tiles, d // tn),
        in_specs=[pl.BlockSpec((st.tm_mm, k), lambda m, n: (m, 0)),
                  pl.BlockSpec(lead + (k, tn), lambda m, n: tuple(w_index) + (0, n)),
                  pl.BlockSpec((st.tm_mm, tn), lambda m, n: (m, n))]
                 + [gate_spec(g) for g in range(st.b + 1)],
        out_specs=pl.BlockSpec((st.tm_mm, tn), lambda m, n: (m, n)),
        out_shape=jax.ShapeDtypeStruct(x.shape, F32),
        compiler_params=_cparams(2),
        name="matmul_residual",
    )(a, w_bf, x, *([mods] * (st.b + 1)))


def _mm_kernel(h_ref, w_ref, o_ref, w_s):
    @pl.when(pl.program_id(1) == 0)
    def _():
        w_s[...] = w_ref[...].astype(BF16)

    o_ref[...] = jnp.dot(h_ref[...], w_s[...], preferred_element_type=F32).astype(o_ref.dtype)


def _in_proj(st, h, w_in, layer):
    d = h.shape[1]
    n_in = w_in.shape[2]
    tn = _largest_divisor(n_in, 1024, 128)
    return pl.pallas_call(
        _mm_kernel,
        grid=(n_in // tn, st.n_mm_tiles),
        in_specs=[pl.BlockSpec((st.tm_mm, d), lambda n, m: (m, 0)),
                  pl.BlockSpec((None, d, tn), lambda n, m: (layer, 0, n))],
        out_specs=pl.BlockSpec((st.tm_mm, tn), lambda n, m: (m, n)),
        out_shape=jax.ShapeDtypeStruct((st.rows, n_in), BF16),
        scratch_shapes=[pltpu.VMEM((d, tn), BF16)],
        compiler_params=_cparams(2),
        name="in_proj",
    )(h, w_in)


def _merge_kernel(ya_ref, yb_ref, yc_ref, pa_ref, pb_ref, pc_ref, ba_ref, bb_ref, bc_ref, w_ref, o_ref):
    acc = None
    for k, (y_ref, p_ref, b_ref) in enumerate(
            ((ya_ref, pa_ref, ba_ref), (yb_ref, pb_ref, bb_ref), (yc_ref, pc_ref, bc_ref))):
        gate = jax.nn.sigmoid(p_ref[...].astype(F32) + b_ref[0])
        t = gate * jnp.dot(y_ref[...], w_ref[k], preferred_element_type=F32)
        acc = t if acc is None else acc + t
    o_ref[...] = acc.astype(o_ref.dtype)


def _merge(st, ya, yb, yc, p, gate_col0, b_gate3, w_up_bf, layer):
    w = ya.shape[1]
    d = w_up_bf.shape[3]
    tn = _largest_divisor(d, 256, 128)
    gb = gate_col0 // tn
    dt = d // tn
    tm = st.tm_mm
    y_spec = pl.BlockSpec((tm, w), lambda m, n: (m, 0))

    def p_spec(k):
        return pl.BlockSpec((tm, tn), lambda m, n: (m, gb + k * dt + n))

    def b_spec(k):
        return pl.BlockSpec((1, 1, tn), lambda m, n: (layer * N_BRANCH + k, 0, n))

    return pl.pallas_call(
        _merge_kernel,
        grid=(st.n_mm_tiles, dt),
        in_specs=[y_spec, y_spec, y_spec, p_spec(0), p_spec(1), p_spec(2),
                  b_spec(0), b_spec(1), b_spec(2),
                  pl.BlockSpec((None, N_BRANCH, w, tn), lambda m, n: (layer, 0, 0, n))],
        out_specs=pl.BlockSpec((tm, tn), lambda m, n: (m, n)),
        out_shape=jax.ShapeDtypeStruct((st.rows, d), BF16),
        compiler_params=_cparams(2),
        name="merge",
    )(ya, yb, yc, p, p, p, b_gate3, b_gate3, b_gate3, w_up_bf)


HALO = 16


def _short_conv_kernel(cur_ref, prev_ref, next_ref, w_ref, b_ref, o_ref, *, blocks_per_seq):
    j = pl.program_id(1) % blocks_per_seq
    u = cur_ref[...].astype(F32)
    t = u.shape[0]
    row = lax.broadcasted_iota(jnp.int32, u.shape, 0)
    prev_row = jnp.where(j == 0, 0.0, prev_ref[HALO - 1:HALO, :].astype(F32))
    next_row = jnp.where(j == blocks_per_seq - 1, 0.0, next_ref[0:1, :].astype(F32))
    up = jnp.where(row == 0, prev_row, pltpu.roll(u, 1, axis=0))
    un = jnp.where(row == t - 1, next_row, pltpu.roll(u, t - 1, axis=0))
    w = w_ref[...]
    o_ref[...] = (b_ref[...] + up * w[0:1] + u * w[1:2] + un * w[2:3]).astype(o_ref.dtype)


def _short_conv_call(p, conv_w, conv_b, width, row0, n_rows, seq_len, out_rows, prior):
    ts = _largest_divisor(seq_len, 512, HALO)
    ct = _largest_divisor(width, 1024, 128)
    blocks_per_seq = seq_len // ts
    rb0 = row0 // ts
    hb = ts // HALO
    n_hblocks = out_rows // HALO
    cpb = width // ct
    n_parts = HYENA_ORDER + 1

    def cur_map(c, r):
        return (rb0 + r, c)

    def prev_map(c, r):
        return (jnp.maximum((rb0 + r) * hb - 1, 0), c)

    def next_map(c, r):
        return (jnp.minimum((rb0 + r + 1) * hb, n_hblocks - 1), c)

    in_specs = [pl.BlockSpec((ts, ct), cur_map),
                pl.BlockSpec((HALO, ct), prev_map),
                pl.BlockSpec((HALO, ct), next_map),
                pl.BlockSpec((3, ct), lambda c, r: (0, c)),
                pl.BlockSpec((1, ct), lambda c, r: (0, c))]
    args = [p, p, p, conv_w, conv_b]
    aliases = {}
    kernel = functools.partial(_short_conv_kernel, blocks_per_seq=blocks_per_seq)
    if prior is not None:
        in_specs.append(pl.BlockSpec(memory_space=pl.ANY))
        args.append(prior)
        aliases = {5: 0}
        body = kernel
        kernel = lambda c, pv, nx, w, b, _prior, o: body(c, pv, nx, w, b, o)
    return pl.pallas_call(
        kernel,
        grid=(n_parts * cpb, n_rows // ts),
        in_specs=in_specs,
        out_specs=pl.BlockSpec((None, ts, ct), lambda c, r: (c // cpb, rb0 + r, c % cpb)),
        out_shape=jax.ShapeDtypeStruct((n_parts, out_rows, width), BF16),
        input_output_aliases=aliases,
        compiler_params=_cparams(2),
        name="short_conv",
    )(*args)


@functools.lru_cache(maxsize=None)
def _filter_positions(seq_len, emb_dim, pad_dim):
    bands = (emb_dim - 1) // 2
    j = np.arange(seq_len, dtype=np.float64)
    t = j / (seq_len - 1)
    wpos = 2.0 * np.pi * j / seq_len
    f = np.linspace(1e-4, bands - 1, bands)
    z = np.concatenate([t[:, None], np.cos(f[None] * wpos[:, None]), -np.sin(f[None] * wpos[:, None])], axis=1)
    lag = np.concatenate([np.arange(seq_len), [0], np.arange(seq_len - 1, 0, -1)])
    z2 = np.zeros((2 * seq_len, pad_dim), np.float64)
    z2[:, :emb_dim] = z[lag]
    return z2.astype(np.float32)


def _filter_mlp_kernel(z_ref, w1_ref, b1_ref, w2_ref, b2_ref, w3_ref, b3_ref, fr_ref, o_ref):
    fr = fr_ref[0]
    h = jnp.dot(z_ref[...], w1_ref[0], precision=HIGHEST, preferred_element_type=F32)
    h = jnp.sin(fr[0:1] * (h + b1_ref[0]))
    h = jnp.dot(h, w2_ref[0], precision=HIGHEST, preferred_element_type=F32)
    h = jnp.sin(fr[1:2] * (h + b2_ref[0]))
    h = jnp.dot(h, w3_ref[0], precision=HIGHEST, preferred_element_type=F32)
    o_ref[0] = jnp.sin(fr[2:3] * (h + b3_ref[0]))


def _filter_mlp(seq_len, w1, b1, w2, b2, w3, b3, freq):
    depth, emb, hid = w1.shape
    pad = -(-emb // 8) * 8
    n = 2 * seq_len
    z2 = jnp.asarray(_filter_positions(seq_len, emb, pad))
    w1p = jnp.pad(w1, ((0, 0), (0, pad - emb), (0, 0)))
    tr = _largest_divisor(n, 1024, 8)
    wspec = lambda k: pl.BlockSpec((1, k, hid), lambda l, r: (l, 0, 0))
    bspec = pl.BlockSpec((1, 1, hid), lambda l, r: (l, 0, 0))
    return pl.pallas_call(
        _filter_mlp_kernel,
        grid=(depth, n // tr),
        in_specs=[pl.BlockSpec((tr, pad), lambda l, r: (r, 0)),
                  wspec(pad), bspec, wspec(hid), bspec, wspec(hid), bspec,
                  pl.BlockSpec((1, 3, hid), lambda l, r: (l, 0, 0))],
        out_specs=pl.BlockSpec((1, tr, hid), lambda l, r: (l, r, 0)),
        out_shape=jax.ShapeDtypeStruct((depth, n, hid), F32),
        compiler_params=_cparams(2),
        name="filter_mlp",
    )(z2, w1p, b1.reshape(depth, 1, hid), w2, b2.reshape(depth, 1, hid), w3, b3.reshape(depth, 1, hid), freq)


def _filter_taps_kernel(h_ref, wf_ref, wb_ref, delta_ref, o_ref, *, seq_len):
    l = seq_len
    delta = delta_ref[...]
    lag = lax.broadcasted_iota(jnp.int32, (l, 1), 0)
    t_top = lag.astype(F32) / (l - 1.0)
    t_bot = jnp.where(lag == 0, 0, l - lag).astype(F32) / (l - 1.0)
    top = jnp.dot(h_ref[0, 0:l, :].astype(BF16), wf_ref[0].astype(BF16), preferred_element_type=F32)
    bot = jnp.dot(h_ref[0, l:2 * l, :].astype(BF16), wb_ref[0].astype(BF16), preferred_element_type=F32)
    top = top * jnp.exp(-t_top * delta)
    bot = bot * jnp.exp(-t_bot * delta)
    top = top + jnp.where(lag == 0, bot[0:1, :], 0.0)
    bot = jnp.where(lag == 0, 0.0, bot)
    ss = jnp.sum(top * top, axis=0, keepdims=True) + jnp.sum(bot * bot, axis=0, keepdims=True)
    scale = lax.rsqrt(ss + EPS)
    o_ref[0, 0:l, :] = (top * scale).astype(o_ref.dtype)
    o_ref[0, l:2 * l, :] = (bot * scale).astype(o_ref.dtype)


def _filter_taps(h_all, filt_w4, layer, seq_len, width):
    hid = h_all.shape[2]
    n = 2 * seq_len
    ct = 128
    cpb = width // ct
    min_decay = math.log(HYENA_TARGET) / HYENA_SLOW_DECAY
    max_decay = math.log(HYENA_TARGET) / HYENA_FAST_DECAY
    delta = jnp.asarray(np.abs(np.linspace(min_decay, max_decay, width)).astype(np.float32)).reshape(1, width)
    return pl.pallas_call(
        functools.partial(_filter_taps_kernel, seq_len=seq_len),
        grid=(HYENA_ORDER, cpb),
        in_specs=[pl.BlockSpec((1, n, hid), lambda o, c: (layer, 0, 0)),
                  pl.BlockSpec((1, hid, ct), lambda o, c: (layer, 0, (2 * o) * cpb + c)),
                  pl.BlockSpec((1, hid, ct), lambda o, c: (layer, 0, (2 * o + 1) * cpb + c)),
                  pl.BlockSpec((1, ct), lambda o, c: (0, c))],
        out_specs=pl.BlockSpec((1, n, ct), lambda o, c: (o, 0, c)),
        out_shape=jax.ShapeDtypeStruct((HYENA_ORDER, n, width), BF16),
        compiler_params=_cparams(2),
        name="filter_taps",
    )(h_all, filt_w4, filt_w4, delta)


def _real_form(e):
    return np.block([[e.real, -e.imag], [e.imag, e.real]])


@functools.lru_cache(maxsize=None)
def _dft_tables(n1, n2):
    n = n1 * n2
    j2 = np.arange(n2)[:, None, None]
    k1 = np.arange(n1)[None, :, None]
    j1 = np.arange(n1)[None, None, :]
    e = np.exp(-2j * np.pi * ((j1 * k1) / n1 + (j2 * k1) / n))
    eh = e[:, :, :n1 // 2]
    t_data = np.concatenate(
        [np.concatenate([eh.real, -eh.imag], axis=2), np.concatenate([eh.imag, eh.real], axis=2)], axis=1)
    t_filt = np.concatenate([e.real, e.imag], axis=1)
    hinv = np.conj(np.transpose(eh, (0, 2, 1))) / n
    t_inv = np.concatenate(
        [np.concatenate([hinv.real, -hinv.imag], axis=2), np.concatenate([hinv.imag, hinv.real], axis=2)], axis=1)
    f2 = np.exp(-2j * np.pi * np.outer(np.arange(n2), np.arange(n2)) / n2)
    as32 = lambda a: np.ascontiguousarray(a, dtype=np.float32)
    return as32(t_data), as32(t_filt), as32(t_inv), as32(_real_form(f2)), as32(_real_form(np.conj(f2)))


def _split_n(n):
    n2 = 1
    while n2 < 128 and n // (n2 * 2) >= 64:
        n2 *= 2
    return n // n2, n2


GS = 16
LANES = 128


def _spread(src_ref, dst_ref):
    a, gs, _ = src_ref.shape
    for c in range(dst_ref.shape[0]):
        dst_ref[c] = src_ref[:, :, c * LANES:(c + 1) * LANES].reshape(a * gs, LANES).astype(F32)


def _gather(s_ref, g, count, stride):
    return jnp.concatenate(
        [s_ref[c, pl.ds(g, count, stride=stride), :] for c in range(s_ref.shape[0])], axis=1)


def _fft_s1_kernel(z_ref, t_ref, o_ref, s_ref):
    n1, gs, _ = z_ref.shape
    _spread(z_ref, s_ref)
    for g in range(gs):
        z = _gather(s_ref, g, n1, gs).astype(BF16)
        o_ref[g] = jnp.dot(t_ref[g], z, preferred_element_type=F32).astype(o_ref.dtype)


def _fft_mid_kernel(ar_ref, ai_ref, fr_ref, fi_ref, f2_ref, f2i_ref, o_ref, sar, sai, sfr, sfi):
    n2, gs, _ = ar_ref.shape
    for src, dst in ((ar_ref, sar), (ai_ref, sai), (fr_ref, sfr), (fi_ref, sfi)):
        _spread(src, dst)
    f2 = f2_ref[...]
    f2i = f2i_ref[...]
    def spectra(g):
        a = jnp.concatenate([_gather(sar, g, n2, gs), _gather(sai, g, n2, gs)], axis=0).astype(BF16)
        f = jnp.concatenate([_gather(sfr, g, n2, gs), _gather(sfi, g, n2, gs)], axis=0).astype(BF16)
        return (jnp.dot(f2, a, preferred_element_type=F32),
                jnp.dot(f2, f, preferred_element_type=F32))

    nxt = spectra(0)
    for g in range(gs):
        u, k = nxt
        if g + 1 < gs:
            nxt = spectra(g + 1)
        ur, ui, kr, ki = u[:n2], u[n2:], k[:n2], k[n2:]
        y = jnp.concatenate([ur * kr - ui * ki, ur * ki + ui * kr], axis=0).astype(BF16)
        o_ref[g] = jnp.dot(f2i, y, preferred_element_type=F32).astype(o_ref.dtype)


def _fft_s3_kernel(cr_ref, ci_ref, t_ref, u_ref, x_ref, bias_ref, o_ref, scr, sci, sy):
    n1, gs, _ = cr_ref.shape
    _spread(cr_ref, scr)
    _spread(ci_ref, sci)
    for g in range(gs):
        c = jnp.concatenate([_gather(scr, g, n1, gs), _gather(sci, g, n1, gs)], axis=0).astype(BF16)
        y = jnp.dot(t_ref[g], c, preferred_element_type=F32)
        for cc in range(sy.shape[0]):
            sy[cc, g * n1:(g + 1) * n1, :] = y[:, cc * LANES:(cc + 1) * LANES]
    bias = bias_ref[...]
    for r in range(n1):
        y = _gather(sy, r, gs, n1)
        u = u_ref[r].astype(F32)
        o_ref[r] = (x_ref[r].astype(F32) * (y + u * bias)).astype(o_ref.dtype)


def _hyena_small_kernel(u_ref, k_ref, x_ref, bias_ref, td_ref, tf_ref, ti_ref, _latent_rows_ref, o_ref, *, n):
    u = u_ref[...]
    a = jnp.dot(td_ref[...], u, preferred_element_type=F32)
    k = jnp.dot(tf_ref[...], k_ref[...], preferred_element_type=F32)
    ar, ai, kr, ki = a[:n], a[n:], k[:n], k[n:]
    y = jnp.concatenate([ar * kr - ai * ki, ar * ki + ai * kr], axis=0).astype(BF16)
    y = jnp.dot(ti_ref[...], y, preferred_element_type=F32)
    o_ref[...] = (x_ref[...].astype(F32) * (y + u.astype(F32) * bias_ref[...])).astype(o_ref.dtype)


def _bf16_table(a):
    return jnp.asarray(a).astype(BF16)


def _long_conv_gate(st, u3, u_part, x_arr, x_part, kk, kk_ctx, order, bias):
    assert st.b == 2, "the two batch elements are packed as one complex signal"
    _, rows, width = u3.shape
    l, lc = st.l, st.c
    n = 2 * l
    n1, n2 = _split_n(n)
    t_data, t_filt, t_inv, f2, f2i = (_bf16_table(a) for a in _dft_tables(n1, n2))
    assert n1 % GS == 0 and n2 % GS == 0 and width % LANES == 0
    bias2 = bias.reshape(1, width)
    slabs = rows // n2
    nl = width // LANES

    def stage1(z4, part, table, name):
        return pl.pallas_call(
            _fft_s1_kernel,
            grid=(n2 // GS,),
            in_specs=[pl.BlockSpec((None, n1, GS, width), lambda j: (part, 0, j, 0)),
                      pl.BlockSpec((GS, 2 * n1, n1), lambda j: (j, 0, 0))],
            out_specs=pl.BlockSpec((GS, 2 * n1, width), lambda j: (j, 0, 0)),
            out_shape=jax.ShapeDtypeStruct((n2, 2 * n1, width), BF16),
            scratch_shapes=[pltpu.VMEM((nl, n1 * GS, LANES), F32)],
            compiler_params=_cparams(1),
            name=name,
        )(z4, table)

    a_filt = stage1(kk.reshape(HYENA_ORDER, n1, n2, width), order, t_filt, "fft_stage1_filter")
    u_v = u3.reshape(u3.shape[0], slabs, n2, width)
    a_data = stage1(u_v, u_part, t_data, "fft_stage1_data")

    ctm = _largest_divisor(width, 256, LANES)
    kb_im = n1 // GS
    re_spec = pl.BlockSpec((n2, GS, ctm), lambda c, k: (0, k, c))
    im_spec = pl.BlockSpec((n2, GS, ctm), lambda c, k: (0, kb_im + k, c))
    mat_spec = pl.BlockSpec((2 * n2, 2 * n2), lambda c, k: (0, 0))
    c_mid = pl.pallas_call(
        _fft_mid_kernel,
        grid=(width // ctm, n1 // GS),
        in_specs=[re_spec, im_spec, re_spec, im_spec, mat_spec, mat_spec],
        out_specs=pl.BlockSpec((GS, 2 * n2, ctm), lambda c, k: (k, 0, c)),
        out_shape=jax.ShapeDtypeStruct((n1, 2 * n2, width), BF16),
        scratch_shapes=[pltpu.VMEM((ctm // LANES, n2 * GS, LANES), F32)] * 4,
        compiler_params=_cparams(2),
        name="fft_mid",
    )(a_data, a_data, a_filt, a_filt, f2, f2i)

    cts = _largest_divisor(width, 1024, LANES)
    jb_im = n2 // GS
    x_v = x_arr.reshape(x_arr.shape[0], slabs, n2, width)
    out = pl.pallas_call(
        _fft_s3_kernel,
        grid=(width // cts, n2 // GS),
        in_specs=[pl.BlockSpec((n1, GS, cts), lambda c, j: (0, j, c)),
                  pl.BlockSpec((n1, GS, cts), lambda c, j: (0, jb_im + j, c)),
                  pl.BlockSpec((GS, n1, 2 * n1), lambda c, j: (j, 0, 0)),
                  pl.BlockSpec((None, n1, GS, cts), lambda c, j: (u_part, 0, j, c)),
                  pl.BlockSpec((None, n1, GS, cts), lambda c, j: (x_part, 0, j, c)),
                  pl.BlockSpec((1, cts), lambda c, j: (0, c))],
        out_specs=pl.BlockSpec((n1, GS, cts), lambda c, j: (0, j, c)),
        out_shape=jax.ShapeDtypeStruct((slabs, n2, width), BF16),
        scratch_shapes=[pltpu.VMEM((cts // LANES, n1 * GS, LANES), F32)] * 3,
        compiler_params=_cparams(2),
        name="fft_stage3",
    )(c_mid, c_mid, t_inv, u_v, x_v, bias2)
    out = out.reshape(rows, width)

    nc = 2 * lc
    tc_data, tc_filt, tc_inv, _, _ = (_bf16_table(a) for a in _dft_tables(nc, 1))
    ct = _largest_divisor(width, 512, 128)
    rb = (st.rows_lat) // (2 * lc)
    out = pl.pallas_call(
        functools.partial(_hyena_small_kernel, n=nc),
        grid=(width // ct,),
        in_specs=[pl.BlockSpec((None, 2 * lc, ct), lambda c: (u_part, rb, c)),
                  pl.BlockSpec((None, nc, ct), lambda c: (order, 0, c)),
                  pl.BlockSpec((None, 2 * lc, ct), lambda c: (x_part, rb, c)),
                  pl.BlockSpec((1, ct), lambda c: (0, c)),
                  pl.BlockSpec((2 * nc, nc), lambda c: (0, 0)),
                  pl.BlockSpec((2 * nc, nc), lambda c: (0, 0)),
                  pl.BlockSpec((nc, 2 * nc), lambda c: (0, 0)),
                  pl.BlockSpec(memory_space=pl.ANY)],
        out_specs=pl.BlockSpec((2 * lc, ct), lambda c: (rb, c)),
        out_shape=jax.ShapeDtypeStruct((rows, width), BF16),
        input_output_aliases={7: 0},
        compiler_params=_cparams(1),
        name="hyena_context",
    )(u3, kk_ctx, x_arr, bias2, tc_data[0], tc_filt[0], tc_inv[0], out)
    return out


@functools.lru_cache(maxsize=None)
def _pool_tables(t, windows):
    out = np.zeros((4, len(windows), t, 3 * t), np.float64)
    for v in range(4):
        lo_bound = t if v & 1 else 0
        hi_bound = 2 * t - 1 if v & 2 else 3 * t - 1
        for g, w in enumerate(windows):
            for r in range(t):
                pos = t + r
                lo = max(pos - w // 2, lo_bound)
                hi = min(pos + w - 1 - w // 2, hi_bound)
                out[v, g, r, lo:hi + 1] = 1.0 / (hi - lo + 1)
                out[v, g, r, pos] -= 1.0
    return out.astype(np.float32)


def _pool_kernel(prev_ref, cur_ref, next_ref, m_ref, w_ref, s_ref, *rest, n_groups, gw):
    o_ref = rest[-1]
    for g in range(n_groups):
        sl = slice(g * gw, (g + 1) * gw)
        u = jnp.concatenate([prev_ref[:, sl], cur_ref[:, sl], next_ref[:, sl]], axis=0)
        d = jnp.dot(m_ref[0, g], u, preferred_element_type=F32)
        y = jnp.dot(d.astype(BF16), w_ref[g], preferred_element_type=F32)
        o_ref[:, sl] = (y * s_ref[:, sl]).astype(o_ref.dtype)


def _pool_call(p, col_block, w_pool_bf, pool_scale3, layer, width, row0, n_seq, seq_len, out_rows, prior):
    t = _largest_divisor(seq_len, 256, 16)
    nblk = seq_len // t
    rb0 = row0 // t
    n_groups = len(POOL_WINDOWS)
    gw = width // n_groups
    tables = _bf16_table(_pool_tables(t, POOL_WINDOWS))

    def blk(delta):
        return lambda s, j: (rb0 + s * nblk + jnp.clip(j + delta, 0, nblk - 1), col_block)

    def variant(s, j):
        return ((j == 0).astype(jnp.int32) + 2 * (j == nblk - 1).astype(jnp.int32), 0, 0, 0)

    in_specs = [pl.BlockSpec((t, width), blk(-1)),
                pl.BlockSpec((t, width), blk(0)),
                pl.BlockSpec((t, width), blk(1)),
                pl.BlockSpec((1, n_groups, t, 3 * t), variant),
                pl.BlockSpec((None, n_groups, gw, gw), lambda s, j: (layer, 0, 0, 0)),
                pl.BlockSpec((None, 1, width), lambda s, j: (layer, 0, 0))]
    args = [p, p, p, tables, w_pool_bf, pool_scale3]
    aliases = {}
    if prior is not None:
        in_specs.append(pl.BlockSpec(memory_space=pl.ANY))
        args.append(prior)
        aliases = {6: 0}
    return pl.pallas_call(
        functools.partial(_pool_kernel, n_groups=n_groups, gw=gw),
        grid=(n_seq, nblk),
        in_specs=in_specs,
        out_specs=pl.BlockSpec((t, width), lambda s, j: (rb0 + s * nblk + j, 0)),
        out_shape=jax.ShapeDtypeStruct((out_rows, width), BF16),
        input_output_aliases=aliases,
        compiler_params=_cparams(2),
        name="pool",
    )(*args)


@functools.lru_cache(maxsize=None)
def _rope_tables(seq_len, grid_w, hd, pad_rows):
    axis = hd // 2
    pos = np.arange(seq_len)
    inv = ROPE_THETA ** (-np.arange(0, axis, 2, dtype=np.float64) / axis)
    ang_r = (pos // grid_w)[:, None] * inv[None]
    ang_c = (pos % grid_w)[:, None] * inv[None]
    cos = np.concatenate([np.cos(ang_r), np.cos(ang_r), np.cos(ang_c), np.cos(ang_c)], axis=1)
    sin = np.concatenate([-np.sin(ang_r), np.sin(ang_r), -np.sin(ang_c), np.sin(ang_c)], axis=1)
    cos = np.concatenate([cos, np.ones((pad_rows, hd))], axis=0)
    sin = np.concatenate([sin, np.zeros((pad_rows, hd))], axis=0)
    return cos.astype(np.float32), sin.astype(np.float32)


def _qk_prep_kernel(q_ref, k_ref, qg_ref, kg_ref, cos_ref, sin_ref, qo_ref, ko_ref, *, hd, q_scale):
    cos = cos_ref[...]
    sin = sin_ref[...]
    lane = lax.broadcasted_iota(jnp.int32, cos.shape, 1)
    first = (lane % (hd // 2)) < (hd // 4)

    def prep(x_ref, g_ref, o_ref, mul):
        for h in range(x_ref.shape[1] // hd):
            sl = slice(h * hd, (h + 1) * hd)
            x = x_ref[:, sl].astype(F32)
            y = x * lax.rsqrt(jnp.mean(x * x, axis=-1, keepdims=True) + EPS) * g_ref[...]
            partner = jnp.where(first, pltpu.roll(y, hd - hd // 4, axis=1), pltpu.roll(y, hd // 4, axis=1))
            o_ref[:, sl] = ((y * cos + partner * sin) * mul).astype(o_ref.dtype)

    prep(q_ref, qg_ref, qo_ref, q_scale)
    prep(k_ref, kg_ref, ko_ref, 1.0)


def _qk_prep(st, p, q_col0, k_col0, aw, kvw, q_gain, k_gain, layer, hd):
    cos_np, sin_np = _rope_tables(st.l, GRID_W, hd, st.tm)
    tab_map = lambda i: (jnp.where(i < st.n_lat_tiles, i % st.tiles_per_seq, st.tiles_per_seq), 0)
    return pl.pallas_call(
        functools.partial(_qk_prep_kernel, hd=hd, q_scale=hd ** -0.5),
        grid=(st.n_tiles,),
        in_specs=[pl.BlockSpec((st.tm, aw), lambda i: (i, q_col0 // aw)),
                  pl.BlockSpec((st.tm, kvw), lambda i: (i, k_col0 // kvw)),
                  pl.BlockSpec((None, 1, hd), lambda i: (layer, 0, 0)),
                  pl.BlockSpec((None, 1, hd), lambda i: (layer, 0, 0)),
                  pl.BlockSpec((st.tm, hd), tab_map),
                  pl.BlockSpec((st.tm, hd), tab_map)],
        out_specs=[pl.BlockSpec((st.tm, aw), lambda i: (i, 0)),
                   pl.BlockSpec((st.tm, kvw), lambda i: (i, 0))],
        out_shape=[jax.ShapeDtypeStruct((st.rows, aw), BF16),
                   jax.ShapeDtypeStruct((st.rows, kvw), BF16)],
        compiler_params=_cparams(1),
        name="qk_prep",
    )(p, p, q_gain.reshape(-1, 1, hd), k_gain.reshape(-1, 1, hd), jnp.asarray(cos_np), jnp.asarray(sin_np))


def _attn_kernel(*refs, group, hd, tk, n_chunks):
    q_ref, kc_ref, vc_ref = refs[:3]
    o_ref = refs[-1]
    q = q_ref[...]
    tq = q.shape[0]
    qs = jnp.concatenate([q[:, g * hd:(g + 1) * hd] for g in range(group)], axis=0)

    def update(carry, k, v):
        m, l, acc = carry
        s = lax.dot_general(qs, k, (((1,), (1,)), ((), ())), preferred_element_type=F32)
        m_new = jnp.maximum(m, jnp.max(s, axis=-1, keepdims=True))
        alpha = jnp.exp(m - m_new)
        pexp = jnp.exp(s - m_new)
        l = alpha * l + jnp.sum(pexp, axis=-1, keepdims=True)
        acc = alpha * acc + jnp.dot(pexp.astype(BF16), v, preferred_element_type=F32)
        return m_new, l, acc

    rows = group * tq
    carry = (jnp.full((rows, 1), -1e30, F32), jnp.zeros((rows, 1), F32), jnp.zeros((rows, hd), F32))
    carry = update(carry, kc_ref[...], vc_ref[...])
    if n_chunks:
        kl_ref, vl_ref = refs[3:5]

        def body(i, c):
            start = pl.multiple_of(i * tk, tk)
            return update(c, kl_ref[pl.ds(start, tk), :], vl_ref[pl.ds(start, tk), :])

        carry = lax.fori_loop(0, n_chunks, body, carry)
    _, l, acc = carry
    o = acc / l
    o_ref[...] = jnp.concatenate([o[g * tq:(g + 1) * tq] for g in range(group)], axis=1).astype(o_ref.dtype)


def _attention(st, q, k, p, v_col0, hd):
    group = N_HEADS // N_KV_HEADS
    gw = group * hd
    aw = N_HEADS * hd
    vb = v_col0 // hd
    tq = _largest_divisor(st.l, 256, 16)
    tk = _largest_divisor(st.l, 512, 128)
    qt = st.l // tq
    cb0 = st.rows_lat // st.c
    yc = pl.pallas_call(
        functools.partial(_attn_kernel, group=group, hd=hd, tk=tk, n_chunks=st.l // tk),
        grid=(st.b, N_KV_HEADS, qt),
        in_specs=[pl.BlockSpec((tq, gw), lambda b, h, t: (b * qt + t, h)),
                  pl.BlockSpec((st.c, hd), lambda b, h, t: (cb0 + b, h)),
                  pl.BlockSpec((st.c, hd), lambda b, h, t: (cb0 + b, vb + h)),
                  pl.BlockSpec((st.l, hd), lambda b, h, t: (b, h)),
                  pl.BlockSpec((st.l, hd), lambda b, h, t: (b, vb + h))],
        out_specs=pl.BlockSpec((tq, gw), lambda b, h, t: (b * qt + t, h)),
        out_shape=jax.ShapeDtypeStruct((st.rows, aw), BF16),
        compiler_params=_cparams(3),
        name="attention_latent",
    )(q, k, p, k, p)
    return pl.pallas_call(
        functools.partial(_attn_kernel, group=group, hd=hd, tk=tk, n_chunks=0),
        grid=(st.b, N_KV_HEADS),
        in_specs=[pl.BlockSpec((st.c, gw), lambda b, h: (cb0 + b, h)),
                  pl.BlockSpec((st.c, hd), lambda b, h: (cb0 + b, h)),
                  pl.BlockSpec((st.c, hd), lambda b, h: (cb0 + b, vb + h)),
                  pl.BlockSpec(memory_space=pl.ANY)],
        out_specs=pl.BlockSpec((st.c, gw), lambda b, h: (cb0 + b, h)),
        out_shape=jax.ShapeDtypeStruct((st.rows, aw), BF16),
        input_output_aliases={3: 0},
        compiler_params=_cparams(2),
        name="attention_context",
    )(q, k, p, yc)


def kernel(x, c, ctx, c_ctx, w_mod, b_mod, norm_gain, final_gain, ffn_w_in, ffn_w_out, w_in, b_gate, conv_w, conv_b, filt_w1, filt_b1, filt_w2, filt_b2, filt_w3, filt_b3, filt_w4, filt_freq, hyena_bias, w_pool, pool_scale, q_gain, k_gain, w_up, w_out):
    b, l, d = x.shape
    lc = ctx.shape[1]
    depth = w_mod.shape[0]
    width = hyena_bias.shape[2]
    hd = q_gain.shape[1]
    aw, kvw = N_HEADS * hd, N_KV_HEADS * hd
    a1 = (HYENA_ORDER + 1) * width
    q0 = a1 + width
    k0 = q0 + aw
    v0 = k0 + kvw
    g0 = v0 + kvw
    st = _Stream(b, l, lc)

    xs = jnp.concatenate([x.reshape(b * l, d), ctx.reshape(b * lc, d)], axis=0)
    cvec = jnp.zeros((8, d), F32).at[:b].set(c).at[b].set(c_ctx)
    mods = _modulation(cvec, w_mod, b_mod).reshape(depth, 8, N_MOD, d)[:, :b + 1]
    mods = mods.reshape(depth * (b + 1) * N_MOD, 1, d)
    gains = norm_gain.reshape(depth * 3, 1, d)
    b_gate3 = b_gate.reshape(depth * N_BRANCH, 1, d)
    pool_scale3 = pool_scale.reshape(depth, 1, width)
    w_pool_bf = w_pool.astype(BF16)
    ffn_w_out_bf = ffn_w_out.astype(BF16)
    w_up_bf = w_up.astype(BF16)
    w_out_bf = w_out.astype(BF16)
    h_lat =_filter_mlp(l, filt_w1, filt_b1, filt_w2, filt_b2, filt_w3, filt_b3, filt_freq)
    h_ctx = _filter_mlp(lc, filt_w1, filt_b1, filt_w2, filt_b2, filt_w3, filt_b3, filt_freq)

    for i in range(depth):
        h = _adaln(st, xs, gains, mods, i, 0, 0, 1)
        a = _ffn_in(st, h, ffn_w_in, i, 0)
        xs = _mm_residual(st, a, ffn_w_out_bf, (i, 0), xs, mods, i, 2, 0.5, 512)
        h = _adaln(st, xs, gains, mods, i, 1, 3, 4)
        p = _in_proj(st, h, w_in, i)
        cw, cb = conv_w[i], conv_b[i].reshape(1, -1)
        u3 = _short_conv_call(p, cw, cb, width, 0, st.rows_lat, l, st.rows, None)
        u3 = _short_conv_call(p, cw, cb, width, st.rows_lat, b * lc, lc, st.rows, u3)
        kk = _filter_taps(h_lat, filt_w4, i, l, width)
        kk_ctx = _filter_taps(h_ctx, filt_w4, i, lc, width)
        z = _long_conv_gate(st, u3, 0, u3, 1, kk, kk_ctx, 0, hyena_bias[i, 0])
        ya = _long_conv_gate(st, z[None], 0, u3, 2, kk, kk_ctx, 1, hyena_bias[i, 1])
        yb = _pool_call(p, a1 // width, w_pool_bf, pool_scale3, i, width, 0, b, l, st.rows, None)
        yb = _pool_call(p, a1 // width, w_pool_bf, pool_scale3, i, width, st.rows_lat, b, lc, st.rows, yb)
        q, k = _qk_prep(st, p, q0, k0, aw, kvw, q_gain, k_gain, i, hd)
        yc = _attention(st, q, k, p, v0, hd)
        merged = _merge(st, ya, yb, yc, p, g0, b_gate3, w_up_bf, i)
        xs = _mm_residual(st, merged, w_out_bf, (i,), xs, mods, i, 5, 1.0, 512)
        h = _adaln(st, xs, gains, mods, i, 2, 6, 7)
        a = _ffn_in(st, h, ffn_w_in, i, 1)
        xs = _mm_residual(st, a, ffn_w_out_bf, (i, 1), xs, mods, i, 8, 0.5, 512)

    out = pl.pallas_call(
        _rmsnorm_kernel,
        grid=(st.n_lat_tiles,),
        in_specs=[pl.BlockSpec((st.tm, d), lambda i: (i, 0)),
                  pl.BlockSpec((1, d), lambda i: (0, 0))],
        out_specs=pl.BlockSpec((st.tm, d), lambda i: (i, 0)),
        out_shape=jax.ShapeDtypeStruct((b * l, d), F32),
        compiler_params=_cparams(1),
        name="final_rmsnorm",
    )(xs, final_gain.reshape(1, d))
    return out.reshape(b, l, d)
```

```python
import functools
import math

import numpy as np
import jax
import jax.numpy as jnp
from jax import lax
from jax.experimental import pallas as pl
from jax.experimental.pallas import tpu as pltpu

F32 = jnp.float32
BF16 = jnp.bfloat16

N_HEADS = 16
N_KV_HEADS = 4
GRID_W = 64
ROPE_THETA = 10000.0
HYENA_ORDER = 2
HYENA_TARGET = 1e-2
HYENA_FAST_DECAY = 0.3
HYENA_SLOW_DECAY = 1.5
POOL_WINDOWS = (2, 4, 8, 16)
N_MOD = 9
N_BRANCH = 3
EPS = 1e-6
HIGHEST = lax.Precision.HIGHEST

VMEM_LIMIT_V7X = 56 * 1024 * 1024


def _cparams(n_axes):
    return pltpu.CompilerParams(
        dimension_semantics=("arbitrary",) * n_axes, vmem_limit_bytes=VMEM_LIMIT_V7X)


def _largest_divisor(total, pref, align):
    if total <= pref:
        return total
    t = (pref // align) * align
    while t > align and total % t:
        t -= align
    assert total % t == 0, (total, pref, align)
    return t


def _silu(v):
    return v * jax.nn.sigmoid(v)


def _mod_kernel(c_ref, w_ref, b_ref, o_ref):
    s = _silu(c_ref[...]).astype(BF16)
    w = w_ref[0].astype(BF16)
    o_ref[0] = jnp.dot(s, w, preferred_element_type=F32) + b_ref[0]


def _modulation(cvec, w_mod, b_mod):
    depth, d, nm = w_mod.shape
    tn = _largest_divisor(nm, 1024, 128)
    return pl.pallas_call(
        _mod_kernel,
        grid=(depth, nm // tn),
        in_specs=[pl.BlockSpec((8, d), lambda l, n: (0, 0)),
                  pl.BlockSpec((1, d, tn), lambda l, n: (l, 0, n)),
                  pl.BlockSpec((1, 1, tn), lambda l, n: (l, 0, n))],
        out_specs=pl.BlockSpec((1, 8, tn), lambda l, n: (l, 0, n)),
        out_shape=jax.ShapeDtypeStruct((depth, 8, nm), F32),
        compiler_params=_cparams(2),
        name="modulation",
    )(cvec, w_mod, b_mod.reshape(depth, 1, nm))


def _adaln_kernel(x_ref, g_ref, sh_ref, sc_ref, o_ref):
    x = x_ref[...]
    y = x * lax.rsqrt(jnp.mean(x * x, axis=-1, keepdims=True) + EPS)
    y = y * g_ref[0]
    o_ref[...] = (y * (1.0 + sc_ref[0]) + sh_ref[0]).astype(o_ref.dtype)


def _rmsnorm_kernel(x_ref, g_ref, o_ref):
    x = x_ref[...]
    y = x * lax.rsqrt(jnp.mean(x * x, axis=-1, keepdims=True) + EPS)
    o_ref[...] = y * g_ref[...]


class _Stream:
    def __init__(self, batch, seq, ctx_len):
        self.b, self.l, self.c = batch, seq, ctx_len
        self.rows_lat = batch * seq
        self.rows = batch * (seq + ctx_len)
        self.tm = _largest_divisor(math.gcd(seq, batch * ctx_len), 512, 16)
        self.tiles_per_seq = seq // self.tm
        self.n_lat_tiles = self.rows_lat // self.tm
        self.n_tiles = self.rows // self.tm
        self.tm_mm = _largest_divisor(self.rows, 1088, 16)
        self.n_mm_tiles = self.rows // self.tm_mm

    def group(self, i):
        return jnp.where(i < self.n_lat_tiles, i // self.tiles_per_seq, self.b)

    def per_row(self, tile, tile_rows, group_vals):
        row = tile * tile_rows + lax.broadcasted_iota(jnp.int32, (tile_rows, 1), 0)
        out = group_vals[self.b]
        for g in reversed(range(self.b)):
            out = jnp.where(row < (g + 1) * self.l, group_vals[g], out)
        return out


def _adaln(st, x, gains, mods, layer, j, m_shift, m_scale):
    d = x.shape[1]
    base = layer * (st.b + 1) * N_MOD

    def mod_map(m):
        return lambda i: (base + st.group(i) * N_MOD + m, 0, 0)

    return pl.pallas_call(
        _adaln_kernel,
        grid=(st.n_tiles,),
        in_specs=[pl.BlockSpec((st.tm, d), lambda i: (i, 0)),
                  pl.BlockSpec((1, 1, d), lambda i: (layer * 3 + j, 0, 0)),
                  pl.BlockSpec((1, 1, d), mod_map(m_shift)),
                  pl.BlockSpec((1, 1, d), mod_map(m_scale))],
        out_specs=pl.BlockSpec((st.tm, d), lambda i: (i, 0)),
        out_shape=jax.ShapeDtypeStruct(x.shape, BF16),
        compiler_params=_cparams(1),
        name="adaln",
    )(x, gains, mods, mods)


def _ffn_in_kernel(h_ref, wg_ref, wu_ref, o_ref, wg_s, wu_s):
    @pl.when(pl.program_id(1) == 0)
    def _():
        wg_s[...] = wg_ref[...].astype(BF16)
        wu_s[...] = wu_ref[...].astype(BF16)

    h = h_ref[...]
    g = jnp.dot(h, wg_s[...], preferred_element_type=F32)
    u = jnp.dot(h, wu_s[...], preferred_element_type=F32)
    o_ref[...] = (_silu(g) * u).astype(o_ref.dtype)


def _ffn_in(st, h, ffn_w_in, layer, j):
    d = h.shape[1]
    f = ffn_w_in.shape[3] // 2
    tn = _largest_divisor(f, 512, 128)
    nt = f // tn
    return pl.pallas_call(
        _ffn_in_kernel,
        grid=(nt, st.n_mm_tiles),
        in_specs=[pl.BlockSpec((st.tm_mm, d), lambda n, m: (m, 0)),
                  pl.BlockSpec((None, None, d, tn), lambda n, m: (layer, j, 0, n)),
                  pl.BlockSpec((None, None, d, tn), lambda n, m: (layer, j, 0, n + nt))],
        out_specs=pl.BlockSpec((st.tm_mm, tn), lambda n, m: (m, n)),
        out_shape=jax.ShapeDtypeStruct((st.rows, f), BF16),
        scratch_shapes=[pltpu.VMEM((d, tn), BF16), pltpu.VMEM((d, tn), BF16)],
        compiler_params=_cparams(2),
        name="ffn_in",
    )(h, ffn_w_in, ffn_w_in)


def _mm_res_kernel(a_ref, w_ref, x_ref, *rest, st, gate_scale):
    gate_refs, o_ref = rest[:-1], rest[-1]
    gate = st.per_row(pl.program_id(0), a_ref.shape[0], [g[0] for g in gate_refs])
    y = jnp.dot(a_ref[...], w_ref[...], preferred_element_type=F32)
    o_ref[...] = x_ref[...] + (gate_scale * gate) * y


def _mm_residual(st, a, w_bf, w_index, x, mods, layer, m_gate, gate_scale, tn_pref):
    k = a.shape[1]
    d = x.shape[1]
    tn = _largest_divisor(d, tn_pref, 128)
    base = layer * (st.b + 1) * N_MOD
    lead = (None,) * len(w_index)

    def gate_spec(g):
        return pl.BlockSpec((1, 1, tn), lambda m, n: (base + g * N_MOD + m_gate, 0, n))

    return pl.pallas_call(
        functools.partial(_mm_res_kernel, st=st, gate_scale=gate_scale),
        grid=(st.n_mm_tiles, d // tn),
        in_specs=[pl.BlockSpec((st.tm_mm, k), lambda m, n: (m, 0)),
                  pl.BlockSpec(lead + (k, tn), lambda m, n: tuple(w_index) + (0, n)),
                  pl.BlockSpec((st.tm_mm, tn), lambda m, n: (m, n))]
                 + [gate_spec(g) for g in range(st.b + 1)],
        out_specs=pl.BlockSpec((st.tm_mm, tn), lambda m, n: (m, n)),
        out_shape=jax.ShapeDtypeStruct(x.shape, F32),
        compiler_params=_cparams(2),
        name="matmul_residual",
    )(a, w_bf, x, *([mods] * (st.b + 1)))


def _mm_kernel(h_ref, w_ref, o_ref, w_s):
    @pl.when(pl.program_id(1) == 0)
    def _():
        w_s[...] = w_ref[...].astype(BF16)

    o_ref[...] = jnp.dot(h_ref[...], w_s[...], preferred_element_type=F32).astype(o_ref.dtype)


def _in_proj(st, h, w_in, layer):
    d = h.shape[1]
    n_in = w_in.shape[2]
    tn = _largest_divisor(n_in, 1024, 128)
    return pl.pallas_call(
        _mm_kernel,
        grid=(n_in // tn, st.n_mm_tiles),
        in_specs=[pl.BlockSpec((st.tm_mm, d), lambda n, m: (m, 0)),
                  pl.BlockSpec((None, d, tn), lambda n, m: (layer, 0, n))],
        out_specs=pl.BlockSpec((st.tm_mm, tn), lambda n, m: (m, n)),
        out_shape=jax.ShapeDtypeStruct((st.rows, n_in), BF16),
        scratch_shapes=[pltpu.VMEM((d, tn), BF16)],
        compiler_params=_cparams(2),
        name="in_proj",
    )(h, w_in)


def _merge_kernel(ya_ref, yb_ref, yc_ref, pa_ref, pb_ref, pc_ref, ba_ref, bb_ref, bc_ref, w_ref, o_ref):
    acc = None
    for k, (y_ref, p_ref, b_ref) in enumerate(
            ((ya_ref, pa_ref, ba_ref), (yb_ref, pb_ref, bb_ref), (yc_ref, pc_ref, bc_ref))):
        gate = jax.nn.sigmoid(p_ref[...].astype(F32) + b_ref[0])
        t = gate * jnp.dot(y_ref[...], w_ref[k], preferred_element_type=F32)
        acc = t if acc is None else acc + t
    o_ref[...] = acc.astype(o_ref.dtype)


def _merge(st, ya, yb, yc, p, gate_col0, b_gate3, w_up_bf, layer):
    w = ya.shape[1]
    d = w_up_bf.shape[3]
    tn = _largest_divisor(d, 256, 128)
    gb = gate_col0 // tn
    dt = d // tn
    tm = st.tm_mm
    y_spec = pl.BlockSpec((tm, w), lambda m, n: (m, 0))

    def p_spec(k):
        return pl.BlockSpec((tm, tn), lambda m, n: (m, gb + k * dt + n))

    def b_spec(k):
        return pl.BlockSpec((1, 1, tn), lambda m, n: (layer * N_BRANCH + k, 0, n))

    return pl.pallas_call(
        _merge_kernel,
        grid=(st.n_mm_tiles, dt),
        in_specs=[y_spec, y_spec, y_spec, p_spec(0), p_spec(1), p_spec(2),
                  b_spec(0), b_spec(1), b_spec(2),
                  pl.BlockSpec((None, N_BRANCH, w, tn), lambda m, n: (layer, 0, 0, n))],
        out_specs=pl.BlockSpec((tm, tn), lambda m, n: (m, n)),
        out_shape=jax.ShapeDtypeStruct((st.rows, d), BF16),
        compiler_params=_cparams(2),
        name="merge",
    )(ya, yb, yc, p, p, p, b_gate3, b_gate3, b_gate3, w_up_bf)


HALO = 16


def _short_conv_kernel(cur_ref, prev_ref, next_ref, w_ref, b_ref, o_ref, *, blocks_per_seq):
    j = pl.program_id(1) % blocks_per_seq
    u = cur_ref[...].astype(F32)
    t = u.shape[0]
    row = lax.broadcasted_iota(jnp.int32, u.shape, 0)
    prev_row = jnp.where(j == 0, 0.0, prev_ref[HALO - 1:HALO, :].astype(F32))
    next_row = jnp.where(j == blocks_per_seq - 1, 0.0, next_ref[0:1, :].astype(F32))
    up = jnp.where(row == 0, prev_row, pltpu.roll(u, 1, axis=0))
    un = jnp.where(row == t - 1, next_row, pltpu.roll(u, t - 1, axis=0))
    w = w_ref[...]
    o_ref[...] = (b_ref[...] + up * w[0:1] + u * w[1:2] + un * w[2:3]).astype(o_ref.dtype)


def _short_conv_call(p, conv_w, conv_b, width, row0, n_rows, seq_len, out_rows, prior):
    ts = _largest_divisor(seq_len, 512, HALO)
    ct = _largest_divisor(width, 1024, 128)
    blocks_per_seq = seq_len // ts
    rb0 = row0 // ts
    hb = ts // HALO
    n_hblocks = out_rows // HALO
    cpb = width // ct
    n_parts = HYENA_ORDER + 1

    def cur_map(c, r):
        return (rb0 + r, c)

    def prev_map(c, r):
        return (jnp.maximum((rb0 + r) * hb - 1, 0), c)

    def next_map(c, r):
        return (jnp.minimum((rb0 + r + 1) * hb, n_hblocks - 1), c)

    in_specs = [pl.BlockSpec((ts, ct), cur_map),
                pl.BlockSpec((HALO, ct), prev_map),
                pl.BlockSpec((HALO, ct), next_map),
                pl.BlockSpec((3, ct), lambda c, r: (0, c)),
                pl.BlockSpec((1, ct), lambda c, r: (0, c))]
    args = [p, p, p, conv_w, conv_b]
    aliases = {}
    kernel = functools.partial(_short_conv_kernel, blocks_per_seq=blocks_per_seq)
    if prior is not None:
        in_specs.append(pl.BlockSpec(memory_space=pl.ANY))
        args.append(prior)
        aliases = {5: 0}
        body = kernel
        kernel = lambda c, pv, nx, w, b, _prior, o: body(c, pv, nx, w, b, o)
    return pl.pallas_call(
        kernel,
        grid=(n_parts * cpb, n_rows // ts),
        in_specs=in_specs,
        out_specs=pl.BlockSpec((None, ts, ct), lambda c, r: (c // cpb, rb0 + r, c % cpb)),
        out_shape=jax.ShapeDtypeStruct((n_parts, out_rows, width), BF16),
        input_output_aliases=aliases,
        compiler_params=_cparams(2),
        name="short_conv",
    )(*args)


@functools.lru_cache(maxsize=None)
def _filter_positions(seq_len, emb_dim, pad_dim):
    bands = (emb_dim - 1) // 2
    j = np.arange(seq_len, dtype=np.float64)
    t = j / (seq_len - 1)
    wpos = 2.0 * np.pi * j / seq_len
    f = np.linspace(1e-4, bands - 1, bands)
    z = np.concatenate([t[:, None], np.cos(f[None] * wpos[:, None]), -np.sin(f[None] * wpos[:, None])], axis=1)
    lag = np.concatenate([np.arange(seq_len), [0], np.arange(seq_len - 1, 0, -1)])
    z2 = np.zeros((2 * seq_len, pad_dim), np.float64)
    z2[:, :emb_dim] = z[lag]
    return z2.astype(np.float32)


def _filter_mlp_kernel(z_ref, w1_ref, b1_ref, w2_ref, b2_ref, w3_ref, b3_ref, fr_ref, o_ref):
    fr = fr_ref[0]
    h = jnp.dot(z_ref[...], w1_ref[0], precision=HIGHEST, preferred_element_type=F32)
    h = jnp.sin(fr[0:1] * (h + b1_ref[0]))
    h = jnp.dot(h, w2_ref[0], precision=HIGHEST, preferred_element_type=F32)
    h = jnp.sin(fr[1:2] * (h + b2_ref[0]))
    h = jnp.dot(h, w3_ref[0], precision=HIGHEST, preferred_element_type=F32)
    o_ref[0] = jnp.sin(fr[2:3] * (h + b3_ref[0]))


def _filter_mlp(seq_len, w1, b1, w2, b2, w3, b3, freq):
    depth, emb, hid = w1.shape
    pad = -(-emb // 8) * 8
    n = 2 * seq_len
    z2 = jnp.asarray(_filter_positions(seq_len, emb, pad))
    w1p = jnp.pad(w1, ((0, 0), (0, pad - emb), (0, 0)))
    tr = _largest_divisor(n, 1024, 8)
    wspec = lambda k: pl.BlockSpec((1, k, hid), lambda l, r: (l, 0, 0))
    bspec = pl.BlockSpec((1, 1, hid), lambda l, r: (l, 0, 0))
    return pl.pallas_call(
        _filter_mlp_kernel,
        grid=(depth, n // tr),
        in_specs=[pl.BlockSpec((tr, pad), lambda l, r: (r, 0)),
                  wspec(pad), bspec, wspec(hid), bspec, wspec(hid), bspec,
                  pl.BlockSpec((1, 3, hid), lambda l, r: (l, 0, 0))],
        out_specs=pl.BlockSpec((1, tr, hid), lambda l, r: (l, r, 0)),
        out_shape=jax.ShapeDtypeStruct((depth, n, hid), F32),
        compiler_params=_cparams(2),
        name="filter_mlp",
    )(z2, w1p, b1.reshape(depth, 1, hid), w2, b2.reshape(depth, 1, hid), w3, b3.reshape(depth, 1, hid), freq)


def _filter_taps_kernel(h_ref, wf_ref, wb_ref, delta_ref, o_ref, *, seq_len):
    l = seq_len
    delta = delta_ref[...]
    lag = lax.broadcasted_iota(jnp.int32, (l, 1), 0)
    t_top = lag.astype(F32) / (l - 1.0)
    t_bot = jnp.where(lag == 0, 0, l - lag).astype(F32) / (l - 1.0)
    top = jnp.dot(h_ref[0, 0:l, :].astype(BF16), wf_ref[0].astype(BF16), preferred_element_type=F32)
    bot = jnp.dot(h_ref[0, l:2 * l, :].astype(BF16), wb_ref[0].astype(BF16), preferred_element_type=F32)
    top = top * jnp.exp(-t_top * delta)
    bot = bot * jnp.exp(-t_bot * delta)
    top = top + jnp.where(lag == 0, bot[0:1, :], 0.0)
    bot = jnp.where(lag == 0, 0.0, bot)
    ss = jnp.sum(top * top, axis=0, keepdims=True) + jnp.sum(bot * bot, axis=0, keepdims=True)
    scale = lax.rsqrt(ss + EPS)
    o_ref[0, 0:l, :] = (top * scale).astype(o_ref.dtype)
    o_ref[0, l:2 * l, :] = (bot * scale).astype(o_ref.dtype)


def _filter_taps(h_all, filt_w4, layer, seq_len, width):
    hid = h_all.shape[2]
    n = 2 * seq_len
    ct = 128
    cpb = width // ct
    min_decay = math.log(HYENA_TARGET) / HYENA_SLOW_DECAY
    max_decay = math.log(HYENA_TARGET) / HYENA_FAST_DECAY
    delta = jnp.asarray(np.abs(np.linspace(min_decay, max_decay, width)).astype(np.float32)).reshape(1, width)
    return pl.pallas_call(
        functools.partial(_filter_taps_kernel, seq_len=seq_len),
        grid=(HYENA_ORDER, cpb),
        in_specs=[pl.BlockSpec((1, n, hid), lambda o, c: (layer, 0, 0)),
                  pl.BlockSpec((1, hid, ct), lambda o, c: (layer, 0, (2 * o) * cpb + c)),
                  pl.BlockSpec((1, hid, ct), lambda o, c: (layer, 0, (2 * o + 1) * cpb + c)),
                  pl.BlockSpec((1, ct), lambda o, c: (0, c))],
        out_specs=pl.BlockSpec((1, n, ct), lambda o, c: (o, 0, c)),
        out_shape=jax.ShapeDtypeStruct((HYENA_ORDER, n, width), BF16),
        compiler_params=_cparams(2),
        name="filter_taps",
    )(h_all, filt_w4, filt_w4, delta)


def _real_form(e):
    return np.block([[e.real, -e.imag], [e.imag, e.real]])


@functools.lru_cache(maxsize=None)
def _dft_tables(n1, n2):
    n = n1 * n2
    j2 = np.arange(n2)[:, None, None]
    k1 = np.arange(n1)[None, :, None]
    j1 = np.arange(n1)[None, None, :]
    e = np.exp(-2j * np.pi * ((j1 * k1) / n1 + (j2 * k1) / n))
    eh = e[:, :, :n1 // 2]
    t_data = np.concatenate(
        [np.concatenate([eh.real, -eh.imag], axis=2), np.concatenate([eh.imag, eh.real], axis=2)], axis=1)
    t_filt = np.concatenate([e.real, e.imag], axis=1)
    hinv = np.conj(np.transpose(eh, (0, 2, 1))) / n
    t_inv = np.concatenate(
        [np.concatenate([hinv.real, -hinv.imag], axis=2), np.concatenate([hinv.imag, hinv.real], axis=2)], axis=1)
    f2 = np.exp(-2j * np.pi * np.outer(np.arange(n2), np.arange(n2)) / n2)
    as32 = lambda a: np.ascontiguousarray(a, dtype=np.float32)
    return as32(t_data), as32(t_filt), as32(t_inv), as32(_real_form(f2)), as32(_real_form(np.conj(f2)))


def _split_n(n):
    n2 = 1
    while n2 < 128 and n // (n2 * 2) >= 64:
        n2 *= 2
    return n // n2, n2


GS = 16
LANES = 128


def _spread(src_ref, dst_ref):
    a, gs, _ = src_ref.shape
    for c in range(dst_ref.shape[0]):
        dst_ref[c] = src_ref[:, :, c * LANES:(c + 1) * LANES].reshape(a * gs, LANES).astype(F32)


def _gather(s_ref, g, count, stride):
    return jnp.concatenate(
        [s_ref[c, pl.ds(g, count, stride=stride), :] for c in range(s_ref.shape[0])], axis=1)


def _fft_s1_kernel(z_ref, t_ref, o_ref, s_ref):
    n1, gs, _ = z_ref.shape
    _spread(z_ref, s_ref)
    for g in range(gs):
        z = _gather(s_ref, g, n1, gs).astype(BF16)
        o_ref[g] = jnp.dot(t_ref[g], z, preferred_element_type=F32).astype(o_ref.dtype)


def _fft_mid_kernel(ar_ref, ai_ref, fr_ref, fi_ref, f2_ref, f2i_ref, o_ref, sar, sai, sfr, sfi):
    n2, gs, _ = ar_ref.shape
    for src, dst in ((ar_ref, sar), (ai_ref, sai), (fr_ref, sfr), (fi_ref, sfi)):
        _spread(src, dst)
    f2 = f2_ref[...]
    f2i = f2i_ref[...]
    def spectra(g):
        a = jnp.concatenate([_gather(sar, g, n2, gs), _gather(sai, g, n2, gs)], axis=0).astype(BF16)
        f = jnp.concatenate([_gather(sfr, g, n2, gs), _gather(sfi, g, n2, gs)], axis=0).astype(BF16)
        return (jnp.dot(f2, a, preferred_element_type=F32),
                jnp.dot(f2, f, preferred_element_type=F32))

    nxt = spectra(0)
    for g in range(gs):
        u, k = nxt
        if g + 1 < gs:
            nxt = spectra(g + 1)
        ur, ui, kr, ki = u[:n2], u[n2:], k[:n2], k[n2:]
        y = jnp.concatenate([ur * kr - ui * ki, ur * ki + ui * kr], axis=0).astype(BF16)
        o_ref[g] = jnp.dot(f2i, y, preferred_element_type=F32).astype(o_ref.dtype)


def _fft_s3_kernel(cr_ref, ci_ref, t_ref, u_ref, x_ref, bias_ref, o_ref, scr, sci, sy):
    n1, gs, _ = cr_ref.shape
    _spread(cr_ref, scr)
    _spread(ci_ref, sci)
    for g in range(gs):
        c = jnp.concatenate([_gather(scr, g, n1, gs), _gather(sci, g, n1, gs)], axis=0).astype(BF16)
        y = jnp.dot(t_ref[g], c, preferred_element_type=F32)
        for cc in range(sy.shape[0]):
            sy[cc, g * n1:(g + 1) * n1, :] = y[:, cc * LANES:(cc + 1) * LANES]
    bias = bias_ref[...]
    for r in range(n1):
        y = _gather(sy, r, gs, n1)
        u = u_ref[r].astype(F32)
        o_ref[r] = (x_ref[r].astype(F32) * (y + u * bias)).astype(o_ref.dtype)


def _hyena_small_kernel(u_ref, k_ref, x_ref, bias_ref, td_ref, tf_ref, ti_ref, _latent_rows_ref, o_ref, *, n):
    u = u_ref[...]
    a = jnp.dot(td_ref[...], u, preferred_element_type=F32)
    k = jnp.dot(tf_ref[...], k_ref[...], preferred_element_type=F32)
    ar, ai, kr, ki = a[:n], a[n:], k[:n], k[n:]
    y = jnp.concatenate([ar * kr - ai * ki, ar * ki + ai * kr], axis=0).astype(BF16)
    y = jnp.dot(ti_ref[...], y, preferred_element_type=F32)
    o_ref[...] = (x_ref[...].astype(F32) * (y + u.astype(F32) * bias_ref[...])).astype(o_ref.dtype)


def _bf16_table(a):
    return jnp.asarray(a).astype(BF16)


def _long_conv_gate(st, u3, u_part, x_arr, x_part, kk, kk_ctx, order, bias):
    assert st.b == 2, "the two batch elements are packed as one complex signal"
    _, rows, width = u3.shape
    l, lc = st.l, st.c
    n = 2 * l
    n1, n2 = _split_n(n)
    t_data, t_filt, t_inv, f2, f2i = (_bf16_table(a) for a in _dft_tables(n1, n2))
    assert n1 % GS == 0 and n2 % GS == 0 and width % LANES == 0
    bias2 = bias.reshape(1, width)
    slabs = rows // n2
    nl = width // LANES

    def stage1(z4, part, table, name):
        return pl.pallas_call(
            _fft_s1_kernel,
            grid=(n2 // GS,),
            in_specs=[pl.BlockSpec((None, n1, GS, width), lambda j: (part, 0, j, 0)),
                      pl.BlockSpec((GS, 2 * n1, n1), lambda j: (j, 0, 0))],
            out_specs=pl.BlockSpec((GS, 2 * n1, width), lambda j: (j, 0, 0)),
            out_shape=jax.ShapeDtypeStruct((n2, 2 * n1, width), BF16),
            scratch_shapes=[pltpu.VMEM((nl, n1 * GS, LANES), F32)],
            compiler_params=_cparams(1),
            name=name,
        )(z4, table)

    a_filt = stage1(kk.reshape(HYENA_ORDER, n1, n2, width), order, t_filt, "fft_stage1_filter")
    u_v = u3.reshape(u3.shape[0], slabs, n2, width)
    a_data = stage1(u_v, u_part, t_data, "fft_stage1_data")

    ctm = _largest_divisor(width, 256, LANES)
    kb_im = n1 // GS
    re_spec = pl.BlockSpec((n2, GS, ctm), lambda c, k: (0, k, c))
    im_spec = pl.BlockSpec((n2, GS, ctm), lambda c, k: (0, kb_im + k, c))
    mat_spec = pl.BlockSpec((2 * n2, 2 * n2), lambda c, k: (0, 0))
    c_mid = pl.pallas_call(
        _fft_mid_kernel,
        grid=(width // ctm, n1 // GS),
        in_specs=[re_spec, im_spec, re_spec, im_spec, mat_spec, mat_spec],
        out_specs=pl.BlockSpec((GS, 2 * n2, ctm), lambda c, k: (k, 0, c)),
        out_shape=jax.ShapeDtypeStruct((n1, 2 * n2, width), BF16),
        scratch_shapes=[pltpu.VMEM((ctm // LANES, n2 * GS, LANES), F32)] * 4,
        compiler_params=_cparams(2),
        name="fft_mid",
    )(a_data, a_data, a_filt, a_filt, f2, f2i)

    cts = _largest_divisor(width, 1024, LANES)
    jb_im = n2 // GS
    x_v = x_arr.reshape(x_arr.shape[0], slabs, n2, width)
    out = pl.pallas_call(
        _fft_s3_kernel,
        grid=(width // cts, n2 // GS),
        in_specs=[pl.BlockSpec((n1, GS, cts), lambda c, j: (0, j, c)),
                  pl.BlockSpec((n1, GS, cts), lambda c, j: (0, jb_im + j, c)),
                  pl.BlockSpec((GS, n1, 2 * n1), lambda c, j: (j, 0, 0)),
                  pl.BlockSpec((None, n1, GS, cts), lambda c, j: (u_part, 0, j, c)),
                  pl.BlockSpec((None, n1, GS, cts), lambda c, j: (x_part, 0, j, c)),
                  pl.BlockSpec((1, cts), lambda c, j: (0, c))],
        out_specs=pl.BlockSpec((n1, GS, cts), lambda c, j: (0, j, c)),
        out_shape=jax.ShapeDtypeStruct((slabs, n2, width), BF16),
        scratch_shapes=[pltpu.VMEM((cts // LANES, n1 * GS, LANES), F32)] * 3,
        compiler_params=_cparams(2),
        name="fft_stage3",
    )(c_mid, c_mid, t_inv, u_v, x_v, bias2)
    out = out.reshape(rows, width)

    nc = 2 * lc
    tc_data, tc_filt, tc_inv, _, _ = (_bf16_table(a) for a in _dft_tables(nc, 1))
    ct = _largest_divisor(width, 512, 128)
    rb = (st.rows_lat) // (2 * lc)
    out = pl.pallas_call(
        functools.partial(_hyena_small_kernel, n=nc),
        grid=(width // ct,),
        in_specs=[pl.BlockSpec((None, 2 * lc, ct), lambda c: (u_part, rb, c)),
                  pl.BlockSpec((None, nc, ct), lambda c: (order, 0, c)),
                  pl.BlockSpec((None, 2 * lc, ct), lambda c: (x_part, rb, c)),
                  pl.BlockSpec((1, ct), lambda c: (0, c)),
                  pl.BlockSpec((2 * nc, nc), lambda c: (0, 0)),
                  pl.BlockSpec((2 * nc, nc), lambda c: (0, 0)),
                  pl.BlockSpec((nc, 2 * nc), lambda c: (0, 0)),
                  pl.BlockSpec(memory_space=pl.ANY)],
        out_specs=pl.BlockSpec((2 * lc, ct), lambda c: (rb, c)),
        out_shape=jax.ShapeDtypeStruct((rows, width), BF16),
        input_output_aliases={7: 0},
        compiler_params=_cparams(1),
        name="hyena_context",
    )(u3, kk_ctx, x_arr, bias2, tc_data[0], tc_filt[0], tc_inv[0], out)
    return out


@functools.lru_cache(maxsize=None)
def _pool_tables(t, windows):
    out = np.zeros((4, len(windows), t, 3 * t), np.float64)
    for v in range(4):
        lo_bound = t if v & 1 else 0
        hi_bound = 2 * t - 1 if v & 2 else 3 * t - 1
        for g, w in enumerate(windows):
            for r in range(t):
                pos = t + r
                lo = max(pos - w // 2, lo_bound)
                hi = min(pos + w - 1 - w // 2, hi_bound)
                out[v, g, r, lo:hi + 1] = 1.0 / (hi - lo + 1)
                out[v, g, r, pos] -= 1.0
    return out.astype(np.float32)


def _pool_kernel(prev_ref, cur_ref, next_ref, m_ref, w_ref, s_ref, *rest, n_groups, gw):
    o_ref = rest[-1]
    for g in range(n_groups):
        sl = slice(g * gw, (g + 1) * gw)
        u = jnp.concatenate([prev_ref[:, sl], cur_ref[:, sl], next_ref[:, sl]], axis=0)
        d = jnp.dot(m_ref[0, g], u, preferred_element_type=F32)
        y = jnp.dot(d.astype(BF16), w_ref[g], preferred_element_type=F32)
        o_ref[:, sl] = (y * s_ref[:, sl]).astype(o_ref.dtype)


def _pool_call(p, col_block, w_pool_bf, pool_scale3, layer, width, row0, n_seq, seq_len, out_rows, prior):
    t = _largest_divisor(seq_len, 256, 16)
    nblk = seq_len // t
    rb0 = row0 // t
    n_groups = len(POOL_WINDOWS)
    gw = width // n_groups
    tables = _bf16_table(_pool_tables(t, POOL_WINDOWS))

    def blk(delta):
        return lambda s, j: (rb0 + s * nblk + jnp.clip(j + delta, 0, nblk - 1), col_block)

    def variant(s, j):
        return ((j == 0).astype(jnp.int32) + 2 * (j == nblk - 1).astype(jnp.int32), 0, 0, 0)

    in_specs = [pl.BlockSpec((t, width), blk(-1)),
                pl.BlockSpec((t, width), blk(0)),
                pl.BlockSpec((t, width), blk(1)),
                pl.BlockSpec((1, n_groups, t, 3 * t), variant),
                pl.BlockSpec((None, n_groups, gw, gw), lambda s, j: (layer, 0, 0, 0)),
                pl.BlockSpec((None, 1, width), lambda s, j: (layer, 0, 0))]
    args = [p, p, p, tables, w_pool_bf, pool_scale3]
    aliases = {}
    if prior is not None:
        in_specs.append(pl.BlockSpec(memory_space=pl.ANY))
        args.append(prior)
        aliases = {6: 0}
    return pl.pallas_call(
        functools.partial(_pool_kernel, n_groups=n_groups, gw=gw),
        grid=(n_seq, nblk),
        in_specs=in_specs,
        out_specs=pl.BlockSpec((t, width), lambda s, j: (rb0 + s * nblk + j, 0)),
        out_shape=jax.ShapeDtypeStruct((out_rows, width), BF16),
        input_output_aliases=aliases,
        compiler_params=_cparams(2),
        name="pool",
    )(*args)


@functools.lru_cache(maxsize=None)
def _rope_tables(seq_len, grid_w, hd, pad_rows):
    axis = hd // 2
    pos = np.arange(seq_len)
    inv = ROPE_THETA ** (-np.arange(0, axis, 2, dtype=np.float64) / axis)
    ang_r = (pos // grid_w)[:, None] * inv[None]
    ang_c = (pos % grid_w)[:, None] * inv[None]
    cos = np.concatenate([np.cos(ang_r), np.cos(ang_r), np.cos(ang_c), np.cos(ang_c)], axis=1)
    sin = np.concatenate([-np.sin(ang_r), np.sin(ang_r), -np.sin(ang_c), np.sin(ang_c)], axis=1)
    cos = np.concatenate([cos, np.ones((pad_rows, hd))], axis=0)
    sin = np.concatenate([sin, np.zeros((pad_rows, hd))], axis=0)
    return cos.astype(np.float32), sin.astype(np.float32)


def _qk_prep_kernel(q_ref, k_ref, v_ref, qg_ref, kg_ref, cos_ref, sin_ref, qto_ref, ko_ref, vto_ref,
                    *, hd, q_scale):
    cos = cos_ref[...]
    sin = sin_ref[...]
    lane = lax.broadcasted_iota(jnp.int32, cos.shape, 1)
    first = (lane % (hd // 2)) < (hd // 4)
    ones = jnp.ones((hd, hd), BF16)

    def normed_rotated(x_ref, h, g_ref, mul):
        x = x_ref[:, h * hd:(h + 1) * hd].astype(F32)
        ss = jnp.dot((x * x).astype(BF16), ones, preferred_element_type=F32)
        y = x * lax.rsqrt(ss * (1.0 / hd) + EPS) * g_ref[...]
        partner = jnp.where(first, pltpu.roll(y, hd - hd // 4, axis=1), pltpu.roll(y, hd // 4, axis=1))
        return (y * cos + partner * sin) * mul

    for h in range(q_ref.shape[1] // hd):
        qto_ref[h * hd:(h + 1) * hd, :] = normed_rotated(q_ref, h, qg_ref, q_scale).T.astype(qto_ref.dtype)
    for h in range(k_ref.shape[1] // hd):
        sl = slice(h * hd, (h + 1) * hd)
        ko_ref[:, sl] = normed_rotated(k_ref, h, kg_ref, 1.0).astype(ko_ref.dtype)
        vto_ref[sl, :] = v_ref[:, sl].astype(F32).T.astype(vto_ref.dtype)


def _qk_prep(st, p, q_col0, k_col0, aw, kvw, q_gain, k_gain, layer, hd):
    cos_np, sin_np = _rope_tables(st.l, GRID_W, hd, st.tm)
    tab_map = lambda i: (jnp.where(i < st.n_lat_tiles, i % st.tiles_per_seq, st.tiles_per_seq), 0)
    return pl.pallas_call(
        functools.partial(_qk_prep_kernel, hd=hd, q_scale=hd ** -0.5),
        grid=(st.n_tiles,),
        in_specs=[pl.BlockSpec((st.tm, aw), lambda i: (i, q_col0 // aw)),
                  pl.BlockSpec((st.tm, kvw), lambda i: (i, k_col0 // kvw)),
                  pl.BlockSpec((st.tm, kvw), lambda i: (i, k_col0 // kvw + 1)),
                  pl.BlockSpec((None, 1, hd), lambda i: (layer, 0, 0)),
                  pl.BlockSpec((None, 1, hd), lambda i: (layer, 0, 0)),
                  pl.BlockSpec((st.tm, hd), tab_map),
                  pl.BlockSpec((st.tm, hd), tab_map)],
        out_specs=[pl.BlockSpec((aw, st.tm), lambda i: (0, i)),
                   pl.BlockSpec((st.tm, kvw), lambda i: (i, 0)),
                   pl.BlockSpec((kvw, st.tm), lambda i: (0, i))],
        out_shape=[jax.ShapeDtypeStruct((aw, st.rows), BF16),
                   jax.ShapeDtypeStruct((st.rows, kvw), BF16),
                   jax.ShapeDtypeStruct((kvw, st.rows), BF16)],
        compiler_params=_cparams(1),
        name="qk_prep",
    )(p, p, p, q_gain.reshape(-1, 1, hd), k_gain.reshape(-1, 1, hd), jnp.asarray(cos_np), jnp.asarray(sin_np))


ONES_ROWS = 16


def _attn_kernel(*refs, group, hd, tk, n_chunks):
    qt_ref, kc_ref, vtc_ref = refs[:3]
    o_ref, m_s, acc_s, s_a, s_b = refs[-5:]
    tq = qt_ref.shape[1]
    qs = jnp.concatenate([qt_ref[g * hd:(g + 1) * hd, :] for g in range(group)], axis=1)
    m_s[...] = jnp.full(m_s.shape, -1e30, F32)
    acc_s[...] = jnp.zeros(acc_s.shape, F32)

    def scores(k):
        return jnp.dot(k, qs, preferred_element_type=F32)

    def softmax_pv(s, vt):
        vt1 = jnp.concatenate([vt, jnp.ones((ONES_ROWS, vt.shape[1]), BF16)], axis=0)
        m_prev = m_s[...]
        m_new = jnp.maximum(m_prev, jnp.max(s, axis=0, keepdims=True))
        alpha = jnp.exp(m_prev - m_new)
        p = jnp.exp(s - m_new).astype(BF16)
        acc_s[...] = alpha * acc_s[...] + jnp.dot(vt1, p, preferred_element_type=F32)
        m_s[...] = m_new

    s_ctx = scores(kc_ref[...])
    if not n_chunks:
        softmax_pv(s_ctx, vtc_ref[...])
    else:
        kl_ref, vtl_ref = refs[3:5]

        def keys(c):
            return kl_ref[pl.ds(pl.multiple_of(c * tk, tk), tk), :]

        def vals(c):
            return vtl_ref[:, pl.ds(pl.multiple_of(c * tk, tk), tk)]

        first = n_chunks % 2
        if n_chunks - first:
            s_a[...] = scores(keys(first))
        softmax_pv(s_ctx, vtc_ref[...])
        if first:
            softmax_pv(scores(keys(0)), vals(0))
        if n_chunks - first:

            def body(j, carry):
                c = first + 2 * j
                s_b[...] = scores(keys(c + 1))
                softmax_pv(s_a[...], vals(c))
                s_a[...] = scores(keys(jnp.minimum(c + 2, n_chunks - 1)))
                softmax_pv(s_b[...], vals(c + 1))
                return carry

            lax.fori_loop(0, (n_chunks - first) // 2, body, 0)
    o = acc_s[0:hd, :] / acc_s[hd:hd + 1, :]
    for g in range(group):
        o_ref[:, g * hd:(g + 1) * hd] = o[:, g * tq:(g + 1) * tq].T.astype(o_ref.dtype)


def _attention(st, q_t, k, v_t, hd):
    group = N_HEADS // N_KV_HEADS
    gw = group * hd
    aw = N_HEADS * hd
    tq = _largest_divisor(st.l, 256, LANES)
    tk = _largest_divisor(st.l, 512, LANES)
    qt = st.l // tq
    cb0 = st.rows_lat // st.c

    def scratch(q_rows):
        nq = group * q_rows
        return [pltpu.VMEM((1, nq), F32), pltpu.VMEM((hd + ONES_ROWS, nq), F32),
                pltpu.VMEM((tk, nq), F32), pltpu.VMEM((tk, nq), F32)]

    yc = pl.pallas_call(
        functools.partial(_attn_kernel, group=group, hd=hd, tk=tk, n_chunks=st.l // tk),
        grid=(st.b, N_KV_HEADS, qt),
        in_specs=[pl.BlockSpec((gw, tq), lambda b, h, t: (h, b * qt + t)),
                  pl.BlockSpec((st.c, hd), lambda b, h, t: (cb0 + b, h)),
                  pl.BlockSpec((hd, st.c), lambda b, h, t: (h, cb0 + b)),
                  pl.BlockSpec((st.l, hd), lambda b, h, t: (b, h)),
                  pl.BlockSpec((hd, st.l), lambda b, h, t: (h, b))],
        out_specs=pl.BlockSpec((tq, gw), lambda b, h, t: (b * qt + t, h)),
        out_shape=jax.ShapeDtypeStruct((st.rows, aw), BF16),
        scratch_shapes=scratch(tq),
        compiler_params=_cparams(3),
        name="attention_latent",
    )(q_t, k, v_t, k, v_t)
    return pl.pallas_call(
        functools.partial(_attn_kernel, group=group, hd=hd, tk=tk, n_chunks=0),
        grid=(st.b, N_KV_HEADS),
        in_specs=[pl.BlockSpec((gw, st.c), lambda b, h: (h, cb0 + b)),
                  pl.BlockSpec((st.c, hd), lambda b, h: (cb0 + b, h)),
                  pl.BlockSpec((hd, st.c), lambda b, h: (h, cb0 + b)),
                  pl.BlockSpec(memory_space=pl.ANY)],
        out_specs=pl.BlockSpec((st.c, gw), lambda b, h: (cb0 + b, h)),
        out_shape=jax.ShapeDtypeStruct((st.rows, aw), BF16),
        input_output_aliases={3: 0},
        scratch_shapes=scratch(st.c),
        compiler_params=_cparams(2),
        name="attention_context",
    )(q_t, k, v_t, yc)


def kernel(x, c, ctx, c_ctx, w_mod, b_mod, norm_gain, final_gain, ffn_w_in, ffn_w_out, w_in, b_gate, conv_w, conv_b, filt_w1, filt_b1, filt_w2, filt_b2, filt_w3, filt_b3, filt_w4, filt_freq, hyena_bias, w_pool, pool_scale, q_gain, k_gain, w_up, w_out):
    b, l, d = x.shape
    lc = ctx.shape[1]
    depth = w_mod.shape[0]
    width = hyena_bias.shape[2]
    hd = q_gain.shape[1]
    aw, kvw = N_HEADS * hd, N_KV_HEADS * hd
    a1 = (HYENA_ORDER + 1) * width
    q0 = a1 + width
    k0 = q0 + aw
    v0 = k0 + kvw
    g0 = v0 + kvw
    st = _Stream(b, l, lc)

    xs = jnp.concatenate([x.reshape(b * l, d), ctx.reshape(b * lc, d)], axis=0)
    cvec = jnp.zeros((8, d), F32).at[:b].set(c).at[b].set(c_ctx)
    mods = _modulation(cvec, w_mod, b_mod).reshape(depth, 8, N_MOD, d)[:, :b + 1]
    mods = mods.reshape(depth * (b + 1) * N_MOD, 1, d)
    gains = norm_gain.reshape(depth * 3, 1, d)
    b_gate3 = b_gate.reshape(depth * N_BRANCH, 1, d)
    pool_scale3 = pool_scale.reshape(depth, 1, width)
    w_pool_bf = w_pool.astype(BF16)
    ffn_w_out_bf = ffn_w_out.astype(BF16)
    w_up_bf = w_up.astype(BF16)
    w_out_bf = w_out.astype(BF16)
    h_lat =_filter_mlp(l, filt_w1, filt_b1, filt_w2, filt_b2, filt_w3, filt_b3, filt_freq)
    h_ctx = _filter_mlp(lc, filt_w1, filt_b1, filt_w2, filt_b2, filt_w3, filt_b3, filt_freq)

    for i in range(depth):
        h = _adaln(st, xs, gains, mods, i, 0, 0, 1)
        a = _ffn_in(st, h, ffn_w_in, i, 0)
        xs = _mm_residual(st, a, ffn_w_out_bf, (i, 0), xs, mods, i, 2, 0.5, 512)
        h = _adaln(st, xs, gains, mods, i, 1, 3, 4)
        p = _in_proj(st, h, w_in, i)
        cw, cb = conv_w[i], conv_b[i].reshape(1, -1)
        u3 = _short_conv_call(p, cw, cb, width, 0, st.rows_lat, l, st.rows, None)
        u3 = _short_conv_call(p, cw, cb, width, st.rows_lat, b * lc, lc, st.rows, u3)
        kk = _filter_taps(h_lat, filt_w4, i, l, width)
        kk_ctx = _filter_taps(h_ctx, filt_w4, i, lc, width)
        z = _long_conv_gate(st, u3, 0, u3, 1, kk, kk_ctx, 0, hyena_bias[i, 0])
        ya = _long_conv_gate(st, z[None], 0, u3, 2, kk, kk_ctx, 1, hyena_bias[i, 1])
        yb = _pool_call(p, a1 // width, w_pool_bf, pool_scale3, i, width, 0, b, l, st.rows, None)
        yb = _pool_call(p, a1 // width, w_pool_bf, pool_scale3, i, width, st.rows_lat, b, lc, st.rows, yb)
        q_t, k, v_t = _qk_prep(st, p, q0, k0, aw, kvw, q_gain, k_gain, i, hd)
        yc = _attention(st, q_t, k, v_t, hd)
        merged = _merge(st, ya, yb, yc, p, g0, b_gate3, w_up_bf, i)
        xs = _mm_residual(st, merged, w_out_bf, (i,), xs, mods, i, 5, 1.0, 512)
        h = _adaln(st, xs, gains, mods, i, 2, 6, 7)
        a = _ffn_in(st, h, ffn_w_in, i, 1)
        xs = _mm_residual(st, a, ffn_w_out_bf, (i, 1), xs, mods, i, 8, 0.5, 512)

    out = pl.pallas_call(
        _rmsnorm_kernel,
        grid=(st.n_lat_tiles,),
        in_specs=[pl.BlockSpec((st.tm, d), lambda i: (i, 0)),
                  pl.BlockSpec((1, d), lambda i: (0, 0))],
        out_specs=pl.BlockSpec((st.tm, d), lambda i: (i, 0)),
        out_shape=jax.ShapeDtypeStruct((b * l, d), F32),
        compiler_params=_cparams(1),
        name="final_rmsnorm",
    )(xs, final_gain.reshape(1, d))
    return out.reshape(b, l, d)
```

```python
import functools
import math

import numpy as np
import jax
import jax.numpy as jnp
from jax import lax
from jax.experimental import pallas as pl
from jax.experimental.pallas import tpu as pltpu

F32 = jnp.float32
BF16 = jnp.bfloat16

N_HEADS = 16
N_KV_HEADS = 4
GRID_W = 64
ROPE_THETA = 10000.0
HYENA_ORDER = 2
HYENA_TARGET = 1e-2
HYENA_FAST_DECAY = 0.3
HYENA_SLOW_DECAY = 1.5
POOL_WINDOWS = (2, 4, 8, 16)
N_MOD = 9
N_BRANCH = 3
EPS = 1e-6
HIGHEST = lax.Precision.HIGHEST

VMEM_LIMIT_V7X = 56 * 1024 * 1024


def _cparams(n_axes):
    return pltpu.CompilerParams(
        dimension_semantics=("arbitrary",) * n_axes, vmem_limit_bytes=VMEM_LIMIT_V7X)


def _largest_divisor(total, pref, align):
    if total <= pref:
        return total
    t = (pref // align) * align
    while t > align and total % t:
        t -= align
    assert total % t == 0, (total, pref, align)
    return t


def _silu(v):
    return v * jax.nn.sigmoid(v)


def _mod_kernel(c_ref, w_ref, b_ref, o_ref):
    s = _silu(c_ref[...]).astype(BF16)
    w = w_ref[0].astype(BF16)
    o_ref[0] = jnp.dot(s, w, preferred_element_type=F32) + b_ref[0]


def _modulation(cvec, w_mod, b_mod):
    depth, d, nm = w_mod.shape
    tn = _largest_divisor(nm, 1024, 128)
    return pl.pallas_call(
        _mod_kernel,
        grid=(depth, nm // tn),
        in_specs=[pl.BlockSpec((8, d), lambda l, n: (0, 0)),
                  pl.BlockSpec((1, d, tn), lambda l, n: (l, 0, n)),
                  pl.BlockSpec((1, 1, tn), lambda l, n: (l, 0, n))],
        out_specs=pl.BlockSpec((1, 8, tn), lambda l, n: (l, 0, n)),
        out_shape=jax.ShapeDtypeStruct((depth, 8, nm), F32),
        compiler_params=_cparams(2),
        name="modulation",
    )(cvec, w_mod, b_mod.reshape(depth, 1, nm))


def _adaln_kernel(x_ref, g_ref, sh_ref, sc_ref, o_ref):
    x = x_ref[...]
    y = x * lax.rsqrt(jnp.mean(x * x, axis=-1, keepdims=True) + EPS)
    y = y * g_ref[0]
    o_ref[...] = (y * (1.0 + sc_ref[0]) + sh_ref[0]).astype(o_ref.dtype)


def _rmsnorm_kernel(x_ref, g_ref, o_ref):
    x = x_ref[...]
    y = x * lax.rsqrt(jnp.mean(x * x, axis=-1, keepdims=True) + EPS)
    o_ref[...] = y * g_ref[...]


class _Stream:
    def __init__(self, batch, seq, ctx_len):
        self.b, self.l, self.c = batch, seq, ctx_len
        self.rows_lat = batch * seq
        self.rows = batch * (seq + ctx_len)
        self.tm = _largest_divisor(math.gcd(seq, batch * ctx_len), 512, 16)
        self.tiles_per_seq = seq // self.tm
        self.n_lat_tiles = self.rows_lat // self.tm
        self.n_tiles = self.rows // self.tm
        self.tm_mm = _largest_divisor(self.rows, 1088, 16)
        self.n_mm_tiles = self.rows // self.tm_mm

    def group(self, i):
        return jnp.where(i < self.n_lat_tiles, i // self.tiles_per_seq, self.b)

    def per_row(self, tile, tile_rows, group_vals):
        row = tile * tile_rows + lax.broadcasted_iota(jnp.int32, (tile_rows, 1), 0)
        out = group_vals[self.b]
        for g in reversed(range(self.b)):
            out = jnp.where(row < (g + 1) * self.l, group_vals[g], out)
        return out


def _adaln(st, x, gains, mods, layer, j, m_shift, m_scale):
    d = x.shape[1]
    base = layer * (st.b + 1) * N_MOD

    def mod_map(m):
        return lambda i: (base + st.group(i) * N_MOD + m, 0, 0)

    return pl.pallas_call(
        _adaln_kernel,
        grid=(st.n_tiles,),
        in_specs=[pl.BlockSpec((st.tm, d), lambda i: (i, 0)),
                  pl.BlockSpec((1, 1, d), lambda i: (layer * 3 + j, 0, 0)),
                  pl.BlockSpec((1, 1, d), mod_map(m_shift)),
                  pl.BlockSpec((1, 1, d), mod_map(m_scale))],
        out_specs=pl.BlockSpec((st.tm, d), lambda i: (i, 0)),
        out_shape=jax.ShapeDtypeStruct(x.shape, BF16),
        compiler_params=_cparams(1),
        name="adaln",
    )(x, gains, mods, mods)


def _ffn_in_kernel(h_ref, wg_ref, wu_ref, o_ref, wg_s, wu_s):
    @pl.when(pl.program_id(1) == 0)
    def _():
        wg_s[...] = wg_ref[...].astype(BF16)
        wu_s[...] = wu_ref[...].astype(BF16)

    h = h_ref[...]
    g = jnp.dot(h, wg_s[...], preferred_element_type=F32)
    u = jnp.dot(h, wu_s[...], preferred_element_type=F32)
    o_ref[...] = (_silu(g) * u).astype(o_ref.dtype)


def _ffn_in(st, h, ffn_w_in, layer, j):
    d = h.shape[1]
    f = ffn_w_in.shape[3] // 2
    tn = _largest_divisor(f, 512, 128)
    nt = f // tn
    return pl.pallas_call(
        _ffn_in_kernel,
        grid=(nt, st.n_mm_tiles),
        in_specs=[pl.BlockSpec((st.tm_mm, d), lambda n, m: (m, 0)),
                  pl.BlockSpec((None, None, d, tn), lambda n, m: (layer, j, 0, n)),
                  pl.BlockSpec((None, None, d, tn), lambda n, m: (layer, j, 0, n + nt))],
        out_specs=pl.BlockSpec((st.tm_mm, tn), lambda n, m: (m, n)),
        out_shape=jax.ShapeDtypeStruct((st.rows, f), BF16),
        scratch_shapes=[pltpu.VMEM((d, tn), BF16), pltpu.VMEM((d, tn), BF16)],
        compiler_params=_cparams(2),
        name="ffn_in",
    )(h, ffn_w_in, ffn_w_in)


def _mm_res_kernel(a_ref, w_ref, x_ref, *rest, st, gate_scale):
    gate_refs, o_ref = rest[:-1], rest[-1]
    gate = st.per_row(pl.program_id(0), a_ref.shape[0], [g[0] for g in gate_refs])
    y = jnp.dot(a_ref[...], w_ref[...], preferred_element_type=F32)
    o_ref[...] = x_ref[...] + (gate_scale * gate) * y


def _mm_residual(st, a, w_bf, w_index, x, mods, layer, m_gate, gate_scale, tn_pref):
    k = a.shape[1]
    d = x.shape[1]
    tn = _largest_divisor(d, tn_pref, 128)
    base = layer * (st.b + 1) * N_MOD
    lead = (None,) * len(w_index)

    def gate_spec(g):
        return pl.BlockSpec((1, 1, tn), lambda m, n: (base + g * N_MOD + m_gate, 0, n))

    return pl.pallas_call(
        functools.partial(_mm_res_kernel, st=st, gate_scale=gate_scale),
        grid=(st.n_mm_tiles, d // tn),
        in_specs=[pl.BlockSpec((st.tm_mm, k), lambda m, n: (m, 0)),
                  pl.BlockSpec(lead + (k, tn), lambda m, n: tuple(w_index) + (0, n)),
                  pl.BlockSpec((st.tm_mm, tn), lambda m, n: (m, n))]
                 + [gate_spec(g) for g in range(st.b + 1)],
        out_specs=pl.BlockSpec((st.tm_mm, tn), lambda m, n: (m, n)),
        out_shape=jax.ShapeDtypeStruct(x.shape, F32),
        compiler_params=_cparams(2),
        name="matmul_residual",
    )(a, w_bf, x, *([mods] * (st.b + 1)))


def _mm_kernel(h_ref, w_ref, o_ref, w_s):
    @pl.when(pl.program_id(1) == 0)
    def _():
        w_s[...] = w_ref[...].astype(BF16)

    o_ref[...] = jnp.dot(h_ref[...], w_s[...], preferred_element_type=F32).astype(o_ref.dtype)


def _in_proj(st, h, w_in, layer):
    d = h.shape[1]
    n_in = w_in.shape[2]
    tn = _largest_divisor(n_in, 1024, 128)
    return pl.pallas_call(
        _mm_kernel,
        grid=(n_in // tn, st.n_mm_tiles),
        in_specs=[pl.BlockSpec((st.tm_mm, d), lambda n, m: (m, 0)),
                  pl.BlockSpec((None, d, tn), lambda n, m: (layer, 0, n))],
        out_specs=pl.BlockSpec((st.tm_mm, tn), lambda n, m: (m, n)),
        out_shape=jax.ShapeDtypeStruct((st.rows, n_in), BF16),
        scratch_shapes=[pltpu.VMEM((d, tn), BF16)],
        compiler_params=_cparams(2),
        name="in_proj",
    )(h, w_in)


def _merge_kernel(ya_ref, yb_ref, yc_ref, pa_ref, pb_ref, pc_ref, ba_ref, bb_ref, bc_ref, w_ref, o_ref):
    acc = None
    for k, (y_ref, p_ref, b_ref) in enumerate(
            ((ya_ref, pa_ref, ba_ref), (yb_ref, pb_ref, bb_ref), (yc_ref, pc_ref, bc_ref))):
        gate = jax.nn.sigmoid(p_ref[...].astype(F32) + b_ref[0])
        t = gate * jnp.dot(y_ref[...], w_ref[k], preferred_element_type=F32)
        acc = t if acc is None else acc + t
    o_ref[...] = acc.astype(o_ref.dtype)


def _merge(st, ya, yb, yc, p, gate_col0, b_gate3, w_up_bf, layer):
    w = ya.shape[1]
    d = w_up_bf.shape[3]
    tn = _largest_divisor(d, 256, 128)
    gb = gate_col0 // tn
    dt = d // tn
    tm = st.tm_mm
    y_spec = pl.BlockSpec((tm, w), lambda m, n: (m, 0))

    def p_spec(k):
        return pl.BlockSpec((tm, tn), lambda m, n: (m, gb + k * dt + n))

    def b_spec(k):
        return pl.BlockSpec((1, 1, tn), lambda m, n: (layer * N_BRANCH + k, 0, n))

    return pl.pallas_call(
        _merge_kernel,
        grid=(st.n_mm_tiles, dt),
        in_specs=[y_spec, y_spec, y_spec, p_spec(0), p_spec(1), p_spec(2),
                  b_spec(0), b_spec(1), b_spec(2),
                  pl.BlockSpec((None, N_BRANCH, w, tn), lambda m, n: (layer, 0, 0, n))],
        out_specs=pl.BlockSpec((tm, tn), lambda m, n: (m, n)),
        out_shape=jax.ShapeDtypeStruct((st.rows, d), BF16),
        compiler_params=_cparams(2),
        name="merge",
    )(ya, yb, yc, p, p, p, b_gate3, b_gate3, b_gate3, w_up_bf)


HALO = 16


def _short_conv_kernel(cur_ref, prev_ref, next_ref, w_ref, b_ref, o_ref, *, blocks_per_seq):
    j = pl.program_id(1) % blocks_per_seq
    u = cur_ref[...].astype(F32)
    t = u.shape[0]
    row = lax.broadcasted_iota(jnp.int32, u.shape, 0)
    prev_row = jnp.where(j == 0, 0.0, prev_ref[HALO - 1:HALO, :].astype(F32))
    next_row = jnp.where(j == blocks_per_seq - 1, 0.0, next_ref[0:1, :].astype(F32))
    up = jnp.where(row == 0, prev_row, pltpu.roll(u, 1, axis=0))
    un = jnp.where(row == t - 1, next_row, pltpu.roll(u, t - 1, axis=0))
    w = w_ref[...]
    o_ref[...] = (b_ref[...] + up * w[0:1] + u * w[1:2] + un * w[2:3]).astype(o_ref.dtype)


def _short_conv_call(p, conv_w, conv_b, width, row0, n_rows, seq_len, out_rows, prior):
    ts = _largest_divisor(seq_len, 512, HALO)
    ct = _largest_divisor(width, 1024, 128)
    blocks_per_seq = seq_len // ts
    rb0 = row0 // ts
    hb = ts // HALO
    n_hblocks = out_rows // HALO
    cpb = width // ct
    n_parts = HYENA_ORDER + 1

    def cur_map(c, r):
        return (rb0 + r, c)

    def prev_map(c, r):
        return (jnp.maximum((rb0 + r) * hb - 1, 0), c)

    def next_map(c, r):
        return (jnp.minimum((rb0 + r + 1) * hb, n_hblocks - 1), c)

    in_specs = [pl.BlockSpec((ts, ct), cur_map),
                pl.BlockSpec((HALO, ct), prev_map),
                pl.BlockSpec((HALO, ct), next_map),
                pl.BlockSpec((3, ct), lambda c, r: (0, c)),
                pl.BlockSpec((1, ct), lambda c, r: (0, c))]
    args = [p, p, p, conv_w, conv_b]
    aliases = {}
    kernel = functools.partial(_short_conv_kernel, blocks_per_seq=blocks_per_seq)
    if prior is not None:
        in_specs.append(pl.BlockSpec(memory_space=pl.ANY))
        args.append(prior)
        aliases = {5: 0}
        body = kernel
        kernel = lambda c, pv, nx, w, b, _prior, o: body(c, pv, nx, w, b, o)
    return pl.pallas_call(
        kernel,
        grid=(n_parts * cpb, n_rows // ts),
        in_specs=in_specs,
        out_specs=pl.BlockSpec((None, ts, ct), lambda c, r: (c // cpb, rb0 + r, c % cpb)),
        out_shape=jax.ShapeDtypeStruct((n_parts, out_rows, width), BF16),
        input_output_aliases=aliases,
        compiler_params=_cparams(2),
        name="short_conv",
    )(*args)


@functools.lru_cache(maxsize=None)
def _filter_positions(seq_len, emb_dim, pad_dim):
    bands = (emb_dim - 1) // 2
    j = np.arange(seq_len, dtype=np.float64)
    t = j / (seq_len - 1)
    wpos = 2.0 * np.pi * j / seq_len
    f = np.linspace(1e-4, bands - 1, bands)
    z = np.concatenate([t[:, None], np.cos(f[None] * wpos[:, None]), -np.sin(f[None] * wpos[:, None])], axis=1)
    lag = np.concatenate([np.arange(seq_len), [0], np.arange(seq_len - 1, 0, -1)])
    z2 = np.zeros((2 * seq_len, pad_dim), np.float64)
    z2[:, :emb_dim] = z[lag]
    return z2.astype(np.float32)


def _filter_mlp_kernel(z_ref, w1_ref, b1_ref, w2_ref, b2_ref, w3_ref, b3_ref, fr_ref, o_ref):
    fr = fr_ref[0]
    h = jnp.dot(z_ref[...], w1_ref[0], precision=HIGHEST, preferred_element_type=F32)
    h = jnp.sin(fr[0:1] * (h + b1_ref[0]))
    h = jnp.dot(h, w2_ref[0], precision=HIGHEST, preferred_element_type=F32)
    h = jnp.sin(fr[1:2] * (h + b2_ref[0]))
    h = jnp.dot(h, w3_ref[0], precision=HIGHEST, preferred_element_type=F32)
    o_ref[0] = jnp.sin(fr[2:3] * (h + b3_ref[0]))


def _filter_mlp(seq_len, w1, b1, w2, b2, w3, b3, freq):
    depth, emb, hid = w1.shape
    pad = -(-emb // 8) * 8
    n = 2 * seq_len
    z2 = jnp.asarray(_filter_positions(seq_len, emb, pad))
    w1p = jnp.pad(w1, ((0, 0), (0, pad - emb), (0, 0)))
    tr = _largest_divisor(n, 1024, 8)
    wspec = lambda k: pl.BlockSpec((1, k, hid), lambda l, r: (l, 0, 0))
    bspec = pl.BlockSpec((1, 1, hid), lambda l, r: (l, 0, 0))
    return pl.pallas_call(
        _filter_mlp_kernel,
        grid=(depth, n // tr),
        in_specs=[pl.BlockSpec((tr, pad), lambda l, r: (r, 0)),
                  wspec(pad), bspec, wspec(hid), bspec, wspec(hid), bspec,
                  pl.BlockSpec((1, 3, hid), lambda l, r: (l, 0, 0))],
        out_specs=pl.BlockSpec((1, tr, hid), lambda l, r: (l, r, 0)),
        out_shape=jax.ShapeDtypeStruct((depth, n, hid), F32),
        compiler_params=_cparams(2),
        name="filter_mlp",
    )(z2, w1p, b1.reshape(depth, 1, hid), w2, b2.reshape(depth, 1, hid), w3, b3.reshape(depth, 1, hid), freq)


def _filter_taps_kernel(h_ref, wf_ref, wb_ref, delta_ref, o_ref, *, seq_len):
    l = seq_len
    delta = delta_ref[...]
    lag = lax.broadcasted_iota(jnp.int32, (l, 1), 0)
    t_top = lag.astype(F32) / (l - 1.0)
    t_bot = jnp.where(lag == 0, 0, l - lag).astype(F32) / (l - 1.0)
    top = jnp.dot(h_ref[0, 0:l, :].astype(BF16), wf_ref[0].astype(BF16), preferred_element_type=F32)
    bot = jnp.dot(h_ref[0, l:2 * l, :].astype(BF16), wb_ref[0].astype(BF16), preferred_element_type=F32)
    top = top * jnp.exp(-t_top * delta)
    bot = bot * jnp.exp(-t_bot * delta)
    top = top + jnp.where(lag == 0, bot[0:1, :], 0.0)
    bot = jnp.where(lag == 0, 0.0, bot)
    ss = jnp.sum(top * top, axis=0, keepdims=True) + jnp.sum(bot * bot, axis=0, keepdims=True)
    scale = lax.rsqrt(ss + EPS)
    o_ref[0, 0:l, :] = (top * scale).astype(o_ref.dtype)
    o_ref[0, l:2 * l, :] = (bot * scale).astype(o_ref.dtype)


def _filter_taps(h_all, filt_w4, layer, seq_len, width):
    hid = h_all.shape[2]
    n = 2 * seq_len
    ct = 128
    cpb = width // ct
    min_decay = math.log(HYENA_TARGET) / HYENA_SLOW_DECAY
    max_decay = math.log(HYENA_TARGET) / HYENA_FAST_DECAY
    delta = jnp.asarray(np.abs(np.linspace(min_decay, max_decay, width)).astype(np.float32)).reshape(1, width)
    return pl.pallas_call(
        functools.partial(_filter_taps_kernel, seq_len=seq_len),
        grid=(HYENA_ORDER, cpb),
        in_specs=[pl.BlockSpec((1, n, hid), lambda o, c: (layer, 0, 0)),
                  pl.BlockSpec((1, hid, ct), lambda o, c: (layer, 0, (2 * o) * cpb + c)),
                  pl.BlockSpec((1, hid, ct), lambda o, c: (layer, 0, (2 * o + 1) * cpb + c)),
                  pl.BlockSpec((1, ct), lambda o, c: (0, c))],
        out_specs=pl.BlockSpec((1, n, ct), lambda o, c: (o, 0, c)),
        out_shape=jax.ShapeDtypeStruct((HYENA_ORDER, n, width), BF16),
        compiler_params=_cparams(2),
        name="filter_taps",
    )(h_all, filt_w4, filt_w4, delta)


def _real_form(e):
    return np.block([[e.real, -e.imag], [e.imag, e.real]])


@functools.lru_cache(maxsize=None)
def _dft_tables(n1, n2):
    n = n1 * n2
    j2 = np.arange(n2)[:, None, None]
    k1 = np.arange(n1)[None, :, None]
    j1 = np.arange(n1)[None, None, :]
    e = np.exp(-2j * np.pi * ((j1 * k1) / n1 + (j2 * k1) / n))
    eh = e[:, :, :n1 // 2]
    t_data = np.concatenate(
        [np.concatenate([eh.real, -eh.imag], axis=2), np.concatenate([eh.imag, eh.real], axis=2)], axis=1)
    t_filt = np.concatenate([e.real, e.imag], axis=1)
    hinv = np.conj(np.transpose(eh, (0, 2, 1))) / n
    t_inv = np.concatenate(
        [np.concatenate([hinv.real, -hinv.imag], axis=2), np.concatenate([hinv.imag, hinv.real], axis=2)], axis=1)
    f2 = np.exp(-2j * np.pi * np.outer(np.arange(n2), np.arange(n2)) / n2)
    as32 = lambda a: np.ascontiguousarray(a, dtype=np.float32)
    return as32(t_data), as32(t_filt), as32(t_inv), as32(_real_form(f2)), as32(_real_form(np.conj(f2)))


def _split_n(n):
    n2 = 1
    while n2 < 128 and n // (n2 * 2) >= 64:
        n2 *= 2
    return n // n2, n2


GS = 16
LANES = 128


def _swap_leading(x):
    return pltpu.einshape("abc->bac", x)


def _fft_s1_kernel(z_ref, t_ref, o_ref, s_ref):
    gs = z_ref.shape[1]
    s_ref[...] = _swap_leading(z_ref[...])
    for g in range(gs):
        o_ref[g] = jnp.dot(t_ref[g], s_ref[g], preferred_element_type=F32).astype(o_ref.dtype)


def _fft_mid_kernel(ar_ref, ai_ref, fr_ref, fi_ref, f2_ref, f2i_ref, o_ref, s_a, s_f):
    n2, gs, _ = ar_ref.shape
    s_a[:, 0:n2, :] = _swap_leading(ar_ref[...])
    s_a[:, n2:2 * n2, :] = _swap_leading(ai_ref[...])
    s_f[:, 0:n2, :] = _swap_leading(fr_ref[...])
    s_f[:, n2:2 * n2, :] = _swap_leading(fi_ref[...])
    f2 = f2_ref[...]
    f2i = f2i_ref[...]

    def spectra(g):
        return (jnp.dot(f2, s_a[g], preferred_element_type=F32),
                jnp.dot(f2, s_f[g], preferred_element_type=F32))

    nxt = spectra(0)
    for g in range(gs):
        u, k = nxt
        if g + 1 < gs:
            nxt = spectra(g + 1)
        ur, ui, kr, ki = u[:n2], u[n2:], k[:n2], k[n2:]
        y = jnp.concatenate([ur * kr - ui * ki, ur * ki + ui * kr], axis=0).astype(BF16)
        o_ref[g] = jnp.dot(f2i, y, preferred_element_type=F32).astype(o_ref.dtype)


def _fft_s3_kernel(cr_ref, ci_ref, t_ref, u_ref, x_ref, bias_ref, o_ref, s_c, s_y):
    n1, gs, _ = cr_ref.shape
    s_c[:, 0:n1, :] = _swap_leading(cr_ref[...])
    s_c[:, n1:2 * n1, :] = _swap_leading(ci_ref[...])
    for g in range(gs):
        s_y[g] = jnp.dot(t_ref[g], s_c[g], preferred_element_type=F32)
    y = _swap_leading(s_y[...])
    u = u_ref[...].astype(F32)
    o_ref[...] = (x_ref[...].astype(F32) * (y + u * bias_ref[...])).astype(o_ref.dtype)


def _hyena_small_kernel(u_ref, k_ref, x_ref, bias_ref, td_ref, tf_ref, ti_ref, _latent_rows_ref, o_ref, *, n):
    u = u_ref[...]
    a = jnp.dot(td_ref[...], u, preferred_element_type=F32)
    k = jnp.dot(tf_ref[...], k_ref[...], preferred_element_type=F32)
    ar, ai, kr, ki = a[:n], a[n:], k[:n], k[n:]
    y = jnp.concatenate([ar * kr - ai * ki, ar * ki + ai * kr], axis=0).astype(BF16)
    y = jnp.dot(ti_ref[...], y, preferred_element_type=F32)
    o_ref[...] = (x_ref[...].astype(F32) * (y + u.astype(F32) * bias_ref[...])).astype(o_ref.dtype)


def _bf16_table(a):
    return jnp.asarray(a).astype(BF16)


def _long_conv_gate(st, u3, u_part, x_arr, x_part, kk, kk_ctx, order, bias):
    assert st.b == 2, "the two batch elements are packed as one complex signal"
    _, rows, width = u3.shape
    l, lc = st.l, st.c
    n = 2 * l
    n1, n2 = _split_n(n)
    t_data, t_filt, t_inv, f2, f2i = (_bf16_table(a) for a in _dft_tables(n1, n2))
    assert n1 % GS == 0 and n2 % GS == 0 and width % LANES == 0
    bias2 = bias.reshape(1, width)
    slabs = rows // n2

    def stage1(z4, part, table, name):
        return pl.pallas_call(
            _fft_s1_kernel,
            grid=(n2 // GS,),
            in_specs=[pl.BlockSpec((None, n1, GS, width), lambda j: (part, 0, j, 0)),
                      pl.BlockSpec((GS, 2 * n1, n1), lambda j: (j, 0, 0))],
            out_specs=pl.BlockSpec((GS, 2 * n1, width), lambda j: (j, 0, 0)),
            out_shape=jax.ShapeDtypeStruct((n2, 2 * n1, width), BF16),
            scratch_shapes=[pltpu.VMEM((GS, n1, width), BF16)],
            compiler_params=_cparams(1),
            name=name,
        )(z4, table)

    a_filt = stage1(kk.reshape(HYENA_ORDER, n1, n2, width), order, t_filt, "fft_stage1_filter")
    u_v = u3.reshape(u3.shape[0], slabs, n2, width)
    a_data = stage1(u_v, u_part, t_data, "fft_stage1_data")

    ctm = _largest_divisor(width, 256, LANES)
    kb_im = n1 // GS
    re_spec = pl.BlockSpec((n2, GS, ctm), lambda c, k: (0, k, c))
    im_spec = pl.BlockSpec((n2, GS, ctm), lambda c, k: (0, kb_im + k, c))
    mat_spec = pl.BlockSpec((2 * n2, 2 * n2), lambda c, k: (0, 0))
    c_mid = pl.pallas_call(
        _fft_mid_kernel,
        grid=(width // ctm, n1 // GS),
        in_specs=[re_spec, im_spec, re_spec, im_spec, mat_spec, mat_spec],
        out_specs=pl.BlockSpec((GS, 2 * n2, ctm), lambda c, k: (k, 0, c)),
        out_shape=jax.ShapeDtypeStruct((n1, 2 * n2, width), BF16),
        scratch_shapes=[pltpu.VMEM((GS, 2 * n2, ctm), BF16)] * 2,
        compiler_params=_cparams(2),
        name="fft_mid",
    )(a_data, a_data, a_filt, a_filt, f2, f2i)

    cts = _largest_divisor(width, 1024, LANES)
    jb_im = n2 // GS
    x_v = x_arr.reshape(x_arr.shape[0], slabs, n2, width)
    out = pl.pallas_call(
        _fft_s3_kernel,
        grid=(width // cts, n2 // GS),
        in_specs=[pl.BlockSpec((n1, GS, cts), lambda c, j: (0, j, c)),
                  pl.BlockSpec((n1, GS, cts), lambda c, j: (0, jb_im + j, c)),
                  pl.BlockSpec((GS, n1, 2 * n1), lambda c, j: (j, 0, 0)),
                  pl.BlockSpec((None, n1, GS, cts), lambda c, j: (u_part, 0, j, c)),
                  pl.BlockSpec((None, n1, GS, cts), lambda c, j: (x_part, 0, j, c)),
                  pl.BlockSpec((1, cts), lambda c, j: (0, c))],
        out_specs=pl.BlockSpec((n1, GS, cts), lambda c, j: (0, j, c)),
        out_shape=jax.ShapeDtypeStruct((slabs, n2, width), BF16),
        scratch_shapes=[pltpu.VMEM((GS, 2 * n1, cts), BF16), pltpu.VMEM((GS, n1, cts), F32)],
        compiler_params=_cparams(2),
        name="fft_stage3",
    )(c_mid, c_mid, t_inv, u_v, x_v, bias2)
    out = out.reshape(rows, width)

    nc = 2 * lc
    tc_data, tc_filt, tc_inv, _, _ = (_bf16_table(a) for a in _dft_tables(nc, 1))
    ct = _largest_divisor(width, 512, 128)
    rb = (st.rows_lat) // (2 * lc)
    out = pl.pallas_call(
        functools.partial(_hyena_small_kernel, n=nc),
        grid=(width // ct,),
        in_specs=[pl.BlockSpec((None, 2 * lc, ct), lambda c: (u_part, rb, c)),
                  pl.BlockSpec((None, nc, ct), lambda c: (order, 0, c)),
                  pl.BlockSpec((None, 2 * lc, ct), lambda c: (x_part, rb, c)),
                  pl.BlockSpec((1, ct), lambda c: (0, c)),
                  pl.BlockSpec((2 * nc, nc), lambda c: (0, 0)),
                  pl.BlockSpec((2 * nc, nc), lambda c: (0, 0)),
                  pl.BlockSpec((nc, 2 * nc), lambda c: (0, 0)),
                  pl.BlockSpec(memory_space=pl.ANY)],
        out_specs=pl.BlockSpec((2 * lc, ct), lambda c: (rb, c)),
        out_shape=jax.ShapeDtypeStruct((rows, width), BF16),
        input_output_aliases={7: 0},
        compiler_params=_cparams(1),
        name="hyena_context",
    )(u3, kk_ctx, x_arr, bias2, tc_data[0], tc_filt[0], tc_inv[0], out)
    return out


@functools.lru_cache(maxsize=None)
def _pool_tables(t, windows):
    out = np.zeros((4, len(windows), t, 3 * t), np.float64)
    for v in range(4):
        lo_bound = t if v & 1 else 0
        hi_bound = 2 * t - 1 if v & 2 else 3 * t - 1
        for g, w in enumerate(windows):
            for r in range(t):
                pos = t + r
                lo = max(pos - w // 2, lo_bound)
                hi = min(pos + w - 1 - w // 2, hi_bound)
                out[v, g, r, lo:hi + 1] = 1.0 / (hi - lo + 1)
                out[v, g, r, pos] -= 1.0
    return out.astype(np.float32)


def _pool_kernel(prev_ref, cur_ref, next_ref, m_ref, w_ref, s_ref, *rest, n_groups, gw):
    o_ref = rest[-1]
    for g in range(n_groups):
        sl = slice(g * gw, (g + 1) * gw)
        u = jnp.concatenate([prev_ref[:, sl], cur_ref[:, sl], next_ref[:, sl]], axis=0)
        d = jnp.dot(m_ref[0, g], u, preferred_element_type=F32)
        y = jnp.dot(d.astype(BF16), w_ref[g], preferred_element_type=F32)
        o_ref[:, sl] = (y * s_ref[:, sl]).astype(o_ref.dtype)


def _pool_call(p, col_block, w_pool_bf, pool_scale3, layer, width, row0, n_seq, seq_len, out_rows, prior):
    t = _largest_divisor(seq_len, 256, 16)
    nblk = seq_len // t
    rb0 = row0 // t
    n_groups = len(POOL_WINDOWS)
    gw = width // n_groups
    tables = _bf16_table(_pool_tables(t, POOL_WINDOWS))

    def blk(delta):
        return lambda s, j: (rb0 + s * nblk + jnp.clip(j + delta, 0, nblk - 1), col_block)

    def variant(s, j):
        return ((j == 0).astype(jnp.int32) + 2 * (j == nblk - 1).astype(jnp.int32), 0, 0, 0)

    in_specs = [pl.BlockSpec((t, width), blk(-1)),
                pl.BlockSpec((t, width), blk(0)),
                pl.BlockSpec((t, width), blk(1)),
                pl.BlockSpec((1, n_groups, t, 3 * t), variant),
                pl.BlockSpec((None, n_groups, gw, gw), lambda s, j: (layer, 0, 0, 0)),
                pl.BlockSpec((None, 1, width), lambda s, j: (layer, 0, 0))]
    args = [p, p, p, tables, w_pool_bf, pool_scale3]
    aliases = {}
    if prior is not None:
        in_specs.append(pl.BlockSpec(memory_space=pl.ANY))
        args.append(prior)
        aliases = {6: 0}
    return pl.pallas_call(
        functools.partial(_pool_kernel, n_groups=n_groups, gw=gw),
        grid=(n_seq, nblk),
        in_specs=in_specs,
        out_specs=pl.BlockSpec((t, width), lambda s, j: (rb0 + s * nblk + j, 0)),
        out_shape=jax.ShapeDtypeStruct((out_rows, width), BF16),
        input_output_aliases=aliases,
        compiler_params=_cparams(2),
        name="pool",
    )(*args)


@functools.lru_cache(maxsize=None)
def _rope_tables(seq_len, grid_w, hd, pad_rows):
    axis = hd // 2
    pos = np.arange(seq_len)
    inv = ROPE_THETA ** (-np.arange(0, axis, 2, dtype=np.float64) / axis)
    ang_r = (pos // grid_w)[:, None] * inv[None]
    ang_c = (pos % grid_w)[:, None] * inv[None]
    cos = np.concatenate([np.cos(ang_r), np.cos(ang_r), np.cos(ang_c), np.cos(ang_c)], axis=1)
    sin = np.concatenate([-np.sin(ang_r), np.sin(ang_r), -np.sin(ang_c), np.sin(ang_c)], axis=1)
    cos = np.concatenate([cos, np.ones((pad_rows, hd))], axis=0)
    sin = np.concatenate([sin, np.zeros((pad_rows, hd))], axis=0)
    return cos.astype(np.float32), sin.astype(np.float32)


def _qk_prep_kernel(q_ref, k_ref, v_ref, qg_ref, kg_ref, cos_ref, sin_ref, qto_ref, ko_ref, vto_ref,
                    *, hd, q_scale):
    cos = cos_ref[...]
    sin = sin_ref[...]
    lane = lax.broadcasted_iota(jnp.int32, cos.shape, 1)
    first = (lane % (hd // 2)) < (hd // 4)
    ones = jnp.ones((hd, hd), BF16)

    def normed_rotated(x_ref, h, g_ref, mul):
        x = x_ref[:, h * hd:(h + 1) * hd].astype(F32)
        ss = jnp.dot((x * x).astype(BF16), ones, preferred_element_type=F32)
        y = x * lax.rsqrt(ss * (1.0 / hd) + EPS) * g_ref[...]
        partner = jnp.where(first, pltpu.roll(y, hd - hd // 4, axis=1), pltpu.roll(y, hd // 4, axis=1))
        return (y * cos + partner * sin) * mul

    for h in range(q_ref.shape[1] // hd):
        qto_ref[h * hd:(h + 1) * hd, :] = normed_rotated(q_ref, h, qg_ref, q_scale).T.astype(qto_ref.dtype)
    for h in range(k_ref.shape[1] // hd):
        sl = slice(h * hd, (h + 1) * hd)
        ko_ref[:, sl] = normed_rotated(k_ref, h, kg_ref, 1.0).astype(ko_ref.dtype)
        vto_ref[sl, :] = v_ref[:, sl].astype(F32).T.astype(vto_ref.dtype)


def _qk_prep(st, p, q_col0, k_col0, aw, kvw, q_gain, k_gain, layer, hd):
    cos_np, sin_np = _rope_tables(st.l, GRID_W, hd, st.tm)
    tab_map = lambda i: (jnp.where(i < st.n_lat_tiles, i % st.tiles_per_seq, st.tiles_per_seq), 0)
    return pl.pallas_call(
        functools.partial(_qk_prep_kernel, hd=hd, q_scale=hd ** -0.5),
        grid=(st.n_tiles,),
        in_specs=[pl.BlockSpec((st.tm, aw), lambda i: (i, q_col0 // aw)),
                  pl.BlockSpec((st.tm, kvw), lambda i: (i, k_col0 // kvw)),
                  pl.BlockSpec((st.tm, kvw), lambda i: (i, k_col0 // kvw + 1)),
                  pl.BlockSpec((None, 1, hd), lambda i: (layer, 0, 0)),
                  pl.BlockSpec((None, 1, hd), lambda i: (layer, 0, 0)),
                  pl.BlockSpec((st.tm, hd), tab_map),
                  pl.BlockSpec((st.tm, hd), tab_map)],
        out_specs=[pl.BlockSpec((aw, st.tm), lambda i: (0, i)),
                   pl.BlockSpec((st.tm, kvw), lambda i: (i, 0)),
                   pl.BlockSpec((kvw, st.tm), lambda i: (0, i))],
        out_shape=[jax.ShapeDtypeStruct((aw, st.rows), BF16),
                   jax.ShapeDtypeStruct((st.rows, kvw), BF16),
                   jax.ShapeDtypeStruct((kvw, st.rows), BF16)],
        compiler_params=_cparams(1),
        name="qk_prep",
    )(p, p, p, q_gain.reshape(-1, 1, hd), k_gain.reshape(-1, 1, hd), jnp.asarray(cos_np), jnp.asarray(sin_np))


ONES_ROWS = 16


def _attn_kernel(*refs, group, hd, tk, n_chunks):
    qt_ref, kc_ref, vtc_ref = refs[:3]
    o_ref, m_s, acc_s, s_a, s_b = refs[-5:]
    tq = qt_ref.shape[1]
    qs = jnp.concatenate([qt_ref[g * hd:(g + 1) * hd, :] for g in range(group)], axis=1)
    m_s[...] = jnp.full(m_s.shape, -1e30, F32)
    acc_s[...] = jnp.zeros(acc_s.shape, F32)

    def scores(k):
        return jnp.dot(k, qs, preferred_element_type=F32)

    def softmax_pv(s, vt):
        vt1 = jnp.concatenate([vt, jnp.ones((ONES_ROWS, vt.shape[1]), BF16)], axis=0)
        m_prev = m_s[...]
        m_new = jnp.maximum(m_prev, jnp.max(s, axis=0, keepdims=True))
        alpha = jnp.exp(m_prev - m_new)
        p = jnp.exp((s - m_new).astype(BF16))
        acc_s[...] = alpha * acc_s[...] + jnp.dot(vt1, p, preferred_element_type=F32)
        m_s[...] = m_new

    s_ctx = scores(kc_ref[...])
    if not n_chunks:
        softmax_pv(s_ctx, vtc_ref[...])
    else:
        kl_ref, vtl_ref = refs[3:5]

        def keys(c):
            return kl_ref[pl.ds(pl.multiple_of(c * tk, tk), tk), :]

        def vals(c):
            return vtl_ref[:, pl.ds(pl.multiple_of(c * tk, tk), tk)]

        first = n_chunks % 2
        if n_chunks - first:
            s_a[...] = scores(keys(first))
        softmax_pv(s_ctx, vtc_ref[...])
        if first:
            softmax_pv(scores(keys(0)), vals(0))
        if n_chunks - first:

            def body(j, carry):
                c = first + 2 * j
                s_b[...] = scores(keys(c + 1))
                softmax_pv(s_a[...], vals(c))
                s_a[...] = scores(keys(jnp.minimum(c + 2, n_chunks - 1)))
                softmax_pv(s_b[...], vals(c + 1))
                return carry

            lax.fori_loop(0, (n_chunks - first) // 2, body, 0)
    o = acc_s[0:hd, :] / acc_s[hd:hd + 1, :]
    for g in range(group):
        o_ref[:, g * hd:(g + 1) * hd] = o[:, g * tq:(g + 1) * tq].T.astype(o_ref.dtype)


def _attention(st, q_t, k, v_t, hd):
    group = N_HEADS // N_KV_HEADS
    gw = group * hd
    aw = N_HEADS * hd
    tq = _largest_divisor(st.l, 256, LANES)
    tk = _largest_divisor(st.l, 512, LANES)
    qt = st.l // tq
    cb0 = st.rows_lat // st.c

    def scratch(q_rows):
        nq = group * q_rows
        return [pltpu.VMEM((1, nq), F32), pltpu.VMEM((hd + ONES_ROWS, nq), F32),
                pltpu.VMEM((tk, nq), F32), pltpu.VMEM((tk, nq), F32)]

    yc = pl.pallas_call(
        functools.partial(_attn_kernel, group=group, hd=hd, tk=tk, n_chunks=st.l // tk),
        grid=(st.b, N_KV_HEADS, qt),
        in_specs=[pl.BlockSpec((gw, tq), lambda b, h, t: (h, b * qt + t)),
                  pl.BlockSpec((st.c, hd), lambda b, h, t: (cb0 + b, h)),
                  pl.BlockSpec((hd, st.c), lambda b, h, t: (h, cb0 + b)),
                  pl.BlockSpec((st.l, hd), lambda b, h, t: (b, h)),
                  pl.BlockSpec((hd, st.l), lambda b, h, t: (h, b))],
        out_specs=pl.BlockSpec((tq, gw), lambda b, h, t: (b * qt + t, h)),
        out_shape=jax.ShapeDtypeStruct((st.rows, aw), BF16),
        scratch_shapes=scratch(tq),
        compiler_params=_cparams(3),
        name="attention_latent",
    )(q_t, k, v_t, k, v_t)
    return pl.pallas_call(
        functools.partial(_attn_kernel, group=group, hd=hd, tk=tk, n_chunks=0),
        grid=(st.b, N_KV_HEADS),
        in_specs=[pl.BlockSpec((gw, st.c), lambda b, h: (h, cb0 + b)),
                  pl.BlockSpec((st.c, hd), lambda b, h: (cb0 + b, h)),
                  pl.BlockSpec((hd, st.c), lambda b, h: (h, cb0 + b)),
                  pl.BlockSpec(memory_space=pl.ANY)],
        out_specs=pl.BlockSpec((st.c, gw), lambda b, h: (cb0 + b, h)),
        out_shape=jax.ShapeDtypeStruct((st.rows, aw), BF16),
        input_output_aliases={3: 0},
        scratch_shapes=scratch(st.c),
        compiler_params=_cparams(2),
        name="attention_context",
    )(q_t, k, v_t, yc)


def kernel(x, c, ctx, c_ctx, w_mod, b_mod, norm_gain, final_gain, ffn_w_in, ffn_w_out, w_in, b_gate, conv_w, conv_b, filt_w1, filt_b1, filt_w2, filt_b2, filt_w3, filt_b3, filt_w4, filt_freq, hyena_bias, w_pool, pool_scale, q_gain, k_gain, w_up, w_out):
    b, l, d = x.shape
    lc = ctx.shape[1]
    depth = w_mod.shape[0]
    width = hyena_bias.shape[2]
    hd = q_gain.shape[1]
    aw, kvw = N_HEADS * hd, N_KV_HEADS * hd
    a1 = (HYENA_ORDER + 1) * width
    q0 = a1 + width
    k0 = q0 + aw
    v0 = k0 + kvw
    g0 = v0 + kvw
    st = _Stream(b, l, lc)

    xs = jnp.concatenate([x.reshape(b * l, d), ctx.reshape(b * lc, d)], axis=0)
    cvec = jnp.zeros((8, d), F32).at[:b].set(c).at[b].set(c_ctx)
    mods = _modulation(cvec, w_mod, b_mod).reshape(depth, 8, N_MOD, d)[:, :b + 1]
    mods = mods.reshape(depth * (b + 1) * N_MOD, 1, d)
    gains = norm_gain.reshape(depth * 3, 1, d)
    b_gate3 = b_gate.reshape(depth * N_BRANCH, 1, d)
    pool_scale3 = pool_scale.reshape(depth, 1, width)
    w_pool_bf = w_pool.astype(BF16)
    ffn_w_out_bf = ffn_w_out.astype(BF16)
    w_up_bf = w_up.astype(BF16)
    w_out_bf = w_out.astype(BF16)
    h_lat =_filter_mlp(l, filt_w1, filt_b1, filt_w2, filt_b2, filt_w3, filt_b3, filt_freq)
    h_ctx = _filter_mlp(lc, filt_w1, filt_b1, filt_w2, filt_b2, filt_w3, filt_b3, filt_freq)

    for i in range(depth):
        h = _adaln(st, xs, gains, mods, i, 0, 0, 1)
        a = _ffn_in(st, h, ffn_w_in, i, 0)
        xs = _mm_residual(st, a, ffn_w_out_bf, (i, 0), xs, mods, i, 2, 0.5, 512)
        h = _adaln(st, xs, gains, mods, i, 1, 3, 4)
        p = _in_proj(st, h, w_in, i)
        cw, cb = conv_w[i], conv_b[i].reshape(1, -1)
        u3 = _short_conv_call(p, cw, cb, width, 0, st.rows_lat, l, st.rows, None)
        u3 = _short_conv_call(p, cw, cb, width, st.rows_lat, b * lc, lc, st.rows, u3)
        kk = _filter_taps(h_lat, filt_w4, i, l, width)
        kk_ctx = _filter_taps(h_ctx, filt_w4, i, lc, width)
        z = _long_conv_gate(st, u3, 0, u3, 1, kk, kk_ctx, 0, hyena_bias[i, 0])
        ya = _long_conv_gate(st, z[None], 0, u3, 2, kk, kk_ctx, 1, hyena_bias[i, 1])
        yb = _pool_call(p, a1 // width, w_pool_bf, pool_scale3, i, width, 0, b, l, st.rows, None)
        yb = _pool_call(p, a1 // width, w_pool_bf, pool_scale3, i, width, st.rows_lat, b, lc, st.rows, yb)
        q_t, k, v_t = _qk_prep(st, p, q0, k0, aw, kvw, q_gain, k_gain, i, hd)
        yc = _attention(st, q_t, k, v_t, hd)
        merged = _merge(st, ya, yb, yc, p, g0, b_gate3, w_up_bf, i)
        xs = _mm_residual(st, merged, w_out_bf, (i,), xs, mods, i, 5, 1.0, 512)
        h = _adaln(st, xs, gains, mods, i, 2, 6, 7)
        a = _ffn_in(st, h, ffn_w_in, i, 1)
        xs = _mm_residual(st, a, ffn_w_out_bf, (i, 1), xs, mods, i, 8, 0.5, 512)

    out = pl.pallas_call(
        _rmsnorm_kernel,
        grid=(st.n_lat_tiles,),
        in_specs=[pl.BlockSpec((st.tm, d), lambda i: (i, 0)),
                  pl.BlockSpec((1, d), lambda i: (0, 0))],
        out_specs=pl.BlockSpec((st.tm, d), lambda i: (i, 0)),
        out_shape=jax.ShapeDtypeStruct((b * l, d), F32),
        compiler_params=_cparams(1),
        name="final_rmsnorm",
    )(xs, final_gain.reshape(1, d))
    return out.reshape(b, l, d)
```

```python
import functools
import math

import numpy as np
import jax
import jax.numpy as jnp
from jax import lax
from jax.experimental import pallas as pl
from jax.experimental.pallas import tpu as pltpu

F32 = jnp.float32
BF16 = jnp.bfloat16

N_HEADS = 16
N_KV_HEADS = 4
GRID_W = 64
ROPE_THETA = 10000.0
HYENA_ORDER = 2
HYENA_TARGET = 1e-2
HYENA_FAST_DECAY = 0.3
HYENA_SLOW_DECAY = 1.5
POOL_WINDOWS = (2, 4, 8, 16)
N_MOD = 9
N_BRANCH = 3
EPS = 1e-6
HIGHEST = lax.Precision.HIGHEST

VMEM_LIMIT_V7X = 56 * 1024 * 1024


def _cparams(n_axes):
    return pltpu.CompilerParams(
        dimension_semantics=("arbitrary",) * n_axes, vmem_limit_bytes=VMEM_LIMIT_V7X)


def _largest_divisor(total, pref, align):
    if total <= pref:
        return total
    t = (pref // align) * align
    while t > align and total % t:
        t -= align
    assert total % t == 0, (total, pref, align)
    return t


def _silu(v):
    return v * jax.nn.sigmoid(v)


def _mod_kernel(c_ref, w_ref, b_ref, o_ref):
    s = _silu(c_ref[...]).astype(BF16)
    w = w_ref[0].astype(BF16)
    o_ref[0] = jnp.dot(s, w, preferred_element_type=F32) + b_ref[0]


def _modulation(cvec, w_mod, b_mod):
    depth, d, nm = w_mod.shape
    tn = _largest_divisor(nm, 1024, 128)
    return pl.pallas_call(
        _mod_kernel,
        grid=(depth, nm // tn),
        in_specs=[pl.BlockSpec((8, d), lambda l, n: (0, 0)),
                  pl.BlockSpec((1, d, tn), lambda l, n: (l, 0, n)),
                  pl.BlockSpec((1, 1, tn), lambda l, n: (l, 0, n))],
        out_specs=pl.BlockSpec((1, 8, tn), lambda l, n: (l, 0, n)),
        out_shape=jax.ShapeDtypeStruct((depth, 8, nm), F32),
        compiler_params=_cparams(2),
        name="modulation",
    )(cvec, w_mod, b_mod.reshape(depth, 1, nm))


def _adaln_kernel(x_ref, g_ref, sh_ref, sc_ref, o_ref):
    x = x_ref[...]
    y = x * lax.rsqrt(jnp.mean(x * x, axis=-1, keepdims=True) + EPS)
    y = y * g_ref[0]
    o_ref[...] = (y * (1.0 + sc_ref[0]) + sh_ref[0]).astype(o_ref.dtype)


def _rmsnorm_kernel(x_ref, g_ref, o_ref):
    x = x_ref[...]
    y = x * lax.rsqrt(jnp.mean(x * x, axis=-1, keepdims=True) + EPS)
    o_ref[...] = y * g_ref[...]


class _Stream:
    def __init__(self, batch, seq, ctx_len):
        self.b, self.l, self.c = batch, seq, ctx_len
        self.rows_lat = batch * seq
        self.rows = batch * (seq + ctx_len)
        self.tm = _largest_divisor(math.gcd(seq, batch * ctx_len), 512, 16)
        self.tiles_per_seq = seq // self.tm
        self.n_lat_tiles = self.rows_lat // self.tm
        self.n_tiles = self.rows // self.tm
        self.tm_mm = _largest_divisor(self.rows, 1088, 16)
        self.n_mm_tiles = self.rows // self.tm_mm

    def group(self, i):
        return jnp.where(i < self.n_lat_tiles, i // self.tiles_per_seq, self.b)

    def per_row(self, tile, tile_rows, group_vals):
        row = tile * tile_rows + lax.broadcasted_iota(jnp.int32, (tile_rows, 1), 0)
        out = group_vals[self.b]
        for g in reversed(range(self.b)):
            out = jnp.where(row < (g + 1) * self.l, group_vals[g], out)
        return out


def _adaln(st, x, gains, mods, layer, j, m_shift, m_scale):
    d = x.shape[1]
    base = layer * (st.b + 1) * N_MOD

    def mod_map(m):
        return lambda i: (base + st.group(i) * N_MOD + m, 0, 0)

    return pl.pallas_call(
        _adaln_kernel,
        grid=(st.n_tiles,),
        in_specs=[pl.BlockSpec((st.tm, d), lambda i: (i, 0)),
                  pl.BlockSpec((1, 1, d), lambda i: (layer * 3 + j, 0, 0)),
                  pl.BlockSpec((1, 1, d), mod_map(m_shift)),
                  pl.BlockSpec((1, 1, d), mod_map(m_scale))],
        out_specs=pl.BlockSpec((st.tm, d), lambda i: (i, 0)),
        out_shape=jax.ShapeDtypeStruct(x.shape, BF16),
        compiler_params=_cparams(1),
        name="adaln",
    )(x, gains, mods, mods)


def _ffn_in_kernel(h_ref, wg_ref, wu_ref, wo_ref, o_ref, wo_bf_ref, wg_s, wu_s):
    @pl.when(pl.program_id(1) == 0)
    def _():
        wg_s[...] = wg_ref[...].astype(BF16)
        wu_s[...] = wu_ref[...].astype(BF16)
        wo_bf_ref[...] = wo_ref[...].astype(BF16)

    h = h_ref[...]
    g = jnp.dot(h, wg_s[...], preferred_element_type=F32)
    u = jnp.dot(h, wu_s[...], preferred_element_type=F32)
    o_ref[...] = (_silu(g) * u).astype(o_ref.dtype)


def _ffn_in(st, h, ffn_w_in, ffn_w_out, layer, j):
    d = h.shape[1]
    f = ffn_w_in.shape[3] // 2
    d_out = ffn_w_out.shape[3]
    tn = _largest_divisor(f, 512, 128)
    nt = f // tn
    return pl.pallas_call(
        _ffn_in_kernel,
        grid=(nt, st.n_mm_tiles),
        in_specs=[pl.BlockSpec((st.tm_mm, d), lambda n, m: (m, 0)),
                  pl.BlockSpec((None, None, d, tn), lambda n, m: (layer, j, 0, n)),
                  pl.BlockSpec((None, None, d, tn), lambda n, m: (layer, j, 0, n + nt)),
                  pl.BlockSpec((None, None, tn, d_out), lambda n, m: (layer, j, n, 0))],
        out_specs=[pl.BlockSpec((st.tm_mm, tn), lambda n, m: (m, n)),
                   pl.BlockSpec((tn, d_out), lambda n, m: (n, 0))],
        out_shape=[jax.ShapeDtypeStruct((st.rows, f), BF16),
                   jax.ShapeDtypeStruct((f, d_out), BF16)],
        scratch_shapes=[pltpu.VMEM((d, tn), BF16), pltpu.VMEM((d, tn), BF16)],
        compiler_params=_cparams(2),
        name="ffn_in",
    )(h, ffn_w_in, ffn_w_in, ffn_w_out)


def _mm_res_kernel(a_ref, w_ref, x_ref, *rest, st, gate_scale):
    gate_refs, o_ref = rest[:-1], rest[-1]
    gate = st.per_row(pl.program_id(0), a_ref.shape[0], [g[0] for g in gate_refs])
    y = jnp.dot(a_ref[...], w_ref[...], preferred_element_type=F32)
    o_ref[...] = x_ref[...] + (gate_scale * gate) * y


def _mm_residual(st, a, w_bf, w_index, x, mods, layer, m_gate, gate_scale, tn_pref):
    k = a.shape[1]
    d = x.shape[1]
    tn = _largest_divisor(d, tn_pref, 128)
    base = layer * (st.b + 1) * N_MOD
    lead = (None,) * len(w_index)

    def gate_spec(g):
        return pl.BlockSpec((1, 1, tn), lambda m, n: (base + g * N_MOD + m_gate, 0, n))

    return pl.pallas_call(
        functools.partial(_mm_res_kernel, st=st, gate_scale=gate_scale),
        grid=(st.n_mm_tiles, d // tn),
        in_specs=[pl.BlockSpec((st.tm_mm, k), lambda m, n: (m, 0)),
                  pl.BlockSpec(lead + (k, tn), lambda m, n: tuple(w_index) + (0, n)),
                  pl.BlockSpec((st.tm_mm, tn), lambda m, n: (m, n))]
                 + [gate_spec(g) for g in range(st.b + 1)],
        out_specs=pl.BlockSpec((st.tm_mm, tn), lambda m, n: (m, n)),
        out_shape=jax.ShapeDtypeStruct(x.shape, F32),
        compiler_params=_cparams(2),
        name="matmul_residual",
    )(a, w_bf, x, *([mods] * (st.b + 1)))


def _mm_kernel(h_ref, w_ref, o_ref, w_s):
    @pl.when(pl.program_id(1) == 0)
    def _():
        w_s[...] = w_ref[...].astype(BF16)

    o_ref[...] = jnp.dot(h_ref[...], w_s[...], preferred_element_type=F32).astype(o_ref.dtype)


def _in_proj(st, h, w_in, layer):
    d = h.shape[1]
    n_in = w_in.shape[2]
    tn = _largest_divisor(n_in, 1024, 128)
    return pl.pallas_call(
        _mm_kernel,
        grid=(n_in // tn, st.n_mm_tiles),
        in_specs=[pl.BlockSpec((st.tm_mm, d), lambda n, m: (m, 0)),
                  pl.BlockSpec((None, d, tn), lambda n, m: (layer, 0, n))],
        out_specs=pl.BlockSpec((st.tm_mm, tn), lambda n, m: (m, n)),
        out_shape=jax.ShapeDtypeStruct((st.rows, n_in), BF16),
        scratch_shapes=[pltpu.VMEM((d, tn), BF16)],
        compiler_params=_cparams(2),
        name="in_proj",
    )(h, w_in)


def _merge_kernel(ya_ref, yb_ref, yc_ref, pa_ref, pb_ref, pc_ref, ba_ref, bb_ref, bc_ref, w_ref, o_ref):
    acc = None
    for k, (y_ref, p_ref, b_ref) in enumerate(
            ((ya_ref, pa_ref, ba_ref), (yb_ref, pb_ref, bb_ref), (yc_ref, pc_ref, bc_ref))):
        gate = jax.nn.sigmoid(p_ref[...].astype(F32) + b_ref[0])
        t = gate * jnp.dot(y_ref[...], w_ref[k], preferred_element_type=F32)
        acc = t if acc is None else acc + t
    o_ref[...] = acc.astype(o_ref.dtype)


def _merge(st, ya, yb, yc, p, gate_col0, b_gate3, w_up_bf, layer):
    w = ya.shape[1]
    d = w_up_bf.shape[3]
    tn = _largest_divisor(d, 256, 128)
    gb = gate_col0 // tn
    dt = d // tn
    tm = st.tm_mm
    y_spec = pl.BlockSpec((tm, w), lambda m, n: (m, 0))

    def p_spec(k):
        return pl.BlockSpec((tm, tn), lambda m, n: (m, gb + k * dt + n))

    def b_spec(k):
        return pl.BlockSpec((1, 1, tn), lambda m, n: (layer * N_BRANCH + k, 0, n))

    return pl.pallas_call(
        _merge_kernel,
        grid=(st.n_mm_tiles, dt),
        in_specs=[y_spec, y_spec, y_spec, p_spec(0), p_spec(1), p_spec(2),
                  b_spec(0), b_spec(1), b_spec(2),
                  pl.BlockSpec((None, N_BRANCH, w, tn), lambda m, n: (layer, 0, 0, n))],
        out_specs=pl.BlockSpec((tm, tn), lambda m, n: (m, n)),
        out_shape=jax.ShapeDtypeStruct((st.rows, d), BF16),
        compiler_params=_cparams(2),
        name="merge",
    )(ya, yb, yc, p, p, p, b_gate3, b_gate3, b_gate3, w_up_bf)


HALO = 16


def _short_conv_kernel(cur_ref, prev_ref, next_ref, w_ref, b_ref, o_ref, *, blocks_per_seq):
    j = pl.program_id(1) % blocks_per_seq
    u = cur_ref[...].astype(F32)
    t = u.shape[0]
    prev_row = jnp.where(j == 0, 0.0, prev_ref[HALO - 1:HALO, :].astype(F32))
    next_row = jnp.where(j == blocks_per_seq - 1, 0.0, next_ref[0:1, :].astype(F32))
    row = lax.broadcasted_iota(jnp.int32, u.shape, 0)
    up = jnp.where(row == 0, prev_row, pltpu.roll(u, 1, axis=0))
    un = jnp.where(row == t - 1, next_row, pltpu.roll(u, t - 1, axis=0))
    w = w_ref[...]
    o_ref[...] = (b_ref[...] + up * w[0:1] + u * w[1:2] + un * w[2:3]).astype(o_ref.dtype)


def _short_conv_call(p, conv_w, conv_b, width, row0, n_rows, seq_len, out_rows, prior):
    ts = _largest_divisor(seq_len, 512, HALO)
    ct = _largest_divisor(width, 1024, 128)
    blocks_per_seq = seq_len // ts
    rb0 = row0 // ts
    hb = ts // HALO
    n_hblocks = out_rows // HALO
    cpb = width // ct
    n_parts = HYENA_ORDER + 1

    def cur_map(c, r):
        return (rb0 + r, c)

    def prev_map(c, r):
        return (jnp.maximum((rb0 + r) * hb - 1, 0), c)

    def next_map(c, r):
        return (jnp.minimum((rb0 + r + 1) * hb, n_hblocks - 1), c)

    in_specs = [pl.BlockSpec((ts, ct), cur_map),
                pl.BlockSpec((HALO, ct), prev_map),
                pl.BlockSpec((HALO, ct), next_map),
                pl.BlockSpec((3, ct), lambda c, r: (0, c)),
                pl.BlockSpec((1, ct), lambda c, r: (0, c))]
    args = [p, p, p, conv_w, conv_b]
    aliases = {}
    kernel = functools.partial(_short_conv_kernel, blocks_per_seq=blocks_per_seq)
    if prior is not None:
        in_specs.append(pl.BlockSpec(memory_space=pl.ANY))
        args.append(prior)
        aliases = {5: 0}
        body = kernel
        kernel = lambda c, pv, nx, w, b, _prior, o: body(c, pv, nx, w, b, o)
    return pl.pallas_call(
        kernel,
        grid=(n_parts * cpb, n_rows // ts),
        in_specs=in_specs,
        out_specs=pl.BlockSpec((None, ts, ct), lambda c, r: (c // cpb, rb0 + r, c % cpb)),
        out_shape=jax.ShapeDtypeStruct((n_parts, out_rows, width), BF16),
        input_output_aliases=aliases,
        compiler_params=_cparams(2),
        name="short_conv",
    )(*args)


@functools.lru_cache(maxsize=None)
def _filter_positions(seq_len, emb_dim, pad_dim):
    bands = (emb_dim - 1) // 2
    j = np.arange(seq_len, dtype=np.float64)
    t = j / (seq_len - 1)
    wpos = 2.0 * np.pi * j / seq_len
    f = np.linspace(1e-4, bands - 1, bands)
    z = np.concatenate([t[:, None], np.cos(f[None] * wpos[:, None]), -np.sin(f[None] * wpos[:, None])], axis=1)
    lag = np.concatenate([np.arange(seq_len), [0], np.arange(seq_len - 1, 0, -1)])
    z2 = np.zeros((2 * seq_len, pad_dim), np.float64)
    z2[:, :emb_dim] = z[lag]
    return z2.astype(np.float32)


def _filter_mlp_kernel(z_ref, w1_ref, b1_ref, w2_ref, b2_ref, w3_ref, b3_ref, fr_ref, o_ref):
    fr = fr_ref[0]
    h = jnp.dot(z_ref[...], w1_ref[0], precision=HIGHEST, preferred_element_type=F32)
    h = jnp.sin(fr[0:1] * (h + b1_ref[0]))
    h = jnp.dot(h, w2_ref[0], precision=HIGHEST, preferred_element_type=F32)
    h = jnp.sin(fr[1:2] * (h + b2_ref[0]))
    h = jnp.dot(h, w3_ref[0], precision=HIGHEST, preferred_element_type=F32)
    o_ref[0] = jnp.sin(fr[2:3] * (h + b3_ref[0]))


def _filter_mlp(seq_len, w1, b1, w2, b2, w3, b3, freq):
    depth, emb, hid = w1.shape
    pad = -(-emb // 8) * 8
    n = 2 * seq_len
    z2 = jnp.asarray(_filter_positions(seq_len, emb, pad))
    w1p = jnp.pad(w1, ((0, 0), (0, pad - emb), (0, 0)))
    tr = _largest_divisor(n, 1024, 8)
    wspec = lambda k: pl.BlockSpec((1, k, hid), lambda l, r: (l, 0, 0))
    bspec = pl.BlockSpec((1, 1, hid), lambda l, r: (l, 0, 0))
    return pl.pallas_call(
        _filter_mlp_kernel,
        grid=(depth, n // tr),
        in_specs=[pl.BlockSpec((tr, pad), lambda l, r: (r, 0)),
                  wspec(pad), bspec, wspec(hid), bspec, wspec(hid), bspec,
                  pl.BlockSpec((1, 3, hid), lambda l, r: (l, 0, 0))],
        out_specs=pl.BlockSpec((1, tr, hid), lambda l, r: (l, r, 0)),
        out_shape=jax.ShapeDtypeStruct((depth, n, hid), F32),
        compiler_params=_cparams(2),
        name="filter_mlp",
    )(z2, w1p, b1.reshape(depth, 1, hid), w2, b2.reshape(depth, 1, hid), w3, b3.reshape(depth, 1, hid), freq)


def _filter_taps_kernel(h_ref, wf_ref, wb_ref, delta_ref, o_ref, *, seq_len):
    l = seq_len
    delta = delta_ref[...]
    lag = lax.broadcasted_iota(jnp.int32, (l, 1), 0)
    t_top = lag.astype(F32) / (l - 1.0)
    t_bot = jnp.where(lag == 0, 0, l - lag).astype(F32) / (l - 1.0)
    top = jnp.dot(h_ref[0, 0:l, :].astype(BF16), wf_ref[0].astype(BF16), preferred_element_type=F32)
    bot = jnp.dot(h_ref[0, l:2 * l, :].astype(BF16), wb_ref[0].astype(BF16), preferred_element_type=F32)
    top = top * jnp.exp(-t_top * delta)
    bot = bot * jnp.exp(-t_bot * delta)
    top = top + jnp.where(lag == 0, bot[0:1, :], 0.0)
    bot = jnp.where(lag == 0, 0.0, bot)
    ss = jnp.sum(top * top, axis=0, keepdims=True) + jnp.sum(bot * bot, axis=0, keepdims=True)
    scale = lax.rsqrt(ss + EPS)
    o_ref[0, 0:l, :] = (top * scale).astype(o_ref.dtype)
    o_ref[0, l:2 * l, :] = (bot * scale).astype(o_ref.dtype)


def _filter_taps(h_all, filt_w4, layer, seq_len, width):
    hid = h_all.shape[2]
    n = 2 * seq_len
    ct = 128
    cpb = width // ct
    min_decay = math.log(HYENA_TARGET) / HYENA_SLOW_DECAY
    max_decay = math.log(HYENA_TARGET) / HYENA_FAST_DECAY
    delta = jnp.asarray(np.abs(np.linspace(min_decay, max_decay, width)).astype(np.float32)).reshape(1, width)
    return pl.pallas_call(
        functools.partial(_filter_taps_kernel, seq_len=seq_len),
        grid=(HYENA_ORDER, cpb),
        in_specs=[pl.BlockSpec((1, n, hid), lambda o, c: (layer, 0, 0)),
                  pl.BlockSpec((1, hid, ct), lambda o, c: (layer, 0, (2 * o) * cpb + c)),
                  pl.BlockSpec((1, hid, ct), lambda o, c: (layer, 0, (2 * o + 1) * cpb + c)),
                  pl.BlockSpec((1, ct), lambda o, c: (0, c))],
        out_specs=pl.BlockSpec((1, n, ct), lambda o, c: (o, 0, c)),
        out_shape=jax.ShapeDtypeStruct((HYENA_ORDER, n, width), BF16),
        compiler_params=_cparams(2),
        name="filter_taps",
    )(h_all, filt_w4, filt_w4, delta)


def _real_form(e):
    return np.block([[e.real, -e.imag], [e.imag, e.real]])


@functools.lru_cache(maxsize=None)
def _dft_tables(n1, n2):
    n = n1 * n2
    j2 = np.arange(n2)[:, None, None]
    k1 = np.arange(n1)[None, :, None]
    j1 = np.arange(n1)[None, None, :]
    e = np.exp(-2j * np.pi * ((j1 * k1) / n1 + (j2 * k1) / n))
    eh = e[:, :, :n1 // 2]
    t_data = np.concatenate(
        [np.concatenate([eh.real, -eh.imag], axis=2), np.concatenate([eh.imag, eh.real], axis=2)], axis=1)
    t_filt = np.concatenate([e.real, e.imag], axis=1)
    hinv = np.conj(np.transpose(eh, (0, 2, 1))) / n
    t_inv = np.concatenate(
        [np.concatenate([hinv.real, -hinv.imag], axis=2), np.concatenate([hinv.imag, hinv.real], axis=2)], axis=1)
    f2 = np.exp(-2j * np.pi * np.outer(np.arange(n2), np.arange(n2)) / n2)
    as32 = lambda a: np.ascontiguousarray(a, dtype=np.float32)
    return as32(t_data), as32(t_filt), as32(t_inv), as32(_real_form(f2)), as32(_real_form(np.conj(f2)))


def _split_n(n):
    n2 = 1
    while n2 < 128 and n // (n2 * 2) >= 64:
        n2 *= 2
    return n // n2, n2


GS = 16
LANES = 128


def _swap_leading(x):
    return pltpu.einshape("abc->bac", x)


def _fft_s1_kernel(z_ref, t_ref, o_ref, s_ref):
    gs = z_ref.shape[1]
    s_ref[...] = _swap_leading(z_ref[...])
    for g in range(gs):
        o_ref[g] = jnp.dot(t_ref[g], s_ref[g], preferred_element_type=F32).astype(o_ref.dtype)


def _fft_mid_kernel(ar_ref, ai_ref, fr_ref, fi_ref, f2_ref, f2i_ref, o_ref, s_a, s_f):
    n2, gs, _ = ar_ref.shape
    s_a[:, 0:n2, :] = _swap_leading(ar_ref[...])
    s_a[:, n2:2 * n2, :] = _swap_leading(ai_ref[...])
    s_f[:, 0:n2, :] = _swap_leading(fr_ref[...])
    s_f[:, n2:2 * n2, :] = _swap_leading(fi_ref[...])
    f2 = f2_ref[...]
    f2i = f2i_ref[...]

    def spectra(g):
        return (jnp.dot(f2, s_a[g], preferred_element_type=F32),
                jnp.dot(f2, s_f[g], preferred_element_type=F32))

    nxt = spectra(0)
    for g in range(gs):
        u, k = nxt
        if g + 1 < gs:
            nxt = spectra(g + 1)
        ur, ui, kr, ki = u[:n2], u[n2:], k[:n2], k[n2:]
        y = jnp.concatenate([ur * kr - ui * ki, ur * ki + ui * kr], axis=0).astype(BF16)
        o_ref[g] = jnp.dot(f2i, y, preferred_element_type=F32).astype(o_ref.dtype)


def _fft_s3_kernel(cr_ref, ci_ref, t_ref, u_ref, x_ref, bias_ref, o_ref, s_c, s_y):
    n1, gs, _ = cr_ref.shape
    s_c[:, 0:n1, :] = _swap_leading(cr_ref[...])
    s_c[:, n1:2 * n1, :] = _swap_leading(ci_ref[...])
    for g in range(gs):
        s_y[g] = jnp.dot(t_ref[g], s_c[g], preferred_element_type=F32)
    y = _swap_leading(s_y[...])
    u = u_ref[...].astype(F32)
    o_ref[...] = (x_ref[...].astype(F32) * (y + u * bias_ref[...])).astype(o_ref.dtype)


def _hyena_small_kernel(u_ref, k_ref, x_ref, bias_ref, td_ref, tf_ref, ti_ref, _latent_rows_ref, o_ref, *, n):
    u = u_ref[...]
    a = jnp.dot(td_ref[...], u, preferred_element_type=F32)
    k = jnp.dot(tf_ref[...], k_ref[...], preferred_element_type=F32)
    ar, ai, kr, ki = a[:n], a[n:], k[:n], k[n:]
    y = jnp.concatenate([ar * kr - ai * ki, ar * ki + ai * kr], axis=0).astype(BF16)
    y = jnp.dot(ti_ref[...], y, preferred_element_type=F32)
    o_ref[...] = (x_ref[...].astype(F32) * (y + u.astype(F32) * bias_ref[...])).astype(o_ref.dtype)


def _bf16_table(a):
    return jnp.asarray(a).astype(BF16)


def _long_conv_gate(st, u3, u_part, x_arr, x_part, kk, kk_ctx, order, bias):
    assert st.b == 2, "the two batch elements are packed as one complex signal"
    _, rows, width = u3.shape
    l, lc = st.l, st.c
    n = 2 * l
    n1, n2 = _split_n(n)
    t_data, t_filt, t_inv, f2, f2i = (_bf16_table(a) for a in _dft_tables(n1, n2))
    assert n1 % GS == 0 and n2 % GS == 0 and width % LANES == 0
    bias2 = bias.reshape(1, width)
    slabs = rows // n2

    def stage1(z4, part, table, name):
        return pl.pallas_call(
            _fft_s1_kernel,
            grid=(n2 // GS,),
            in_specs=[pl.BlockSpec((None, n1, GS, width), lambda j: (part, 0, j, 0)),
                      pl.BlockSpec((GS, 2 * n1, n1), lambda j: (j, 0, 0))],
            out_specs=pl.BlockSpec((GS, 2 * n1, width), lambda j: (j, 0, 0)),
            out_shape=jax.ShapeDtypeStruct((n2, 2 * n1, width), BF16),
            scratch_shapes=[pltpu.VMEM((GS, n1, width), BF16)],
            compiler_params=_cparams(1),
            name=name,
        )(z4, table)

    a_filt = stage1(kk.reshape(HYENA_ORDER, n1, n2, width), order, t_filt, "fft_stage1_filter")
    u_v = u3.reshape(u3.shape[0], slabs, n2, width)
    a_data = stage1(u_v, u_part, t_data, "fft_stage1_data")

    ctm = _largest_divisor(width, 256, LANES)
    kb_im = n1 // GS
    re_spec = pl.BlockSpec((n2, GS, ctm), lambda c, k: (0, k, c))
    im_spec = pl.BlockSpec((n2, GS, ctm), lambda c, k: (0, kb_im + k, c))
    mat_spec = pl.BlockSpec((2 * n2, 2 * n2), lambda c, k: (0, 0))
    c_mid = pl.pallas_call(
        _fft_mid_kernel,
        grid=(width // ctm, n1 // GS),
        in_specs=[re_spec, im_spec, re_spec, im_spec, mat_spec, mat_spec],
        out_specs=pl.BlockSpec((GS, 2 * n2, ctm), lambda c, k: (k, 0, c)),
        out_shape=jax.ShapeDtypeStruct((n1, 2 * n2, width), BF16),
        scratch_shapes=[pltpu.VMEM((GS, 2 * n2, ctm), BF16)] * 2,
        compiler_params=_cparams(2),
        name="fft_mid",
    )(a_data, a_data, a_filt, a_filt, f2, f2i)

    cts = _largest_divisor(width, 1024, LANES)
    jb_im = n2 // GS
    x_v = x_arr.reshape(x_arr.shape[0], slabs, n2, width)
    out = pl.pallas_call(
        _fft_s3_kernel,
        grid=(width // cts, n2 // GS),
        in_specs=[pl.BlockSpec((n1, GS, cts), lambda c, j: (0, j, c)),
                  pl.BlockSpec((n1, GS, cts), lambda c, j: (0, jb_im + j, c)),
                  pl.BlockSpec((GS, n1, 2 * n1), lambda c, j: (j, 0, 0)),
                  pl.BlockSpec((None, n1, GS, cts), lambda c, j: (u_part, 0, j, c)),
                  pl.BlockSpec((None, n1, GS, cts), lambda c, j: (x_part, 0, j, c)),
                  pl.BlockSpec((1, cts), lambda c, j: (0, c))],
        out_specs=pl.BlockSpec((n1, GS, cts), lambda c, j: (0, j, c)),
        out_shape=jax.ShapeDtypeStruct((slabs, n2, width), BF16),
        scratch_shapes=[pltpu.VMEM((GS, 2 * n1, cts), BF16), pltpu.VMEM((GS, n1, cts), F32)],
        compiler_params=_cparams(2),
        name="fft_stage3",
    )(c_mid, c_mid, t_inv, u_v, x_v, bias2)
    out = out.reshape(rows, width)

    nc = 2 * lc
    tc_data, tc_filt, tc_inv, _, _ = (_bf16_table(a) for a in _dft_tables(nc, 1))
    ct = _largest_divisor(width, 512, 128)
    rb = (st.rows_lat) // (2 * lc)
    out = pl.pallas_call(
        functools.partial(_hyena_small_kernel, n=nc),
        grid=(width // ct,),
        in_specs=[pl.BlockSpec((None, 2 * lc, ct), lambda c: (u_part, rb, c)),
                  pl.BlockSpec((None, nc, ct), lambda c: (order, 0, c)),
                  pl.BlockSpec((None, 2 * lc, ct), lambda c: (x_part, rb, c)),
                  pl.BlockSpec((1, ct), lambda c: (0, c)),
                  pl.BlockSpec((2 * nc, nc), lambda c: (0, 0)),
                  pl.BlockSpec((2 * nc, nc), lambda c: (0, 0)),
                  pl.BlockSpec((nc, 2 * nc), lambda c: (0, 0)),
                  pl.BlockSpec(memory_space=pl.ANY)],
        out_specs=pl.BlockSpec((2 * lc, ct), lambda c: (rb, c)),
        out_shape=jax.ShapeDtypeStruct((rows, width), BF16),
        input_output_aliases={7: 0},
        compiler_params=_cparams(1),
        name="hyena_context",
    )(u3, kk_ctx, x_arr, bias2, tc_data[0], tc_filt[0], tc_inv[0], out)
    return out


@functools.lru_cache(maxsize=None)
def _pool_tables(t, windows):
    out = np.zeros((4, len(windows), t, 3 * t), np.float64)
    for v in range(4):
        lo_bound = t if v & 1 else 0
        hi_bound = 2 * t - 1 if v & 2 else 3 * t - 1
        for g, w in enumerate(windows):
            for r in range(t):
                pos = t + r
                lo = max(pos - w // 2, lo_bound)
                hi = min(pos + w - 1 - w // 2, hi_bound)
                out[v, g, r, lo:hi + 1] = 1.0 / (hi - lo + 1)
                out[v, g, r, pos] -= 1.0
    return out.astype(np.float32)


def _pool_kernel(prev_ref, cur_ref, next_ref, m_ref, w_ref, s_ref, *rest, n_groups, gw):
    o_ref = rest[-1]
    for g in range(n_groups):
        sl = slice(g * gw, (g + 1) * gw)
        u = jnp.concatenate([prev_ref[:, sl], cur_ref[:, sl], next_ref[:, sl]], axis=0)
        d = jnp.dot(m_ref[0, g], u, preferred_element_type=F32)
        y = jnp.dot(d.astype(BF16), w_ref[g], preferred_element_type=F32)
        o_ref[:, sl] = (y * s_ref[:, sl]).astype(o_ref.dtype)


def _pool_call(p, col_block, w_pool_bf, pool_scale3, layer, width, row0, n_seq, seq_len, out_rows, prior):
    t = _largest_divisor(seq_len, 256, 16)
    nblk = seq_len // t
    rb0 = row0 // t
    n_groups = len(POOL_WINDOWS)
    gw = width // n_groups
    tables = _bf16_table(_pool_tables(t, POOL_WINDOWS))

    def blk(delta):
        return lambda s, j: (rb0 + s * nblk + jnp.clip(j + delta, 0, nblk - 1), col_block)

    def variant(s, j):
        return ((j == 0).astype(jnp.int32) + 2 * (j == nblk - 1).astype(jnp.int32), 0, 0, 0)

    in_specs = [pl.BlockSpec((t, width), blk(-1)),
                pl.BlockSpec((t, width), blk(0)),
                pl.BlockSpec((t, width), blk(1)),
                pl.BlockSpec((1, n_groups, t, 3 * t), variant),
                pl.BlockSpec((None, n_groups, gw, gw), lambda s, j: (layer, 0, 0, 0)),
                pl.BlockSpec((None, 1, width), lambda s, j: (layer, 0, 0))]
    args = [p, p, p, tables, w_pool_bf, pool_scale3]
    aliases = {}
    if prior is not None:
        in_specs.append(pl.BlockSpec(memory_space=pl.ANY))
        args.append(prior)
        aliases = {6: 0}
    return pl.pallas_call(
        functools.partial(_pool_kernel, n_groups=n_groups, gw=gw),
        grid=(n_seq, nblk),
        in_specs=in_specs,
        out_specs=pl.BlockSpec((t, width), lambda s, j: (rb0 + s * nblk + j, 0)),
        out_shape=jax.ShapeDtypeStruct((out_rows, width), BF16),
        input_output_aliases=aliases,
        compiler_params=_cparams(2),
        name="pool",
    )(*args)


@functools.lru_cache(maxsize=None)
def _rope_tables(seq_len, grid_w, hd, pad_rows):
    axis = hd // 2
    pos = np.arange(seq_len)
    inv = ROPE_THETA ** (-np.arange(0, axis, 2, dtype=np.float64) / axis)
    ang_r = (pos // grid_w)[:, None] * inv[None]
    ang_c = (pos % grid_w)[:, None] * inv[None]
    cos = np.concatenate([np.cos(ang_r), np.cos(ang_r), np.cos(ang_c), np.cos(ang_c)], axis=1)
    sin = np.concatenate([-np.sin(ang_r), np.sin(ang_r), -np.sin(ang_c), np.sin(ang_c)], axis=1)
    cos = np.concatenate([cos, np.ones((pad_rows, hd))], axis=0)
    sin = np.concatenate([sin, np.zeros((pad_rows, hd))], axis=0)
    return cos.astype(np.float32), sin.astype(np.float32)


def _qk_prep_kernel(q_ref, k_ref, v_ref, qg_ref, kg_ref, cos_ref, sin_ref, qto_ref, ko_ref, vto_ref,
                    *, hd, q_scale):
    cos = cos_ref[...]
    sin = sin_ref[...]
    lane = lax.broadcasted_iota(jnp.int32, cos.shape, 1)
    first = (lane % (hd // 2)) < (hd // 4)
    ones = jnp.ones((hd, hd), BF16)

    def normed_rotated(x_ref, h, g_ref, mul):
        x = x_ref[:, h * hd:(h + 1) * hd].astype(F32)
        ss = jnp.dot((x * x).astype(BF16), ones, preferred_element_type=F32)
        y = x * lax.rsqrt(ss * (1.0 / hd) + EPS) * g_ref[...]
        partner = jnp.where(first, pltpu.roll(y, hd - hd // 4, axis=1), pltpu.roll(y, hd // 4, axis=1))
        return (y * cos + partner * sin) * mul

    for h in range(q_ref.shape[1] // hd):
        qto_ref[h * hd:(h + 1) * hd, :] = normed_rotated(q_ref, h, qg_ref, q_scale).T.astype(qto_ref.dtype)
    for h in range(k_ref.shape[1] // hd):
        sl = slice(h * hd, (h + 1) * hd)
        ko_ref[:, sl] = normed_rotated(k_ref, h, kg_ref, 1.0).astype(ko_ref.dtype)
        vto_ref[sl, :] = v_ref[:, sl].astype(F32).T.astype(vto_ref.dtype)


def _qk_prep(st, p, q_col0, k_col0, aw, kvw, q_gain, k_gain, layer, hd):
    cos_np, sin_np = _rope_tables(st.l, GRID_W, hd, st.tm)
    tab_map = lambda i: (jnp.where(i < st.n_lat_tiles, i % st.tiles_per_seq, st.tiles_per_seq), 0)
    return pl.pallas_call(
        functools.partial(_qk_prep_kernel, hd=hd, q_scale=hd ** -0.5),
        grid=(st.n_tiles,),
        in_specs=[pl.BlockSpec((st.tm, aw), lambda i: (i, q_col0 // aw)),
                  pl.BlockSpec((st.tm, kvw), lambda i: (i, k_col0 // kvw)),
                  pl.BlockSpec((st.tm, kvw), lambda i: (i, k_col0 // kvw + 1)),
                  pl.BlockSpec((None, 1, hd), lambda i: (layer, 0, 0)),
                  pl.BlockSpec((None, 1, hd), lambda i: (layer, 0, 0)),
                  pl.BlockSpec((st.tm, hd), tab_map),
                  pl.BlockSpec((st.tm, hd), tab_map)],
        out_specs=[pl.BlockSpec((aw, st.tm), lambda i: (0, i)),
                   pl.BlockSpec((st.tm, kvw), lambda i: (i, 0)),
                   pl.BlockSpec((kvw, st.tm), lambda i: (0, i))],
        out_shape=[jax.ShapeDtypeStruct((aw, st.rows), BF16),
                   jax.ShapeDtypeStruct((st.rows, kvw), BF16),
                   jax.ShapeDtypeStruct((kvw, st.rows), BF16)],
        compiler_params=_cparams(1),
        name="qk_prep",
    )(p, p, p, q_gain.reshape(-1, 1, hd), k_gain.reshape(-1, 1, hd), jnp.asarray(cos_np), jnp.asarray(sin_np))


ONES_ROWS = 16


def _attn_kernel(*refs, group, hd, tk, n_chunks):
    qt_ref, kc_ref, vtc_ref = refs[:3]
    o_ref, m_s, acc_s, s_a, s_b = refs[-5:]
    tq = qt_ref.shape[1]
    qs = jnp.concatenate([qt_ref[g * hd:(g + 1) * hd, :] for g in range(group)], axis=1)
    m_s[...] = jnp.full(m_s.shape, -1e30, F32)
    acc_s[...] = jnp.zeros(acc_s.shape, F32)

    def scores(k):
        return jnp.dot(k, qs, preferred_element_type=F32)

    def softmax_pv(s, vt):
        vt1 = jnp.concatenate([vt, jnp.ones((ONES_ROWS, vt.shape[1]), BF16)], axis=0)
        m_prev = m_s[...]
        m_new = jnp.maximum(m_prev, jnp.max(s, axis=0, keepdims=True))
        alpha = jnp.exp(m_prev - m_new)
        p = jnp.exp((s - m_new).astype(BF16))
        acc_s[...] = alpha * acc_s[...] + jnp.dot(vt1, p, preferred_element_type=F32)
        m_s[...] = m_new

    s_ctx = scores(kc_ref[...])
    if not n_chunks:
        softmax_pv(s_ctx, vtc_ref[...])
    else:
        kl_ref, vtl_ref = refs[3:5]

        def keys(c):
            return kl_ref[pl.ds(pl.multiple_of(c * tk, tk), tk), :]

        def vals(c):
            return vtl_ref[:, pl.ds(pl.multiple_of(c * tk, tk), tk)]

        first = n_chunks % 2
        if n_chunks - first:
            s_a[...] = scores(keys(first))
        softmax_pv(s_ctx, vtc_ref[...])
        if first:
            softmax_pv(scores(keys(0)), vals(0))
        if n_chunks - first:

            def body(j, carry):
                c = first + 2 * j
                s_b[...] = scores(keys(c + 1))
                softmax_pv(s_a[...], vals(c))
                s_a[...] = scores(keys(jnp.minimum(c + 2, n_chunks - 1)))
                softmax_pv(s_b[...], vals(c + 1))
                return carry

            lax.fori_loop(0, (n_chunks - first) // 2, body, 0)
    o = acc_s[0:hd, :] / acc_s[hd:hd + 1, :]
    for g in range(group):
        o_ref[:, g * hd:(g + 1) * hd] = o[:, g * tq:(g + 1) * tq].T.astype(o_ref.dtype)


def _attention(st, q_t, k, v_t, hd):
    group = N_HEADS // N_KV_HEADS
    gw = group * hd
    aw = N_HEADS * hd
    tq = _largest_divisor(st.l, 256, LANES)
    tk = _largest_divisor(st.l, 512, LANES)
    qt = st.l // tq
    cb0 = st.rows_lat // st.c

    def scratch(q_rows):
        nq = group * q_rows
        return [pltpu.VMEM((1, nq), F32), pltpu.VMEM((hd + ONES_ROWS, nq), F32),
                pltpu.VMEM((tk, nq), F32), pltpu.VMEM((tk, nq), F32)]

    yc = pl.pallas_call(
        functools.partial(_attn_kernel, group=group, hd=hd, tk=tk, n_chunks=st.l // tk),
        grid=(st.b, N_KV_HEADS, qt),
        in_specs=[pl.BlockSpec((gw, tq), lambda b, h, t: (h, b * qt + t)),
                  pl.BlockSpec((st.c, hd), lambda b, h, t: (cb0 + b, h)),
                  pl.BlockSpec((hd, st.c), lambda b, h, t: (h, cb0 + b)),
                  pl.BlockSpec((st.l, hd), lambda b, h, t: (b, h)),
                  pl.BlockSpec((hd, st.l), lambda b, h, t: (h, b))],
        out_specs=pl.BlockSpec((tq, gw), lambda b, h, t: (b * qt + t, h)),
        out_shape=jax.ShapeDtypeStruct((st.rows, aw), BF16),
        scratch_shapes=scratch(tq),
        compiler_params=_cparams(3),
        name="attention_latent",
    )(q_t, k, v_t, k, v_t)
    return pl.pallas_call(
        functools.partial(_attn_kernel, group=group, hd=hd, tk=tk, n_chunks=0),
        grid=(st.b, N_KV_HEADS),
        in_specs=[pl.BlockSpec((gw, st.c), lambda b, h: (h, cb0 + b)),
                  pl.BlockSpec((st.c, hd), lambda b, h: (cb0 + b, h)),
                  pl.BlockSpec((hd, st.c), lambda b, h: (h, cb0 + b)),
                  pl.BlockSpec(memory_space=pl.ANY)],
        out_specs=pl.BlockSpec((st.c, gw), lambda b, h: (cb0 + b, h)),
        out_shape=jax.ShapeDtypeStruct((st.rows, aw), BF16),
        input_output_aliases={3: 0},
        scratch_shapes=scratch(st.c),
        compiler_params=_cparams(2),
        name="attention_context",
    )(q_t, k, v_t, yc)


def kernel(x, c, ctx, c_ctx, w_mod, b_mod, norm_gain, final_gain, ffn_w_in, ffn_w_out, w_in, b_gate, conv_w, conv_b, filt_w1, filt_b1, filt_w2, filt_b2, filt_w3, filt_b3, filt_w4, filt_freq, hyena_bias, w_pool, pool_scale, q_gain, k_gain, w_up, w_out):
    b, l, d = x.shape
    lc = ctx.shape[1]
    depth = w_mod.shape[0]
    width = hyena_bias.shape[2]
    hd = q_gain.shape[1]
    aw, kvw = N_HEADS * hd, N_KV_HEADS * hd
    a1 = (HYENA_ORDER + 1) * width
    q0 = a1 + width
    k0 = q0 + aw
    v0 = k0 + kvw
    g0 = v0 + kvw
    st = _Stream(b, l, lc)

    xs = jnp.concatenate([x.reshape(b * l, d), ctx.reshape(b * lc, d)], axis=0)
    cvec = jnp.zeros((8, d), F32).at[:b].set(c).at[b].set(c_ctx)
    mods = _modulation(cvec, w_mod, b_mod).reshape(depth, 8, N_MOD, d)[:, :b + 1]
    mods = mods.reshape(depth * (b + 1) * N_MOD, 1, d)
    gains = norm_gain.reshape(depth * 3, 1, d)
    b_gate3 = b_gate.reshape(depth * N_BRANCH, 1, d)
    pool_scale3 = pool_scale.reshape(depth, 1, width)
    w_pool_bf = w_pool.astype(BF16)
    w_up_bf = w_up.astype(BF16)
    w_out_bf = w_out.astype(BF16)
    h_lat =_filter_mlp(l, filt_w1, filt_b1, filt_w2, filt_b2, filt_w3, filt_b3, filt_freq)
    h_ctx = _filter_mlp(lc, filt_w1, filt_b1, filt_w2, filt_b2, filt_w3, filt_b3, filt_freq)

    for i in range(depth):
        h = _adaln(st, xs, gains, mods, i, 0, 0, 1)
        a, w_down = _ffn_in(st, h, ffn_w_in, ffn_w_out, i, 0)
        xs = _mm_residual(st, a, w_down, (), xs, mods, i, 2, 0.5, 512)
        h = _adaln(st, xs, gains, mods, i, 1, 3, 4)
        p = _in_proj(st, h, w_in, i)
        cw, cb = conv_w[i], conv_b[i].reshape(1, -1)
        u3 = _short_conv_call(p, cw, cb, width, 0, st.rows_lat, l, st.rows, None)
        u3 = _short_conv_call(p, cw, cb, width, st.rows_lat, b * lc, lc, st.rows, u3)
        kk = _filter_taps(h_lat, filt_w4, i, l, width)
        kk_ctx = _filter_taps(h_ctx, filt_w4, i, lc, width)
        z = _long_conv_gate(st, u3, 0, u3, 1, kk, kk_ctx, 0, hyena_bias[i, 0])
        ya = _long_conv_gate(st, z[None], 0, u3, 2, kk, kk_ctx, 1, hyena_bias[i, 1])
        yb = _pool_call(p, a1 // width, w_pool_bf, pool_scale3, i, width, 0, b, l, st.rows, None)
        yb = _pool_call(p, a1 // width, w_pool_bf, pool_scale3, i, width, st.rows_lat, b, lc, st.rows, yb)
        q_t, k, v_t = _qk_prep(st, p, q0, k0, aw, kvw, q_gain, k_gain, i, hd)
        yc = _attention(st, q_t, k, v_t, hd)
        merged = _merge(st, ya, yb, yc, p, g0, b_gate3, w_up_bf, i)
        xs = _mm_residual(st, merged, w_out_bf, (i,), xs, mods, i, 5, 1.0, 512)
        h = _adaln(st, xs, gains, mods, i, 2, 6, 7)
        a, w_down = _ffn_in(st, h, ffn_w_in, ffn_w_out, i, 1)
        xs = _mm_residual(st, a, w_down, (), xs, mods, i, 8, 0.5, 512)

    out = pl.pallas_call(
        _rmsnorm_kernel,
        grid=(st.n_lat_tiles,),
        in_specs=[pl.BlockSpec((st.tm, d), lambda i: (i, 0)),
                  pl.BlockSpec((1, d), lambda i: (0, 0))],
        out_specs=pl.BlockSpec((st.tm, d), lambda i: (i, 0)),
        out_shape=jax.ShapeDtypeStruct((b * l, d), F32),
        compiler_params=_cparams(1),
        name="final_rmsnorm",
    )(xs, final_gain.reshape(1, d))
    return out.reshape(b, l, d)
```

```python
import functools
import math

import numpy as np
import jax
import jax.numpy as jnp
from jax import lax
from jax.experimental import pallas as pl
from jax.experimental.pallas import tpu as pltpu

F32 = jnp.float32
BF16 = jnp.bfloat16

N_HEADS = 16
N_KV_HEADS = 4
GRID_W = 64
ROPE_THETA = 10000.0
HYENA_ORDER = 2
HYENA_TARGET = 1e-2
HYENA_FAST_DECAY = 0.3
HYENA_SLOW_DECAY = 1.5
POOL_WINDOWS = (2, 4, 8, 16)
N_MOD = 9
N_BRANCH = 3
EPS = 1e-6
HIGHEST = lax.Precision.HIGHEST

VMEM_LIMIT_V7X = 56 * 1024 * 1024


def _cparams(n_axes):
    return pltpu.CompilerParams(
        dimension_semantics=("arbitrary",) * n_axes, vmem_limit_bytes=VMEM_LIMIT_V7X)


def _largest_divisor(total, pref, align):
    if total <= pref:
        return total
    t = (pref // align) * align
    while t > align and total % t:
        t -= align
    assert total % t == 0, (total, pref, align)
    return t


def _silu(v):
    return v * jax.nn.sigmoid(v)


def _mod_kernel(c_ref, w_ref, b_ref, o_ref):
    s = _silu(c_ref[...]).astype(BF16)
    w = w_ref[0].astype(BF16)
    o_ref[0] = jnp.dot(s, w, preferred_element_type=F32) + b_ref[0]


def _modulation(cvec, w_mod, b_mod):
    depth, d, nm = w_mod.shape
    tn = _largest_divisor(nm, 1024, 128)
    return pl.pallas_call(
        _mod_kernel,
        grid=(depth, nm // tn),
        in_specs=[pl.BlockSpec((8, d), lambda l, n: (0, 0)),
                  pl.BlockSpec((1, d, tn), lambda l, n: (l, 0, n)),
                  pl.BlockSpec((1, 1, tn), lambda l, n: (l, 0, n))],
        out_specs=pl.BlockSpec((1, 8, tn), lambda l, n: (l, 0, n)),
        out_shape=jax.ShapeDtypeStruct((depth, 8, nm), F32),
        compiler_params=_cparams(2),
        name="modulation",
    )(cvec, w_mod, b_mod.reshape(depth, 1, nm))


def _adaln_kernel(x_ref, g_ref, sh_ref, sc_ref, o_ref):
    x = x_ref[...]
    y = x * lax.rsqrt(jnp.mean(x * x, axis=-1, keepdims=True) + EPS)
    y = y * g_ref[0]
    o_ref[...] = (y * (1.0 + sc_ref[0]) + sh_ref[0]).astype(o_ref.dtype)


def _rmsnorm_kernel(x_ref, g_ref, o_ref):
    x = x_ref[...]
    y = x * lax.rsqrt(jnp.mean(x * x, axis=-1, keepdims=True) + EPS)
    o_ref[...] = y * g_ref[...]


class _Stream:
    def __init__(self, batch, seq, ctx_len):
        self.b, self.l, self.c = batch, seq, ctx_len
        self.rows_lat = batch * seq
        self.rows = batch * (seq + ctx_len)
        self.tm = _largest_divisor(math.gcd(seq, batch * ctx_len), 512, 16)
        self.tiles_per_seq = seq // self.tm
        self.n_lat_tiles = self.rows_lat // self.tm
        self.n_tiles = self.rows // self.tm
        self.tm_mm = _largest_divisor(self.rows, 1088, 16)
        self.n_mm_tiles = self.rows // self.tm_mm

    def group(self, i):
        return jnp.where(i < self.n_lat_tiles, i // self.tiles_per_seq, self.b)

    def per_row(self, tile, tile_rows, group_vals):
        row = tile * tile_rows + lax.broadcasted_iota(jnp.int32, (tile_rows, 1), 0)
        out = group_vals[self.b]
        for g in reversed(range(self.b)):
            out = jnp.where(row < (g + 1) * self.l, group_vals[g], out)
        return out


def _adaln(st, x, gains, mods, layer, j, m_shift, m_scale):
    d = x.shape[1]
    base = layer * (st.b + 1) * N_MOD

    def mod_map(m):
        return lambda i: (base + st.group(i) * N_MOD + m, 0, 0)

    return pl.pallas_call(
        _adaln_kernel,
        grid=(st.n_tiles,),
        in_specs=[pl.BlockSpec((st.tm, d), lambda i: (i, 0)),
                  pl.BlockSpec((1, 1, d), lambda i: (layer * 3 + j, 0, 0)),
                  pl.BlockSpec((1, 1, d), mod_map(m_shift)),
                  pl.BlockSpec((1, 1, d), mod_map(m_scale))],
        out_specs=pl.BlockSpec((st.tm, d), lambda i: (i, 0)),
        out_shape=jax.ShapeDtypeStruct(x.shape, BF16),
        compiler_params=_cparams(1),
        name="adaln",
    )(x, gains, mods, mods)


def _ffn_in_kernel(h_ref, wg_ref, wu_ref, o_ref, wg_s, wu_s):
    @pl.when(pl.program_id(1) == 0)
    def _():
        wg_s[...] = wg_ref[...].astype(BF16)
        wu_s[...] = wu_ref[...].astype(BF16)

    h = h_ref[...]
    g = jnp.dot(h, wg_s[...], preferred_element_type=F32)
    u = jnp.dot(h, wu_s[...], preferred_element_type=F32)
    o_ref[...] = (_silu(g) * u).astype(o_ref.dtype)


def _ffn_in(st, h, ffn_w_in, layer, j):
    d = h.shape[1]
    f = ffn_w_in.shape[3] // 2
    tn = _largest_divisor(f, 512, 128)
    nt = f // tn
    return pl.pallas_call(
        _ffn_in_kernel,
        grid=(nt, st.n_mm_tiles),
        in_specs=[pl.BlockSpec((st.tm_mm, d), lambda n, m: (m, 0)),
                  pl.BlockSpec((None, None, d, tn), lambda n, m: (layer, j, 0, n)),
                  pl.BlockSpec((None, None, d, tn), lambda n, m: (layer, j, 0, n + nt))],
        out_specs=pl.BlockSpec((st.tm_mm, tn), lambda n, m: (m, n)),
        out_shape=jax.ShapeDtypeStruct((st.rows, f), BF16),
        scratch_shapes=[pltpu.VMEM((d, tn), BF16), pltpu.VMEM((d, tn), BF16)],
        compiler_params=_cparams(2),
        name="ffn_in",
    )(h, ffn_w_in, ffn_w_in)


def _mm_res_kernel(a_ref, w_ref, x_ref, *rest, st, gate_scale):
    gate_refs, o_ref = rest[:-1], rest[-1]
    gate = st.per_row(pl.program_id(0), a_ref.shape[0], [g[0] for g in gate_refs])
    y = jnp.dot(a_ref[...], w_ref[...], preferred_element_type=F32)
    o_ref[...] = x_ref[...] + (gate_scale * gate) * y


def _mm_residual(st, a, w_bf, w_index, x, mods, layer, m_gate, gate_scale, tn_pref):
    k = a.shape[1]
    d = x.shape[1]
    tn = _largest_divisor(d, tn_pref, 128)
    base = layer * (st.b + 1) * N_MOD
    lead = (None,) * len(w_index)

    def gate_spec(g):
        return pl.BlockSpec((1, 1, tn), lambda m, n: (base + g * N_MOD + m_gate, 0, n))

    return pl.pallas_call(
        functools.partial(_mm_res_kernel, st=st, gate_scale=gate_scale),
        grid=(st.n_mm_tiles, d // tn),
        in_specs=[pl.BlockSpec((st.tm_mm, k), lambda m, n: (m, 0)),
                  pl.BlockSpec(lead + (k, tn), lambda m, n: tuple(w_index) + (0, n)),
                  pl.BlockSpec((st.tm_mm, tn), lambda m, n: (m, n))]
                 + [gate_spec(g) for g in range(st.b + 1)],
        out_specs=pl.BlockSpec((st.tm_mm, tn), lambda m, n: (m, n)),
        out_shape=jax.ShapeDtypeStruct(x.shape, F32),
        compiler_params=_cparams(2),
        name="matmul_residual",
    )(a, w_bf, x, *([mods] * (st.b + 1)))


def _mm_kernel(h_ref, w_ref, o_ref, w_s):
    @pl.when(pl.program_id(1) == 0)
    def _():
        w_s[...] = w_ref[...].astype(BF16)

    o_ref[...] = jnp.dot(h_ref[...], w_s[...], preferred_element_type=F32).astype(o_ref.dtype)


def _in_proj(st, h, w_in, layer):
    d = h.shape[1]
    n_in = w_in.shape[2]
    tn = _largest_divisor(n_in, 1024, 128)
    return pl.pallas_call(
        _mm_kernel,
        grid=(n_in // tn, st.n_mm_tiles),
        in_specs=[pl.BlockSpec((st.tm_mm, d), lambda n, m: (m, 0)),
                  pl.BlockSpec((None, d, tn), lambda n, m: (layer, 0, n))],
        out_specs=pl.BlockSpec((st.tm_mm, tn), lambda n, m: (m, n)),
        out_shape=jax.ShapeDtypeStruct((st.rows, n_in), BF16),
        scratch_shapes=[pltpu.VMEM((d, tn), BF16)],
        compiler_params=_cparams(2),
        name="in_proj",
    )(h, w_in)


def _merge_kernel(ya_ref, yb_ref, yc_ref, pa_ref, pb_ref, pc_ref, ba_ref, bb_ref, bc_ref, w_ref, o_ref):
    acc = None
    for k, (y_ref, p_ref, b_ref) in enumerate(
            ((ya_ref, pa_ref, ba_ref), (yb_ref, pb_ref, bb_ref), (yc_ref, pc_ref, bc_ref))):
        gate = jax.nn.sigmoid(p_ref[...].astype(F32) + b_ref[0])
        t = gate * jnp.dot(y_ref[...], w_ref[k], preferred_element_type=F32)
        acc = t if acc is None else acc + t
    o_ref[...] = acc.astype(o_ref.dtype)


def _merge(st, ya, yb, yc, p, gate_col0, b_gate3, w_up_bf, layer):
    w = ya.shape[1]
    d = w_up_bf.shape[3]
    tn = _largest_divisor(d, 256, 128)
    gb = gate_col0 // tn
    dt = d // tn
    tm = st.tm_mm
    y_spec = pl.BlockSpec((tm, w), lambda m, n: (m, 0))

    def p_spec(k):
        return pl.BlockSpec((tm, tn), lambda m, n: (m, gb + k * dt + n))

    def b_spec(k):
        return pl.BlockSpec((1, 1, tn), lambda m, n: (layer * N_BRANCH + k, 0, n))

    return pl.pallas_call(
        _merge_kernel,
        grid=(st.n_mm_tiles, dt),
        in_specs=[y_spec, y_spec, y_spec, p_spec(0), p_spec(1), p_spec(2),
                  b_spec(0), b_spec(1), b_spec(2),
                  pl.BlockSpec((None, N_BRANCH, w, tn), lambda m, n: (layer, 0, 0, n))],
        out_specs=pl.BlockSpec((tm, tn), lambda m, n: (m, n)),
        out_shape=jax.ShapeDtypeStruct((st.rows, d), BF16),
        compiler_params=_cparams(2),
        name="merge",
    )(ya, yb, yc, p, p, p, b_gate3, b_gate3, b_gate3, w_up_bf)


HALO = 16


def _short_conv_kernel(cur_ref, prev_ref, next_ref, w_ref, b_ref, o_ref, *, blocks_per_seq):
    j = pl.program_id(1) % blocks_per_seq
    u = cur_ref[...].astype(F32)
    t = u.shape[0]
    prev_row = jnp.where(j == 0, 0.0, prev_ref[HALO - 1:HALO, :].astype(F32))
    next_row = jnp.where(j == blocks_per_seq - 1, 0.0, next_ref[0:1, :].astype(F32))
    row = lax.broadcasted_iota(jnp.int32, u.shape, 0)
    up = jnp.where(row == 0, prev_row, pltpu.roll(u, 1, axis=0))
    un = jnp.where(row == t - 1, next_row, pltpu.roll(u, t - 1, axis=0))
    w = w_ref[...]
    o_ref[...] = (b_ref[...] + up * w[0:1] + u * w[1:2] + un * w[2:3]).astype(o_ref.dtype)


def _short_conv_call(p, conv_w, conv_b, width, row0, n_rows, seq_len, out_rows, prior):
    ts = _largest_divisor(seq_len, 512, HALO)
    ct = _largest_divisor(width, 1024, 128)
    blocks_per_seq = seq_len // ts
    rb0 = row0 // ts
    hb = ts // HALO
    n_hblocks = out_rows // HALO
    cpb = width // ct
    n_parts = HYENA_ORDER + 1

    def cur_map(c, r):
        return (rb0 + r, c)

    def prev_map(c, r):
        return (jnp.maximum((rb0 + r) * hb - 1, 0), c)

    def next_map(c, r):
        return (jnp.minimum((rb0 + r + 1) * hb, n_hblocks - 1), c)

    in_specs = [pl.BlockSpec((ts, ct), cur_map),
                pl.BlockSpec((HALO, ct), prev_map),
                pl.BlockSpec((HALO, ct), next_map),
                pl.BlockSpec((3, ct), lambda c, r: (0, c)),
                pl.BlockSpec((1, ct), lambda c, r: (0, c))]
    args = [p, p, p, conv_w, conv_b]
    aliases = {}
    kernel = functools.partial(_short_conv_kernel, blocks_per_seq=blocks_per_seq)
    if prior is not None:
        in_specs.append(pl.BlockSpec(memory_space=pl.ANY))
        args.append(prior)
        aliases = {5: 0}
        body = kernel
        kernel = lambda c, pv, nx, w, b, _prior, o: body(c, pv, nx, w, b, o)
    return pl.pallas_call(
        kernel,
        grid=(n_parts * cpb, n_rows // ts),
        in_specs=in_specs,
        out_specs=pl.BlockSpec((None, ts, ct), lambda c, r: (c // cpb, rb0 + r, c % cpb)),
        out_shape=jax.ShapeDtypeStruct((n_parts, out_rows, width), BF16),
        input_output_aliases=aliases,
        compiler_params=_cparams(2),
        name="short_conv",
    )(*args)


@functools.lru_cache(maxsize=None)
def _filter_positions(seq_len, emb_dim, pad_dim):
    bands = (emb_dim - 1) // 2
    j = np.arange(seq_len, dtype=np.float64)
    t = j / (seq_len - 1)
    wpos = 2.0 * np.pi * j / seq_len
    f = np.linspace(1e-4, bands - 1, bands)
    z = np.concatenate([t[:, None], np.cos(f[None] * wpos[:, None]), -np.sin(f[None] * wpos[:, None])], axis=1)
    lag = np.concatenate([np.arange(seq_len), [0], np.arange(seq_len - 1, 0, -1)])
    z2 = np.zeros((2 * seq_len, pad_dim), np.float64)
    z2[:, :emb_dim] = z[lag]
    return z2.astype(np.float32)


def _filter_mlp_kernel(z_ref, w1_ref, b1_ref, w2_ref, b2_ref, w3_ref, b3_ref, fr_ref, o_ref):
    fr = fr_ref[0]
    h = jnp.dot(z_ref[...], w1_ref[0], precision=HIGHEST, preferred_element_type=F32)
    h = jnp.sin(fr[0:1] * (h + b1_ref[0]))
    h = jnp.dot(h, w2_ref[0], precision=HIGHEST, preferred_element_type=F32)
    h = jnp.sin(fr[1:2] * (h + b2_ref[0]))
    h = jnp.dot(h, w3_ref[0], precision=HIGHEST, preferred_element_type=F32)
    o_ref[0] = jnp.sin(fr[2:3] * (h + b3_ref[0]))


def _filter_mlp(seq_len, w1, b1, w2, b2, w3, b3, freq):
    depth, emb, hid = w1.shape
    pad = -(-emb // 8) * 8
    n = 2 * seq_len
    z2 = jnp.asarray(_filter_positions(seq_len, emb, pad))
    w1p = jnp.pad(w1, ((0, 0), (0, pad - emb), (0, 0)))
    tr = _largest_divisor(n, 1024, 8)
    wspec = lambda k: pl.BlockSpec((1, k, hid), lambda l, r: (l, 0, 0))
    bspec = pl.BlockSpec((1, 1, hid), lambda l, r: (l, 0, 0))
    return pl.pallas_call(
        _filter_mlp_kernel,
        grid=(depth, n // tr),
        in_specs=[pl.BlockSpec((tr, pad), lambda l, r: (r, 0)),
                  wspec(pad), bspec, wspec(hid), bspec, wspec(hid), bspec,
                  pl.BlockSpec((1, 3, hid), lambda l, r: (l, 0, 0))],
        out_specs=pl.BlockSpec((1, tr, hid), lambda l, r: (l, r, 0)),
        out_shape=jax.ShapeDtypeStruct((depth, n, hid), F32),
        compiler_params=_cparams(2),
        name="filter_mlp",
    )(z2, w1p, b1.reshape(depth, 1, hid), w2, b2.reshape(depth, 1, hid), w3, b3.reshape(depth, 1, hid), freq)


def _filter_taps_kernel(h_ref, wf_ref, wb_ref, delta_ref, o_ref, *, seq_len):
    l = seq_len
    delta = delta_ref[...]
    lag = lax.broadcasted_iota(jnp.int32, (l, 1), 0)
    t_top = lag.astype(F32) / (l - 1.0)
    t_bot = jnp.where(lag == 0, 0, l - lag).astype(F32) / (l - 1.0)
    top = jnp.dot(h_ref[0, 0:l, :].astype(BF16), wf_ref[0].astype(BF16), preferred_element_type=F32)
    bot = jnp.dot(h_ref[0, l:2 * l, :].astype(BF16), wb_ref[0].astype(BF16), preferred_element_type=F32)
    top = top * jnp.exp(-t_top * delta)
    bot = bot * jnp.exp(-t_bot * delta)
    top = top + jnp.where(lag == 0, bot[0:1, :], 0.0)
    bot = jnp.where(lag == 0, 0.0, bot)
    ss = jnp.sum(top * top, axis=0, keepdims=True) + jnp.sum(bot * bot, axis=0, keepdims=True)
    scale = lax.rsqrt(ss + EPS)
    o_ref[0, 0:l, :] = (top * scale).astype(o_ref.dtype)
    o_ref[0, l:2 * l, :] = (bot * scale).astype(o_ref.dtype)


def _filter_taps(h_all, filt_w4, layer, seq_len, width):
    hid = h_all.shape[2]
    n = 2 * seq_len
    ct = 128
    cpb = width // ct
    min_decay = math.log(HYENA_TARGET) / HYENA_SLOW_DECAY
    max_decay = math.log(HYENA_TARGET) / HYENA_FAST_DECAY
    delta = jnp.asarray(np.abs(np.linspace(min_decay, max_decay, width)).astype(np.float32)).reshape(1, width)
    return pl.pallas_call(
        functools.partial(_filter_taps_kernel, seq_len=seq_len),
        grid=(HYENA_ORDER, cpb),
        in_specs=[pl.BlockSpec((1, n, hid), lambda o, c: (layer, 0, 0)),
                  pl.BlockSpec((1, hid, ct), lambda o, c: (layer, 0, (2 * o) * cpb + c)),
                  pl.BlockSpec((1, hid, ct), lambda o, c: (layer, 0, (2 * o + 1) * cpb + c)),
                  pl.BlockSpec((1, ct), lambda o, c: (0, c))],
        out_specs=pl.BlockSpec((1, n, ct), lambda o, c: (o, 0, c)),
        out_shape=jax.ShapeDtypeStruct((HYENA_ORDER, n, width), BF16),
        compiler_params=_cparams(2),
        name="filter_taps",
    )(h_all, filt_w4, filt_w4, delta)


def _real_form(e):
    return np.block([[e.real, -e.imag], [e.imag, e.real]])


@functools.lru_cache(maxsize=None)
def _dft_tables(n1, n2):
    n = n1 * n2
    j2 = np.arange(n2)[:, None, None]
    k1 = np.arange(n1)[None, :, None]
    j1 = np.arange(n1)[None, None, :]
    e = np.exp(-2j * np.pi * ((j1 * k1) / n1 + (j2 * k1) / n))
    eh = e[:, :, :n1 // 2]
    t_data = np.concatenate(
        [np.concatenate([eh.real, -eh.imag], axis=2), np.concatenate([eh.imag, eh.real], axis=2)], axis=1)
    t_filt = np.concatenate([e.real, e.imag], axis=1)
    hinv = np.conj(np.transpose(eh, (0, 2, 1))) / n
    t_inv = np.concatenate(
        [np.concatenate([hinv.real, -hinv.imag], axis=2), np.concatenate([hinv.imag, hinv.real], axis=2)], axis=1)
    f2 = np.exp(-2j * np.pi * np.outer(np.arange(n2), np.arange(n2)) / n2)
    as32 = lambda a: np.ascontiguousarray(a, dtype=np.float32)
    return as32(t_data), as32(t_filt), as32(t_inv), as32(_real_form(f2)), as32(_real_form(np.conj(f2)))


def _split_n(n):
    n2 = 1
    while n2 < 128 and n // (n2 * 2) >= 64:
        n2 *= 2
    return n // n2, n2


GS = 16
LANES = 128


def _swap_leading(x):
    return pltpu.einshape("abc->bac", x)


def _fft_s1_kernel(z_ref, t_ref, o_ref, s_ref):
    gs = z_ref.shape[1]
    s_ref[...] = _swap_leading(z_ref[...])
    for g in range(gs):
        o_ref[g] = jnp.dot(t_ref[g], s_ref[g], preferred_element_type=F32).astype(o_ref.dtype)


def _fft_mid_kernel(ar_ref, ai_ref, fr_ref, fi_ref, f2_ref, f2i_ref, o_ref, s_a, s_f):
    n2, gs, _ = ar_ref.shape
    s_a[:, 0:n2, :] = _swap_leading(ar_ref[...])
    s_a[:, n2:2 * n2, :] = _swap_leading(ai_ref[...])
    s_f[:, 0:n2, :] = _swap_leading(fr_ref[...])
    s_f[:, n2:2 * n2, :] = _swap_leading(fi_ref[...])
    f2 = f2_ref[...]
    f2i = f2i_ref[...]

    def spectra(g):
        return (jnp.dot(f2, s_a[g], preferred_element_type=F32),
                jnp.dot(f2, s_f[g], preferred_element_type=F32))

    nxt = spectra(0)
    for g in range(gs):
        u, k = nxt
        if g + 1 < gs:
            nxt = spectra(g + 1)
        ur, ui, kr, ki = u[:n2], u[n2:], k[:n2], k[n2:]
        y = jnp.concatenate([ur * kr - ui * ki, ur * ki + ui * kr], axis=0).astype(BF16)
        o_ref[g] = jnp.dot(f2i, y, preferred_element_type=F32).astype(o_ref.dtype)


def _fft_s3_kernel(cr_ref, ci_ref, t_ref, u_ref, x_ref, bias_ref, o_ref, s_c, s_y):
    n1, gs, _ = cr_ref.shape
    s_c[:, 0:n1, :] = _swap_leading(cr_ref[...])
    s_c[:, n1:2 * n1, :] = _swap_leading(ci_ref[...])
    for g in range(gs):
        s_y[g] = jnp.dot(t_ref[g], s_c[g], preferred_element_type=F32)
    y = _swap_leading(s_y[...])
    u = u_ref[...].astype(F32)
    o_ref[...] = (x_ref[...].astype(F32) * (y + u * bias_ref[...])).astype(o_ref.dtype)


def _hyena_small_kernel(u_ref, k_ref, x_ref, bias_ref, td_ref, tf_ref, ti_ref, _latent_rows_ref, o_ref, *, n):
    u = u_ref[...]
    a = jnp.dot(td_ref[...], u, preferred_element_type=F32)
    k = jnp.dot(tf_ref[...], k_ref[...], preferred_element_type=F32)
    ar, ai, kr, ki = a[:n], a[n:], k[:n], k[n:]
    y = jnp.concatenate([ar * kr - ai * ki, ar * ki + ai * kr], axis=0).astype(BF16)
    y = jnp.dot(ti_ref[...], y, preferred_element_type=F32)
    o_ref[...] = (x_ref[...].astype(F32) * (y + u.astype(F32) * bias_ref[...])).astype(o_ref.dtype)


def _bf16_table(a):
    return jnp.asarray(a).astype(BF16)


def _long_conv_gate(st, u3, u_part, x_arr, x_part, kk, kk_ctx, order, bias):
    assert st.b == 2, "the two batch elements are packed as one complex signal"
    _, rows, width = u3.shape
    l, lc = st.l, st.c
    n = 2 * l
    n1, n2 = _split_n(n)
    t_data, t_filt, t_inv, f2, f2i = (_bf16_table(a) for a in _dft_tables(n1, n2))
    assert n1 % GS == 0 and n2 % GS == 0 and width % LANES == 0
    bias2 = bias.reshape(1, width)
    slabs = rows // n2

    def stage1(z4, part, table, name):
        return pl.pallas_call(
            _fft_s1_kernel,
            grid=(n2 // GS,),
            in_specs=[pl.BlockSpec((None, n1, GS, width), lambda j: (part, 0, j, 0)),
                      pl.BlockSpec((GS, 2 * n1, n1), lambda j: (j, 0, 0))],
            out_specs=pl.BlockSpec((GS, 2 * n1, width), lambda j: (j, 0, 0)),
            out_shape=jax.ShapeDtypeStruct((n2, 2 * n1, width), BF16),
            scratch_shapes=[pltpu.VMEM((GS, n1, width), BF16)],
            compiler_params=_cparams(1),
            name=name,
        )(z4, table)

    a_filt = stage1(kk.reshape(HYENA_ORDER, n1, n2, width), order, t_filt, "fft_stage1_filter")
    u_v = u3.reshape(u3.shape[0], slabs, n2, width)
    a_data = stage1(u_v, u_part, t_data, "fft_stage1_data")

    ctm = _largest_divisor(width, 256, LANES)
    kb_im = n1 // GS
    re_spec = pl.BlockSpec((n2, GS, ctm), lambda c, k: (0, k, c))
    im_spec = pl.BlockSpec((n2, GS, ctm), lambda c, k: (0, kb_im + k, c))
    mat_spec = pl.BlockSpec((2 * n2, 2 * n2), lambda c, k: (0, 0))
    c_mid = pl.pallas_call(
        _fft_mid_kernel,
        grid=(width // ctm, n1 // GS),
        in_specs=[re_spec, im_spec, re_spec, im_spec, mat_spec, mat_spec],
        out_specs=pl.BlockSpec((GS, 2 * n2, ctm), lambda c, k: (k, 0, c)),
        out_shape=jax.ShapeDtypeStruct((n1, 2 * n2, width), BF16),
        scratch_shapes=[pltpu.VMEM((GS, 2 * n2, ctm), BF16)] * 2,
        compiler_params=_cparams(2),
        name="fft_mid",
    )(a_data, a_data, a_filt, a_filt, f2, f2i)

    cts = _largest_divisor(width, 1024, LANES)
    jb_im = n2 // GS
    x_v = x_arr.reshape(x_arr.shape[0], slabs, n2, width)
    out = pl.pallas_call(
        _fft_s3_kernel,
        grid=(width // cts, n2 // GS),
        in_specs=[pl.BlockSpec((n1, GS, cts), lambda c, j: (0, j, c)),
                  pl.BlockSpec((n1, GS, cts), lambda c, j: (0, jb_im + j, c)),
                  pl.BlockSpec((GS, n1, 2 * n1), lambda c, j: (j, 0, 0)),
                  pl.BlockSpec((None, n1, GS, cts), lambda c, j: (u_part, 0, j, c)),
                  pl.BlockSpec((None, n1, GS, cts), lambda c, j: (x_part, 0, j, c)),
                  pl.BlockSpec((1, cts), lambda c, j: (0, c))],
        out_specs=pl.BlockSpec((n1, GS, cts), lambda c, j: (0, j, c)),
        out_shape=jax.ShapeDtypeStruct((slabs, n2, width), BF16),
        scratch_shapes=[pltpu.VMEM((GS, 2 * n1, cts), BF16), pltpu.VMEM((GS, n1, cts), F32)],
        compiler_params=_cparams(2),
        name="fft_stage3",
    )(c_mid, c_mid, t_inv, u_v, x_v, bias2)
    out = out.reshape(rows, width)

    nc = 2 * lc
    tc_data, tc_filt, tc_inv, _, _ = (_bf16_table(a) for a in _dft_tables(nc, 1))
    ct = _largest_divisor(width, 512, 128)
    rb = (st.rows_lat) // (2 * lc)
    out = pl.pallas_call(
        functools.partial(_hyena_small_kernel, n=nc),
        grid=(width // ct,),
        in_specs=[pl.BlockSpec((None, 2 * lc, ct), lambda c: (u_part, rb, c)),
                  pl.BlockSpec((None, nc, ct), lambda c: (order, 0, c)),
                  pl.BlockSpec((None, 2 * lc, ct), lambda c: (x_part, rb, c)),
                  pl.BlockSpec((1, ct), lambda c: (0, c)),
                  pl.BlockSpec((2 * nc, nc), lambda c: (0, 0)),
                  pl.BlockSpec((2 * nc, nc), lambda c: (0, 0)),
                  pl.BlockSpec((nc, 2 * nc), lambda c: (0, 0)),
                  pl.BlockSpec(memory_space=pl.ANY)],
        out_specs=pl.BlockSpec((2 * lc, ct), lambda c: (rb, c)),
        out_shape=jax.ShapeDtypeStruct((rows, width), BF16),
        input_output_aliases={7: 0},
        compiler_params=_cparams(1),
        name="hyena_context",
    )(u3, kk_ctx, x_arr, bias2, tc_data[0], tc_filt[0], tc_inv[0], out)
    return out


@functools.lru_cache(maxsize=None)
def _pool_tables(t, windows):
    out = np.zeros((4, len(windows), t, 3 * t), np.float64)
    for v in range(4):
        lo_bound = t if v & 1 else 0
        hi_bound = 2 * t - 1 if v & 2 else 3 * t - 1
        for g, w in enumerate(windows):
            for r in range(t):
                pos = t + r
                lo = max(pos - w // 2, lo_bound)
                hi = min(pos + w - 1 - w // 2, hi_bound)
                out[v, g, r, lo:hi + 1] = 1.0 / (hi - lo + 1)
                out[v, g, r, pos] -= 1.0
    return out.astype(np.float32)


def _pool_kernel(prev_ref, cur_ref, next_ref, m_ref, w_ref, s_ref, *rest, n_groups, gw):
    o_ref = rest[-1]
    for g in range(n_groups):
        sl = slice(g * gw, (g + 1) * gw)
        u = jnp.concatenate([prev_ref[:, sl], cur_ref[:, sl], next_ref[:, sl]], axis=0)
        d = jnp.dot(m_ref[0, g], u, preferred_element_type=F32)
        y = jnp.dot(d.astype(BF16), w_ref[g], preferred_element_type=F32)
        o_ref[:, sl] = (y * s_ref[:, sl]).astype(o_ref.dtype)


def _pool_call(p, col_block, w_pool_bf, pool_scale3, layer, width, row0, n_seq, seq_len, out_rows, prior):
    t = _largest_divisor(seq_len, 256, 16)
    nblk = seq_len // t
    rb0 = row0 // t
    n_groups = len(POOL_WINDOWS)
    gw = width // n_groups
    tables = _bf16_table(_pool_tables(t, POOL_WINDOWS))

    def blk(delta):
        return lambda s, j: (rb0 + s * nblk + jnp.clip(j + delta, 0, nblk - 1), col_block)

    def variant(s, j):
        return ((j == 0).astype(jnp.int32) + 2 * (j == nblk - 1).astype(jnp.int32), 0, 0, 0)

    in_specs = [pl.BlockSpec((t, width), blk(-1)),
                pl.BlockSpec((t, width), blk(0)),
                pl.BlockSpec((t, width), blk(1)),
                pl.BlockSpec((1, n_groups, t, 3 * t), variant),
                pl.BlockSpec((None, n_groups, gw, gw), lambda s, j: (layer, 0, 0, 0)),
                pl.BlockSpec((None, 1, width), lambda s, j: (layer, 0, 0))]
    args = [p, p, p, tables, w_pool_bf, pool_scale3]
    aliases = {}
    if prior is not None:
        in_specs.append(pl.BlockSpec(memory_space=pl.ANY))
        args.append(prior)
        aliases = {6: 0}
    return pl.pallas_call(
        functools.partial(_pool_kernel, n_groups=n_groups, gw=gw),
        grid=(n_seq, nblk),
        in_specs=in_specs,
        out_specs=pl.BlockSpec((t, width), lambda s, j: (rb0 + s * nblk + j, 0)),
        out_shape=jax.ShapeDtypeStruct((out_rows, width), BF16),
        input_output_aliases=aliases,
        compiler_params=_cparams(2),
        name="pool",
    )(*args)


@functools.lru_cache(maxsize=None)
def _rope_tables(seq_len, grid_w, hd, pad_rows):
    axis = hd // 2
    pos = np.arange(seq_len)
    inv = ROPE_THETA ** (-np.arange(0, axis, 2, dtype=np.float64) / axis)
    ang_r = (pos // grid_w)[:, None] * inv[None]
    ang_c = (pos % grid_w)[:, None] * inv[None]
    cos = np.concatenate([np.cos(ang_r), np.cos(ang_r), np.cos(ang_c), np.cos(ang_c)], axis=1)
    sin = np.concatenate([-np.sin(ang_r), np.sin(ang_r), -np.sin(ang_c), np.sin(ang_c)], axis=1)
    cos = np.concatenate([cos, np.ones((pad_rows, hd))], axis=0)
    sin = np.concatenate([sin, np.zeros((pad_rows, hd))], axis=0)
    return cos.astype(np.float32), sin.astype(np.float32)


def _qk_prep_kernel(q_ref, k_ref, v_ref, qg_ref, kg_ref, cos_ref, sin_ref, qto_ref, ko_ref, vto_ref,
                    *, hd, q_scale):
    cos = cos_ref[...]
    sin = sin_ref[...]
    lane = lax.broadcasted_iota(jnp.int32, cos.shape, 1)
    first = (lane % (hd // 2)) < (hd // 4)
    ones = jnp.ones((hd, hd), BF16)

    def normed_rotated(x_ref, h, g_ref, mul):
        x = x_ref[:, h * hd:(h + 1) * hd].astype(F32)
        ss = jnp.dot((x * x).astype(BF16), ones, preferred_element_type=F32)
        y = x * lax.rsqrt(ss * (1.0 / hd) + EPS) * g_ref[...]
        partner = jnp.where(first, pltpu.roll(y, hd - hd // 4, axis=1), pltpu.roll(y, hd // 4, axis=1))
        return (y * cos + partner * sin) * mul

    for h in range(q_ref.shape[1] // hd):
        qto_ref[h * hd:(h + 1) * hd, :] = normed_rotated(q_ref, h, qg_ref, q_scale).T.astype(qto_ref.dtype)
    for h in range(k_ref.shape[1] // hd):
        sl = slice(h * hd, (h + 1) * hd)
        ko_ref[:, sl] = normed_rotated(k_ref, h, kg_ref, 1.0).astype(ko_ref.dtype)
        vto_ref[sl, :] = v_ref[:, sl].astype(F32).T.astype(vto_ref.dtype)


def _qk_prep(st, p, q_col0, k_col0, aw, kvw, q_gain, k_gain, layer, hd):
    cos_np, sin_np = _rope_tables(st.l, GRID_W, hd, st.tm)
    tab_map = lambda i: (jnp.where(i < st.n_lat_tiles, i % st.tiles_per_seq, st.tiles_per_seq), 0)
    return pl.pallas_call(
        functools.partial(_qk_prep_kernel, hd=hd, q_scale=hd ** -0.5),
        grid=(st.n_tiles,),
        in_specs=[pl.BlockSpec((st.tm, aw), lambda i: (i, q_col0 // aw)),
                  pl.BlockSpec((st.tm, kvw), lambda i: (i, k_col0 // kvw)),
                  pl.BlockSpec((st.tm, kvw), lambda i: (i, k_col0 // kvw + 1)),
                  pl.BlockSpec((None, 1, hd), lambda i: (layer, 0, 0)),
                  pl.BlockSpec((None, 1, hd), lambda i: (layer, 0, 0)),
                  pl.BlockSpec((st.tm, hd), tab_map),
                  pl.BlockSpec((st.tm, hd), tab_map)],
        out_specs=[pl.BlockSpec((aw, st.tm), lambda i: (0, i)),
                   pl.BlockSpec((st.tm, kvw), lambda i: (i, 0)),
                   pl.BlockSpec((kvw, st.tm), lambda i: (0, i))],
        out_shape=[jax.ShapeDtypeStruct((aw, st.rows), BF16),
                   jax.ShapeDtypeStruct((st.rows, kvw), BF16),
                   jax.ShapeDtypeStruct((kvw, st.rows), BF16)],
        compiler_params=_cparams(1),
        name="qk_prep",
    )(p, p, p, q_gain.reshape(-1, 1, hd), k_gain.reshape(-1, 1, hd), jnp.asarray(cos_np), jnp.asarray(sin_np))


ONES_ROWS = 16

def _attn_kernel(*refs, group, hd, tk, n_chunks):
    qt_ref, kc_ref, vtc_ref = refs[:3]
    o_ref, m_s, acc_s, s_a, s_b = refs[-5:]
    tq = qt_ref.shape[1]
    qs = jnp.concatenate([qt_ref[g * hd:(g + 1) * hd, :] for g in range(group)], axis=1)
    m_s[...] = jnp.full(m_s.shape, -1e30, F32)
    acc_s[...] = jnp.zeros(acc_s.shape, F32)

    def scores(k):
        return jnp.dot(k, qs, preferred_element_type=F32)

    def softmax_pv(s, vt):
        vt1 = jnp.concatenate([vt, jnp.ones((ONES_ROWS, vt.shape[1]), BF16)], axis=0)
        m_prev = m_s[...]
        m_new = jnp.maximum(m_prev, jnp.max(s, axis=0, keepdims=True))
        alpha = jnp.exp(m_prev - m_new)
        p = jnp.exp((s - m_new).astype(BF16))
        acc_s[...] = alpha * acc_s[...] + jnp.dot(vt1, p, preferred_element_type=F32)
        m_s[...] = m_new

    s_ctx = scores(kc_ref[...])
    if not n_chunks:
        softmax_pv(s_ctx, vtc_ref[...])
    else:
        kl_ref, vtl_ref = refs[3:5]

        def keys(c):
            return kl_ref[pl.ds(pl.multiple_of(c * tk, tk), tk), :]

        def vals(c):
            return vtl_ref[:, pl.ds(pl.multiple_of(c * tk, tk), tk)]

        first = n_chunks % 2
        if n_chunks - first:
            s_a[...] = scores(keys(first))
        softmax_pv(s_ctx, vtc_ref[...])
        if first:
            softmax_pv(scores(keys(0)), vals(0))
        if n_chunks - first:

            def body(j, carry):
                c = first + 2 * j
                s_b[...] = scores(keys(c + 1))
                softmax_pv(s_a[...], vals(c))
                s_a[...] = scores(keys(jnp.minimum(c + 2, n_chunks - 1)))
                softmax_pv(s_b[...], vals(c + 1))
                return carry

            lax.fori_loop(0, (n_chunks - first) // 2, body, 0)
    o = acc_s[0:hd, :] / acc_s[hd:hd + 1, :]
    for g in range(group):
        o_ref[:, g * hd:(g + 1) * hd] = o[:, g * tq:(g + 1) * tq].T.astype(o_ref.dtype)


def _attention(st, q_t, k, v_t, hd):
    group = N_HEADS // N_KV_HEADS
    gw = group * hd
    aw = N_HEADS * hd
    tq = _largest_divisor(st.l, 512, LANES)
    tk = _largest_divisor(st.l, 512, LANES)
    qt = st.l // tq
    cb0 = st.rows_lat // st.c

    def scratch(q_rows):
        nq = group * q_rows
        return [pltpu.VMEM((1, nq), F32), pltpu.VMEM((hd + ONES_ROWS, nq), F32),
                pltpu.VMEM((tk, nq), F32), pltpu.VMEM((tk, nq), F32)]

    yc = pl.pallas_call(
        functools.partial(_attn_kernel, group=group, hd=hd, tk=tk, n_chunks=st.l // tk),
        grid=(st.b, N_KV_HEADS, qt),
        in_specs=[pl.BlockSpec((gw, tq), lambda b, h, t: (h, b * qt + t)),
                  pl.BlockSpec((st.c, hd), lambda b, h, t: (cb0 + b, h)),
                  pl.BlockSpec((hd, st.c), lambda b, h, t: (h, cb0 + b)),
                  pl.BlockSpec((st.l, hd), lambda b, h, t: (b, h)),
                  pl.BlockSpec((hd, st.l), lambda b, h, t: (h, b))],
        out_specs=pl.BlockSpec((tq, gw), lambda b, h, t: (b * qt + t, h)),
        out_shape=jax.ShapeDtypeStruct((st.rows, aw), BF16),
        scratch_shapes=scratch(tq),
        compiler_params=_cparams(3),
        name="attention_latent",
    )(q_t, k, v_t, k, v_t)
    return pl.pallas_call(
        functools.partial(_attn_kernel, group=group, hd=hd, tk=tk, n_chunks=0),
        grid=(st.b, N_KV_HEADS),
        in_specs=[pl.BlockSpec((gw, st.c), lambda b, h: (h, cb0 + b)),
                  pl.BlockSpec((st.c, hd), lambda b, h: (cb0 + b, h)),
                  pl.BlockSpec((hd, st.c), lambda b, h: (h, cb0 + b)),
                  pl.BlockSpec(memory_space=pl.ANY)],
        out_specs=pl.BlockSpec((st.c, gw), lambda b, h: (cb0 + b, h)),
        out_shape=jax.ShapeDtypeStruct((st.rows, aw), BF16),
        input_output_aliases={3: 0},
        scratch_shapes=scratch(st.c),
        compiler_params=_cparams(2),
        name="attention_context",
    )(q_t, k, v_t, yc)


def kernel(x, c, ctx, c_ctx, w_mod, b_mod, norm_gain, final_gain, ffn_w_in, ffn_w_out, w_in, b_gate, conv_w, conv_b, filt_w1, filt_b1, filt_w2, filt_b2, filt_w3, filt_b3, filt_w4, filt_freq, hyena_bias, w_pool, pool_scale, q_gain, k_gain, w_up, w_out):
    b, l, d = x.shape
    lc = ctx.shape[1]
    depth = w_mod.shape[0]
    width = hyena_bias.shape[2]
    hd = q_gain.shape[1]
    aw, kvw = N_HEADS * hd, N_KV_HEADS * hd
    a1 = (HYENA_ORDER + 1) * width
    q0 = a1 + width
    k0 = q0 + aw
    v0 = k0 + kvw
    g0 = v0 + kvw
    st = _Stream(b, l, lc)

    xs = jnp.concatenate([x.reshape(b * l, d), ctx.reshape(b * lc, d)], axis=0)
    cvec = jnp.zeros((8, d), F32).at[:b].set(c).at[b].set(c_ctx)
    mods = _modulation(cvec, w_mod, b_mod).reshape(depth, 8, N_MOD, d)[:, :b + 1]
    mods = mods.reshape(depth * (b + 1) * N_MOD, 1, d)
    gains = norm_gain.reshape(depth * 3, 1, d)
    b_gate3 = b_gate.reshape(depth * N_BRANCH, 1, d)
    pool_scale3 = pool_scale.reshape(depth, 1, width)
    w_pool_bf = w_pool.astype(BF16)
    ffn_w_out_bf = ffn_w_out.astype(BF16)
    w_up_bf = w_up.astype(BF16)
    w_out_bf = w_out.astype(BF16)
    h_lat =_filter_mlp(l, filt_w1, filt_b1, filt_w2, filt_b2, filt_w3, filt_b3, filt_freq)
    h_ctx = _filter_mlp(lc, filt_w1, filt_b1, filt_w2, filt_b2, filt_w3, filt_b3, filt_freq)

    for i in range(depth):
        h = _adaln(st, xs, gains, mods, i, 0, 0, 1)
        a = _ffn_in(st, h, ffn_w_in, i, 0)
        xs = _mm_residual(st, a, ffn_w_out_bf, (i, 0), xs, mods, i, 2, 0.5, 512)
        h = _adaln(st, xs, gains, mods, i, 1, 3, 4)
        p = _in_proj(st, h, w_in, i)
        cw, cb = conv_w[i], conv_b[i].reshape(1, -1)
        u3 = _short_conv_call(p, cw, cb, width, 0, st.rows_lat, l, st.rows, None)
        u3 = _short_conv_call(p, cw, cb, width, st.rows_lat, b * lc, lc, st.rows, u3)
        kk = _filter_taps(h_lat, filt_w4, i, l, width)
        kk_ctx = _filter_taps(h_ctx, filt_w4, i, lc, width)
        z = _long_conv_gate(st, u3, 0, u3, 1, kk, kk_ctx, 0, hyena_bias[i, 0])
        ya = _long_conv_gate(st, z[None], 0, u3, 2, kk, kk_ctx, 1, hyena_bias[i, 1])
        yb = _pool_call(p, a1 // width, w_pool_bf, pool_scale3, i, width, 0, b, l, st.rows, None)
        yb = _pool_call(p, a1 // width, w_pool_bf, pool_scale3, i, width, st.rows_lat, b, lc, st.rows, yb)
        q_t, k, v_t = _qk_prep(st, p, q0, k0, aw, kvw, q_gain, k_gain, i, hd)
        yc = _attention(st, q_t, k, v_t, hd)
        merged = _merge(st, ya, yb, yc, p, g0, b_gate3, w_up_bf, i)
        xs = _mm_residual(st, merged, w_out_bf, (i,), xs, mods, i, 5, 1.0, 512)
        h = _adaln(st, xs, gains, mods, i, 2, 6, 7)
        a = _ffn_in(st, h, ffn_w_in, i, 1)
        xs = _mm_residual(st, a, ffn_w_out_bf, (i, 1), xs, mods, i, 8, 0.5, 512)

    out = pl.pallas_call(
        _rmsnorm_kernel,
        grid=(st.n_lat_tiles,),
        in_specs=[pl.BlockSpec((st.tm, d), lambda i: (i, 0)),
                  pl.BlockSpec((1, d), lambda i: (0, 0))],
        out_specs=pl.BlockSpec((st.tm, d), lambda i: (i, 0)),
        out_shape=jax.ShapeDtypeStruct((b * l, d), F32),
        compiler_params=_cparams(1),
        name="final_rmsnorm",
    )(xs, final_gain.reshape(1, d))
    return out.reshape(b, l, d)
```

```python
import functools
import math

import numpy as np
import jax
import jax.numpy as jnp
from jax import lax
from jax.experimental import pallas as pl
from jax.experimental.pallas import tpu as pltpu

F32 = jnp.float32
BF16 = jnp.bfloat16

N_HEADS = 16
N_KV_HEADS = 4
GRID_W = 64
ROPE_THETA = 10000.0
HYENA_ORDER = 2
HYENA_TARGET = 1e-2
HYENA_FAST_DECAY = 0.3
HYENA_SLOW_DECAY = 1.5
POOL_WINDOWS = (2, 4, 8, 16)
N_MOD = 9
N_BRANCH = 3
EPS = 1e-6
HIGHEST = lax.Precision.HIGHEST

VMEM_LIMIT_V7X = 56 * 1024 * 1024


def _cparams(n_axes):
    return pltpu.CompilerParams(
        dimension_semantics=("arbitrary",) * n_axes, vmem_limit_bytes=VMEM_LIMIT_V7X)


def _largest_divisor(total, pref, align):
    if total <= pref:
        return total
    t = (pref // align) * align
    while t > align and total % t:
        t -= align
    assert total % t == 0, (total, pref, align)
    return t


def _silu(v):
    return v * jax.nn.sigmoid(v)


def _mod_kernel(c_ref, w_ref, b_ref, o_ref):
    s = _silu(c_ref[...]).astype(BF16)
    w = w_ref[0].astype(BF16)
    o_ref[0] = jnp.dot(s, w, preferred_element_type=F32) + b_ref[0]


def _modulation(cvec, w_mod, b_mod):
    depth, d, nm = w_mod.shape
    tn = _largest_divisor(nm, 1024, 128)
    return pl.pallas_call(
        _mod_kernel,
        grid=(depth, nm // tn),
        in_specs=[pl.BlockSpec((8, d), lambda l, n: (0, 0)),
                  pl.BlockSpec((1, d, tn), lambda l, n: (l, 0, n)),
                  pl.BlockSpec((1, 1, tn), lambda l, n: (l, 0, n))],
        out_specs=pl.BlockSpec((1, 8, tn), lambda l, n: (l, 0, n)),
        out_shape=jax.ShapeDtypeStruct((depth, 8, nm), F32),
        compiler_params=_cparams(2),
        name="modulation",
    )(cvec, w_mod, b_mod.reshape(depth, 1, nm))


def _adaln_kernel(x_ref, g_ref, sh_ref, sc_ref, o_ref):
    x = x_ref[...]
    y = x * lax.rsqrt(jnp.mean(x * x, axis=-1, keepdims=True) + EPS)
    y = y * g_ref[0]
    o_ref[...] = (y * (1.0 + sc_ref[0]) + sh_ref[0]).astype(o_ref.dtype)


def _rmsnorm_kernel(x_ref, g_ref, o_ref):
    x = x_ref[...]
    y = x * lax.rsqrt(jnp.mean(x * x, axis=-1, keepdims=True) + EPS)
    o_ref[...] = y * g_ref[...]


class _Stream:
    def __init__(self, batch, seq, ctx_len):
        self.b, self.l, self.c = batch, seq, ctx_len
        self.rows_lat = batch * seq
        self.rows = batch * (seq + ctx_len)
        self.tm = _largest_divisor(math.gcd(seq, batch * ctx_len), 512, 16)
        self.tiles_per_seq = seq // self.tm
        self.n_lat_tiles = self.rows_lat // self.tm
        self.n_tiles = self.rows // self.tm
        self.tm_mm = _largest_divisor(self.rows, 1088, 16)
        self.n_mm_tiles = self.rows // self.tm_mm

    def group(self, i):
        return jnp.where(i < self.n_lat_tiles, i // self.tiles_per_seq, self.b)

    def per_row(self, tile, tile_rows, group_vals):
        row = tile * tile_rows + lax.broadcasted_iota(jnp.int32, (tile_rows, 1), 0)
        out = group_vals[self.b]
        for g in reversed(range(self.b)):
            out = jnp.where(row < (g + 1) * self.l, group_vals[g], out)
        return out


def _adaln(st, x, gains, mods, layer, j, m_shift, m_scale):
    d = x.shape[1]
    base = layer * (st.b + 1) * N_MOD

    def mod_map(m):
        return lambda i: (base + st.group(i) * N_MOD + m, 0, 0)

    return pl.pallas_call(
        _adaln_kernel,
        grid=(st.n_tiles,),
        in_specs=[pl.BlockSpec((st.tm, d), lambda i: (i, 0)),
                  pl.BlockSpec((1, 1, d), lambda i: (layer * 3 + j, 0, 0)),
                  pl.BlockSpec((1, 1, d), mod_map(m_shift)),
                  pl.BlockSpec((1, 1, d), mod_map(m_scale))],
        out_specs=pl.BlockSpec((st.tm, d), lambda i: (i, 0)),
        out_shape=jax.ShapeDtypeStruct(x.shape, BF16),
        compiler_params=_cparams(1),
        name="adaln",
    )(x, gains, mods, mods)


def _ffn_in_kernel(h_ref, wg_ref, wu_ref, o_ref, wg_s, wu_s):
    @pl.when(pl.program_id(1) == 0)
    def _():
        wg_s[...] = wg_ref[...].astype(BF16)
        wu_s[...] = wu_ref[...].astype(BF16)

    h = h_ref[...]
    g = jnp.dot(h, wg_s[...], preferred_element_type=F32)
    u = jnp.dot(h, wu_s[...], preferred_element_type=F32)
    o_ref[...] = (_silu(g) * u).astype(o_ref.dtype)


def _ffn_in(st, h, ffn_w_in, layer, j):
    d = h.shape[1]
    f = ffn_w_in.shape[3] // 2
    tn = _largest_divisor(f, 512, 128)
    nt = f // tn
    return pl.pallas_call(
        _ffn_in_kernel,
        grid=(nt, st.n_mm_tiles),
        in_specs=[pl.BlockSpec((st.tm_mm, d), lambda n, m: (m, 0)),
                  pl.BlockSpec((None, None, d, tn), lambda n, m: (layer, j, 0, n)),
                  pl.BlockSpec((None, None, d, tn), lambda n, m: (layer, j, 0, n + nt))],
        out_specs=pl.BlockSpec((st.tm_mm, tn), lambda n, m: (m, n)),
        out_shape=jax.ShapeDtypeStruct((st.rows, f), BF16),
        scratch_shapes=[pltpu.VMEM((d, tn), BF16), pltpu.VMEM((d, tn), BF16)],
        compiler_params=_cparams(2),
        name="ffn_in",
    )(h, ffn_w_in, ffn_w_in)


def _mm_res_kernel(a_ref, w_ref, x_ref, *rest, st, gate_scale):
    gate_refs, o_ref = rest[:-1], rest[-1]
    gate = st.per_row(pl.program_id(0), a_ref.shape[0], [g[0] for g in gate_refs])
    y = jnp.dot(a_ref[...], w_ref[...], preferred_element_type=F32)
    o_ref[...] = x_ref[...] + (gate_scale * gate) * y


def _mm_residual(st, a, w_bf, w_index, x, mods, layer, m_gate, gate_scale, tn_pref):
    k = a.shape[1]
    d = x.shape[1]
    tn = _largest_divisor(d, tn_pref, 128)
    base = layer * (st.b + 1) * N_MOD
    lead = (None,) * len(w_index)

    def gate_spec(g):
        return pl.BlockSpec((1, 1, tn), lambda m, n: (base + g * N_MOD + m_gate, 0, n))

    return pl.pallas_call(
        functools.partial(_mm_res_kernel, st=st, gate_scale=gate_scale),
        grid=(st.n_mm_tiles, d // tn),
        in_specs=[pl.BlockSpec((st.tm_mm, k), lambda m, n: (m, 0)),
                  pl.BlockSpec(lead + (k, tn), lambda m, n: tuple(w_index) + (0, n)),
                  pl.BlockSpec((st.tm_mm, tn), lambda m, n: (m, n))]
                 + [gate_spec(g) for g in range(st.b + 1)],
        out_specs=pl.BlockSpec((st.tm_mm, tn), lambda m, n: (m, n)),
        out_shape=jax.ShapeDtypeStruct(x.shape, F32),
        compiler_params=_cparams(2),
        name="matmul_residual",
    )(a, w_bf, x, *([mods] * (st.b + 1)))


def _mm_kernel(h_ref, w_ref, o_ref, w_s):
    @pl.when(pl.program_id(1) == 0)
    def _():
        w_s[...] = w_ref[...].astype(BF16)

    o_ref[...] = jnp.dot(h_ref[...], w_s[...], preferred_element_type=F32).astype(o_ref.dtype)


def _in_proj(st, h, w_in, layer):
    d = h.shape[1]
    n_in = w_in.shape[2]
    tn = _largest_divisor(n_in, 1024, 128)
    return pl.pallas_call(
        _mm_kernel,
        grid=(n_in // tn, st.n_mm_tiles),
        in_specs=[pl.BlockSpec((st.tm_mm, d), lambda n, m: (m, 0)),
                  pl.BlockSpec((None, d, tn), lambda n, m: (layer, 0, n))],
        out_specs=pl.BlockSpec((st.tm_mm, tn), lambda n, m: (m, n)),
        out_shape=jax.ShapeDtypeStruct((st.rows, n_in), BF16),
        scratch_shapes=[pltpu.VMEM((d, tn), BF16)],
        compiler_params=_cparams(2),
        name="in_proj",
    )(h, w_in)


def _merge_kernel(ya_ref, yb_ref, yc_ref, pa_ref, pb_ref, pc_ref, ba_ref, bb_ref, bc_ref, w_ref, o_ref):
    acc = None
    for k, (y_ref, p_ref, b_ref) in enumerate(
            ((ya_ref, pa_ref, ba_ref), (yb_ref, pb_ref, bb_ref), (yc_ref, pc_ref, bc_ref))):
        gate = jax.nn.sigmoid(p_ref[...].astype(F32) + b_ref[0])
        t = gate * jnp.dot(y_ref[...], w_ref[k], preferred_element_type=F32)
        acc = t if acc is None else acc + t
    o_ref[...] = acc.astype(o_ref.dtype)


def _merge(st, ya, yb, yc, p, gate_col0, b_gate3, w_up_bf, layer):
    w = ya.shape[1]
    d = w_up_bf.shape[3]
    tn = _largest_divisor(d, 512, 128)
    gb = gate_col0 // tn
    dt = d // tn
    tm = _largest_divisor(st.rows, 544, 16)
    y_spec = pl.BlockSpec((tm, w), lambda m, n: (m, 0))

    def p_spec(k):
        return pl.BlockSpec((tm, tn), lambda m, n: (m, gb + k * dt + n))

    def b_spec(k):
        return pl.BlockSpec((1, 1, tn), lambda m, n: (layer * N_BRANCH + k, 0, n))

    return pl.pallas_call(
        _merge_kernel,
        grid=(st.rows // tm, dt),
        in_specs=[y_spec, y_spec, y_spec, p_spec(0), p_spec(1), p_spec(2),
                  b_spec(0), b_spec(1), b_spec(2),
                  pl.BlockSpec((None, N_BRANCH, w, tn), lambda m, n: (layer, 0, 0, n))],
        out_specs=pl.BlockSpec((tm, tn), lambda m, n: (m, n)),
        out_shape=jax.ShapeDtypeStruct((st.rows, d), BF16),
        compiler_params=_cparams(2),
        name="merge",
    )(ya, yb, yc, p, p, p, b_gate3, b_gate3, b_gate3, w_up_bf)


HALO = 16


def _short_conv_kernel(cur_ref, prev_ref, next_ref, w_ref, b_ref, o_ref, *, blocks_per_seq):
    j = pl.program_id(1) % blocks_per_seq
    u = cur_ref[...].astype(F32)
    t = u.shape[0]
    prev_row = jnp.where(j == 0, 0.0, prev_ref[HALO - 1:HALO, :].astype(F32))
    next_row = jnp.where(j == blocks_per_seq - 1, 0.0, next_ref[0:1, :].astype(F32))
    row = lax.broadcasted_iota(jnp.int32, u.shape, 0)
    up = jnp.where(row == 0, prev_row, pltpu.roll(u, 1, axis=0))
    un = jnp.where(row == t - 1, next_row, pltpu.roll(u, t - 1, axis=0))
    w = w_ref[...]
    o_ref[...] = (b_ref[...] + up * w[0:1] + u * w[1:2] + un * w[2:3]).astype(o_ref.dtype)


def _short_conv_call(p, conv_w, conv_b, width, row0, n_rows, seq_len, out_rows, prior):
    ts = _largest_divisor(seq_len, 512, HALO)
    ct = _largest_divisor(width, 1024, 128)
    blocks_per_seq = seq_len // ts
    rb0 = row0 // ts
    hb = ts // HALO
    n_hblocks = out_rows // HALO
    cpb = width // ct
    n_parts = HYENA_ORDER + 1

    def cur_map(c, r):
        return (rb0 + r, c)

    def prev_map(c, r):
        return (jnp.maximum((rb0 + r) * hb - 1, 0), c)

    def next_map(c, r):
        return (jnp.minimum((rb0 + r + 1) * hb, n_hblocks - 1), c)

    in_specs = [pl.BlockSpec((ts, ct), cur_map),
                pl.BlockSpec((HALO, ct), prev_map),
                pl.BlockSpec((HALO, ct), next_map),
                pl.BlockSpec((3, ct), lambda c, r: (0, c)),
                pl.BlockSpec((1, ct), lambda c, r: (0, c))]
    args = [p, p, p, conv_w, conv_b]
    aliases = {}
    kernel = functools.partial(_short_conv_kernel, blocks_per_seq=blocks_per_seq)
    if prior is not None:
        in_specs.append(pl.BlockSpec(memory_space=pl.ANY))
        args.append(prior)
        aliases = {5: 0}
        body = kernel
        kernel = lambda c, pv, nx, w, b, _prior, o: body(c, pv, nx, w, b, o)
    return pl.pallas_call(
        kernel,
        grid=(n_parts * cpb, n_rows // ts),
        in_specs=in_specs,
        out_specs=pl.BlockSpec((None, ts, ct), lambda c, r: (c // cpb, rb0 + r, c % cpb)),
        out_shape=jax.ShapeDtypeStruct((n_parts, out_rows, width), BF16),
        input_output_aliases=aliases,
        compiler_params=_cparams(2),
        name="short_conv",
    )(*args)


@functools.lru_cache(maxsize=None)
def _filter_positions(seq_len, emb_dim, pad_dim):
    bands = (emb_dim - 1) // 2
    j = np.arange(seq_len, dtype=np.float64)
    t = j / (seq_len - 1)
    wpos = 2.0 * np.pi * j / seq_len
    f = np.linspace(1e-4, bands - 1, bands)
    z = np.concatenate([t[:, None], np.cos(f[None] * wpos[:, None]), -np.sin(f[None] * wpos[:, None])], axis=1)
    lag = np.concatenate([np.arange(seq_len), [0], np.arange(seq_len - 1, 0, -1)])
    z2 = np.zeros((2 * seq_len, pad_dim), np.float64)
    z2[:, :emb_dim] = z[lag]
    return z2.astype(np.float32)


def _filter_mlp_kernel(z_ref, w1_ref, b1_ref, w2_ref, b2_ref, w3_ref, b3_ref, fr_ref, o_ref):
    fr = fr_ref[0]
    h = jnp.dot(z_ref[...], w1_ref[0], precision=HIGHEST, preferred_element_type=F32)
    h = jnp.sin(fr[0:1] * (h + b1_ref[0]))
    h = jnp.dot(h, w2_ref[0], precision=HIGHEST, preferred_element_type=F32)
    h = jnp.sin(fr[1:2] * (h + b2_ref[0]))
    h = jnp.dot(h, w3_ref[0], precision=HIGHEST, preferred_element_type=F32)
    o_ref[0] = jnp.sin(fr[2:3] * (h + b3_ref[0]))


def _filter_mlp(seq_len, w1, b1, w2, b2, w3, b3, freq):
    depth, emb, hid = w1.shape
    pad = -(-emb // 8) * 8
    n = 2 * seq_len
    z2 = jnp.asarray(_filter_positions(seq_len, emb, pad))
    w1p = jnp.pad(w1, ((0, 0), (0, pad - emb), (0, 0)))
    tr = _largest_divisor(n, 1024, 8)
    wspec = lambda k: pl.BlockSpec((1, k, hid), lambda l, r: (l, 0, 0))
    bspec = pl.BlockSpec((1, 1, hid), lambda l, r: (l, 0, 0))
    return pl.pallas_call(
        _filter_mlp_kernel,
        grid=(depth, n // tr),
        in_specs=[pl.BlockSpec((tr, pad), lambda l, r: (r, 0)),
                  wspec(pad), bspec, wspec(hid), bspec, wspec(hid), bspec,
                  pl.BlockSpec((1, 3, hid), lambda l, r: (l, 0, 0))],
        out_specs=pl.BlockSpec((1, tr, hid), lambda l, r: (l, r, 0)),
        out_shape=jax.ShapeDtypeStruct((depth, n, hid), F32),
        compiler_params=_cparams(2),
        name="filter_mlp",
    )(z2, w1p, b1.reshape(depth, 1, hid), w2, b2.reshape(depth, 1, hid), w3, b3.reshape(depth, 1, hid), freq)


def _filter_taps_kernel(h_ref, wf_ref, wb_ref, delta_ref, o_ref, *, seq_len):
    l = seq_len
    delta = delta_ref[...]
    lag = lax.broadcasted_iota(jnp.int32, (l, 1), 0)
    t_top = lag.astype(F32) / (l - 1.0)
    t_bot = jnp.where(lag == 0, 0, l - lag).astype(F32) / (l - 1.0)
    top = jnp.dot(h_ref[0, 0:l, :].astype(BF16), wf_ref[0].astype(BF16), preferred_element_type=F32)
    bot = jnp.dot(h_ref[0, l:2 * l, :].astype(BF16), wb_ref[0].astype(BF16), preferred_element_type=F32)
    top = top * jnp.exp(-t_top * delta)
    bot = bot * jnp.exp(-t_bot * delta)
    top = top + jnp.where(lag == 0, bot[0:1, :], 0.0)
    bot = jnp.where(lag == 0, 0.0, bot)
    ss = jnp.sum(top * top, axis=0, keepdims=True) + jnp.sum(bot * bot, axis=0, keepdims=True)
    scale = lax.rsqrt(ss + EPS)
    o_ref[0, 0:l, :] = (top * scale).astype(o_ref.dtype)
    o_ref[0, l:2 * l, :] = (bot * scale).astype(o_ref.dtype)


def _filter_taps(h_all, filt_w4, layer, seq_len, width):
    hid = h_all.shape[2]
    n = 2 * seq_len
    ct = 128
    cpb = width // ct
    min_decay = math.log(HYENA_TARGET) / HYENA_SLOW_DECAY
    max_decay = math.log(HYENA_TARGET) / HYENA_FAST_DECAY
    delta = jnp.asarray(np.abs(np.linspace(min_decay, max_decay, width)).astype(np.float32)).reshape(1, width)
    return pl.pallas_call(
        functools.partial(_filter_taps_kernel, seq_len=seq_len),
        grid=(HYENA_ORDER, cpb),
        in_specs=[pl.BlockSpec((1, n, hid), lambda o, c: (layer, 0, 0)),
                  pl.BlockSpec((1, hid, ct), lambda o, c: (layer, 0, (2 * o) * cpb + c)),
                  pl.BlockSpec((1, hid, ct), lambda o, c: (layer, 0, (2 * o + 1) * cpb + c)),
                  pl.BlockSpec((1, ct), lambda o, c: (0, c))],
        out_specs=pl.BlockSpec((1, n, ct), lambda o, c: (o, 0, c)),
        out_shape=jax.ShapeDtypeStruct((HYENA_ORDER, n, width), BF16),
        compiler_params=_cparams(2),
        name="filter_taps",
    )(h_all, filt_w4, filt_w4, delta)


def _real_form(e):
    return np.block([[e.real, -e.imag], [e.imag, e.real]])


@functools.lru_cache(maxsize=None)
def _dft_tables(n1, n2):
    n = n1 * n2
    j2 = np.arange(n2)[:, None, None]
    k1 = np.arange(n1)[None, :, None]
    j1 = np.arange(n1)[None, None, :]
    e = np.exp(-2j * np.pi * ((j1 * k1) / n1 + (j2 * k1) / n))
    eh = e[:, :, :n1 // 2]
    t_data = np.concatenate(
        [np.concatenate([eh.real, -eh.imag], axis=2), np.concatenate([eh.imag, eh.real], axis=2)], axis=1)
    t_filt = np.concatenate([e.real, e.imag], axis=1)
    hinv = np.conj(np.transpose(eh, (0, 2, 1))) / n
    t_inv = np.concatenate(
        [np.concatenate([hinv.real, -hinv.imag], axis=2), np.concatenate([hinv.imag, hinv.real], axis=2)], axis=1)
    f2 = np.exp(-2j * np.pi * np.outer(np.arange(n2), np.arange(n2)) / n2)
    as32 = lambda a: np.ascontiguousarray(a, dtype=np.float32)
    return as32(t_data), as32(t_filt), as32(t_inv), as32(_real_form(f2)), as32(_real_form(np.conj(f2)))


def _split_n(n):
    n2 = 1
    while n2 < 128 and n // (n2 * 2) >= 64:
        n2 *= 2
    return n // n2, n2


GS = 16
LANES = 128


def _swap_leading(x):
    return pltpu.einshape("abc->bac", x)


def _fft_s1_kernel(z_ref, t_ref, o_ref, s_ref):
    gs = z_ref.shape[1]
    s_ref[...] = _swap_leading(z_ref[...])
    for g in range(gs):
        o_ref[g] = jnp.dot(t_ref[g], s_ref[g], preferred_element_type=F32).astype(o_ref.dtype)


def _fft_mid_kernel(ar_ref, ai_ref, fr_ref, fi_ref, f2_ref, f2i_ref, o_ref, s_a, s_f):
    n2, gs, _ = ar_ref.shape
    s_a[:, 0:n2, :] = _swap_leading(ar_ref[...])
    s_a[:, n2:2 * n2, :] = _swap_leading(ai_ref[...])
    s_f[:, 0:n2, :] = _swap_leading(fr_ref[...])
    s_f[:, n2:2 * n2, :] = _swap_leading(fi_ref[...])
    f2 = f2_ref[...]
    f2i = f2i_ref[...]

    def spectra(g):
        return (jnp.dot(f2, s_a[g], preferred_element_type=F32),
                jnp.dot(f2, s_f[g], preferred_element_type=F32))

    nxt = spectra(0)
    for g in range(gs):
        u, k = nxt
        if g + 1 < gs:
            nxt = spectra(g + 1)
        ur, ui, kr, ki = u[:n2], u[n2:], k[:n2], k[n2:]
        y = jnp.concatenate([ur * kr - ui * ki, ur * ki + ui * kr], axis=0).astype(BF16)
        o_ref[g] = jnp.dot(f2i, y, preferred_element_type=F32).astype(o_ref.dtype)


def _fft_s3_kernel(cr_ref, ci_ref, t_ref, u_ref, x_ref, bias_ref, o_ref, s_c, s_y):
    n1, gs, _ = cr_ref.shape
    s_c[:, 0:n1, :] = _swap_leading(cr_ref[...])
    s_c[:, n1:2 * n1, :] = _swap_leading(ci_ref[...])
    for g in range(gs):
        s_y[g] = jnp.dot(t_ref[g], s_c[g], preferred_element_type=F32)
    y = _swap_leading(s_y[...])
    u = u_ref[...].astype(F32)
    o_ref[...] = (x_ref[...].astype(F32) * (y + u * bias_ref[...])).astype(o_ref.dtype)


def _hyena_small_kernel(u_ref, k_ref, x_ref, bias_ref, td_ref, tf_ref, ti_ref, _latent_rows_ref, o_ref, *, n):
    u = u_ref[...]
    a = jnp.dot(td_ref[...], u, preferred_element_type=F32)
    k = jnp.dot(tf_ref[...], k_ref[...], preferred_element_type=F32)
    ar, ai, kr, ki = a[:n], a[n:], k[:n], k[n:]
    y = jnp.concatenate([ar * kr - ai * ki, ar * ki + ai * kr], axis=0).astype(BF16)
    y = jnp.dot(ti_ref[...], y, preferred_element_type=F32)
    o_ref[...] = (x_ref[...].astype(F32) * (y + u.astype(F32) * bias_ref[...])).astype(o_ref.dtype)


def _bf16_table(a):
    return jnp.asarray(a).astype(BF16)


def _long_conv_gate(st, u3, u_part, x_arr, x_part, kk, kk_ctx, order, bias):
    assert st.b == 2, "the two batch elements are packed as one complex signal"
    _, rows, width = u3.shape
    l, lc = st.l, st.c
    n = 2 * l
    n1, n2 = _split_n(n)
    t_data, t_filt, t_inv, f2, f2i = (_bf16_table(a) for a in _dft_tables(n1, n2))
    assert n1 % GS == 0 and n2 % GS == 0 and width % LANES == 0
    bias2 = bias.reshape(1, width)
    slabs = rows // n2

    def stage1(z4, part, table, name):
        return pl.pallas_call(
            _fft_s1_kernel,
            grid=(n2 // GS,),
            in_specs=[pl.BlockSpec((None, n1, GS, width), lambda j: (part, 0, j, 0)),
                      pl.BlockSpec((GS, 2 * n1, n1), lambda j: (j, 0, 0))],
            out_specs=pl.BlockSpec((GS, 2 * n1, width), lambda j: (j, 0, 0)),
            out_shape=jax.ShapeDtypeStruct((n2, 2 * n1, width), BF16),
            scratch_shapes=[pltpu.VMEM((GS, n1, width), BF16)],
            compiler_params=_cparams(1),
            name=name,
        )(z4, table)

    a_filt = stage1(kk.reshape(HYENA_ORDER, n1, n2, width), order, t_filt, "fft_stage1_filter")
    u_v = u3.reshape(u3.shape[0], slabs, n2, width)
    a_data = stage1(u_v, u_part, t_data, "fft_stage1_data")

    ctm = _largest_divisor(width, 256, LANES)
    kb_im = n1 // GS
    re_spec = pl.BlockSpec((n2, GS, ctm), lambda c, k: (0, k, c))
    im_spec = pl.BlockSpec((n2, GS, ctm), lambda c, k: (0, kb_im + k, c))
    mat_spec = pl.BlockSpec((2 * n2, 2 * n2), lambda c, k: (0, 0))
    c_mid = pl.pallas_call(
        _fft_mid_kernel,
        grid=(width // ctm, n1 // GS),
        in_specs=[re_spec, im_spec, re_spec, im_spec, mat_spec, mat_spec],
        out_specs=pl.BlockSpec((GS, 2 * n2, ctm), lambda c, k: (k, 0, c)),
        out_shape=jax.ShapeDtypeStruct((n1, 2 * n2, width), BF16),
        scratch_shapes=[pltpu.VMEM((GS, 2 * n2, ctm), BF16)] * 2,
        compiler_params=_cparams(2),
        name="fft_mid",
    )(a_data, a_data, a_filt, a_filt, f2, f2i)

    cts = _largest_divisor(width, 1024, LANES)
    jb_im = n2 // GS
    x_v = x_arr.reshape(x_arr.shape[0], slabs, n2, width)
    out = pl.pallas_call(
        _fft_s3_kernel,
        grid=(width // cts, n2 // GS),
        in_specs=[pl.BlockSpec((n1, GS, cts), lambda c, j: (0, j, c)),
                  pl.BlockSpec((n1, GS, cts), lambda c, j: (0, jb_im + j, c)),
                  pl.BlockSpec((GS, n1, 2 * n1), lambda c, j: (j, 0, 0)),
                  pl.BlockSpec((None, n1, GS, cts), lambda c, j: (u_part, 0, j, c)),
                  pl.BlockSpec((None, n1, GS, cts), lambda c, j: (x_part, 0, j, c)),
                  pl.BlockSpec((1, cts), lambda c, j: (0, c))],
        out_specs=pl.BlockSpec((n1, GS, cts), lambda c, j: (0, j, c)),
        out_shape=jax.ShapeDtypeStruct((slabs, n2, width), BF16),
        scratch_shapes=[pltpu.VMEM((GS, 2 * n1, cts), BF16), pltpu.VMEM((GS, n1, cts), F32)],
        compiler_params=_cparams(2),
        name="fft_stage3",
    )(c_mid, c_mid, t_inv, u_v, x_v, bias2)
    out = out.reshape(rows, width)

    nc = 2 * lc
    tc_data, tc_filt, tc_inv, _, _ = (_bf16_table(a) for a in _dft_tables(nc, 1))
    ct = _largest_divisor(width, 512, 128)
    rb = (st.rows_lat) // (2 * lc)
    out = pl.pallas_call(
        functools.partial(_hyena_small_kernel, n=nc),
        grid=(width // ct,),
        in_specs=[pl.BlockSpec((None, 2 * lc, ct), lambda c: (u_part, rb, c)),
                  pl.BlockSpec((None, nc, ct), lambda c: (order, 0, c)),
                  pl.BlockSpec((None, 2 * lc, ct), lambda c: (x_part, rb, c)),
                  pl.BlockSpec((1, ct), lambda c: (0, c)),
                  pl.BlockSpec((2 * nc, nc), lambda c: (0, 0)),
                  pl.BlockSpec((2 * nc, nc), lambda c: (0, 0)),
                  pl.BlockSpec((nc, 2 * nc), lambda c: (0, 0)),
                  pl.BlockSpec(memory_space=pl.ANY)],
        out_specs=pl.BlockSpec((2 * lc, ct), lambda c: (rb, c)),
        out_shape=jax.ShapeDtypeStruct((rows, width), BF16),
        input_output_aliases={7: 0},
        compiler_params=_cparams(1),
        name="hyena_context",
    )(u3, kk_ctx, x_arr, bias2, tc_data[0], tc_filt[0], tc_inv[0], out)
    return out


@functools.lru_cache(maxsize=None)
def _pool_tables(t, windows):
    out = np.zeros((4, len(windows), t, 3 * t), np.float64)
    for v in range(4):
        lo_bound = t if v & 1 else 0
        hi_bound = 2 * t - 1 if v & 2 else 3 * t - 1
        for g, w in enumerate(windows):
            for r in range(t):
                pos = t + r
                lo = max(pos - w // 2, lo_bound)
                hi = min(pos + w - 1 - w // 2, hi_bound)
                out[v, g, r, lo:hi + 1] = 1.0 / (hi - lo + 1)
                out[v, g, r, pos] -= 1.0
    return out.astype(np.float32)


def _pool_kernel(prev_ref, cur_ref, next_ref, m_ref, w_ref, s_ref, *rest, n_groups, gw):
    o_ref = rest[-1]
    for g in range(n_groups):
        sl = slice(g * gw, (g + 1) * gw)
        u = jnp.concatenate([prev_ref[:, sl], cur_ref[:, sl], next_ref[:, sl]], axis=0)
        d = jnp.dot(m_ref[0, g], u, preferred_element_type=F32)
        y = jnp.dot(d.astype(BF16), w_ref[g], preferred_element_type=F32)
        o_ref[:, sl] = (y * s_ref[:, sl]).astype(o_ref.dtype)


def _pool_call(p, col_block, w_pool_bf, pool_scale3, layer, width, row0, n_seq, seq_len, out_rows, prior):
    t = _largest_divisor(seq_len, 256, 16)
    nblk = seq_len // t
    rb0 = row0 // t
    n_groups = len(POOL_WINDOWS)
    gw = width // n_groups
    tables = _bf16_table(_pool_tables(t, POOL_WINDOWS))

    def blk(delta):
        return lambda s, j: (rb0 + s * nblk + jnp.clip(j + delta, 0, nblk - 1), col_block)

    def variant(s, j):
        return ((j == 0).astype(jnp.int32) + 2 * (j == nblk - 1).astype(jnp.int32), 0, 0, 0)

    in_specs = [pl.BlockSpec((t, width), blk(-1)),
                pl.BlockSpec((t, width), blk(0)),
                pl.BlockSpec((t, width), blk(1)),
                pl.BlockSpec((1, n_groups, t, 3 * t), variant),
                pl.BlockSpec((None, n_groups, gw, gw), lambda s, j: (layer, 0, 0, 0)),
                pl.BlockSpec((None, 1, width), lambda s, j: (layer, 0, 0))]
    args = [p, p, p, tables, w_pool_bf, pool_scale3]
    aliases = {}
    if prior is not None:
        in_specs.append(pl.BlockSpec(memory_space=pl.ANY))
        args.append(prior)
        aliases = {6: 0}
    return pl.pallas_call(
        functools.partial(_pool_kernel, n_groups=n_groups, gw=gw),
        grid=(n_seq, nblk),
        in_specs=in_specs,
        out_specs=pl.BlockSpec((t, width), lambda s, j: (rb0 + s * nblk + j, 0)),
        out_shape=jax.ShapeDtypeStruct((out_rows, width), BF16),
        input_output_aliases=aliases,
        compiler_params=_cparams(2),
        name="pool",
    )(*args)


@functools.lru_cache(maxsize=None)
def _rope_tables(seq_len, grid_w, hd, pad_rows):
    axis = hd // 2
    pos = np.arange(seq_len)
    inv = ROPE_THETA ** (-np.arange(0, axis, 2, dtype=np.float64) / axis)
    ang_r = (pos // grid_w)[:, None] * inv[None]
    ang_c = (pos % grid_w)[:, None] * inv[None]
    cos = np.concatenate([np.cos(ang_r), np.cos(ang_r), np.cos(ang_c), np.cos(ang_c)], axis=1)
    sin = np.concatenate([-np.sin(ang_r), np.sin(ang_r), -np.sin(ang_c), np.sin(ang_c)], axis=1)
    cos = np.concatenate([cos, np.ones((pad_rows, hd))], axis=0)
    sin = np.concatenate([sin, np.zeros((pad_rows, hd))], axis=0)
    return cos.astype(np.float32), sin.astype(np.float32)


def _qk_prep_kernel(q_ref, k_ref, v_ref, qg_ref, kg_ref, cos_ref, sin_ref, qto_ref, ko_ref, vto_ref,
                    *, hd, q_scale):
    cos = cos_ref[...]
    sin = sin_ref[...]
    lane = lax.broadcasted_iota(jnp.int32, cos.shape, 1)
    first = (lane % (hd // 2)) < (hd // 4)
    ones = jnp.ones((hd, hd), BF16)

    def normed_rotated(x_ref, h, g_ref, mul):
        x = x_ref[:, h * hd:(h + 1) * hd].astype(F32)
        ss = jnp.dot((x * x).astype(BF16), ones, preferred_element_type=F32)
        y = x * lax.rsqrt(ss * (1.0 / hd) + EPS) * g_ref[...]
        partner = jnp.where(first, pltpu.roll(y, hd - hd // 4, axis=1), pltpu.roll(y, hd // 4, axis=1))
        return (y * cos + partner * sin) * mul

    for h in range(q_ref.shape[1] // hd):
        qto_ref[h * hd:(h + 1) * hd, :] = normed_rotated(q_ref, h, qg_ref, q_scale).T.astype(qto_ref.dtype)
    for h in range(k_ref.shape[1] // hd):
        sl = slice(h * hd, (h + 1) * hd)
        ko_ref[:, sl] = normed_rotated(k_ref, h, kg_ref, 1.0).astype(ko_ref.dtype)
        vto_ref[sl, :] = v_ref[:, sl].astype(F32).T.astype(vto_ref.dtype)


def _qk_prep(st, p, q_col0, k_col0, aw, kvw, q_gain, k_gain, layer, hd):
    cos_np, sin_np = _rope_tables(st.l, GRID_W, hd, st.tm)
    tab_map = lambda i: (jnp.where(i < st.n_lat_tiles, i % st.tiles_per_seq, st.tiles_per_seq), 0)
    return pl.pallas_call(
        functools.partial(_qk_prep_kernel, hd=hd, q_scale=hd ** -0.5),
        grid=(st.n_tiles,),
        in_specs=[pl.BlockSpec((st.tm, aw), lambda i: (i, q_col0 // aw)),
                  pl.BlockSpec((st.tm, kvw), lambda i: (i, k_col0 // kvw)),
                  pl.BlockSpec((st.tm, kvw), lambda i: (i, k_col0 // kvw + 1)),
                  pl.BlockSpec((None, 1, hd), lambda i: (layer, 0, 0)),
                  pl.BlockSpec((None, 1, hd), lambda i: (layer, 0, 0)),
                  pl.BlockSpec((st.tm, hd), tab_map),
                  pl.BlockSpec((st.tm, hd), tab_map)],
        out_specs=[pl.BlockSpec((aw, st.tm), lambda i: (0, i)),
                   pl.BlockSpec((st.tm, kvw), lambda i: (i, 0)),
                   pl.BlockSpec((kvw, st.tm), lambda i: (0, i))],
        out_shape=[jax.ShapeDtypeStruct((aw, st.rows), BF16),
                   jax.ShapeDtypeStruct((st.rows, kvw), BF16),
                   jax.ShapeDtypeStruct((kvw, st.rows), BF16)],
        compiler_params=_cparams(1),
        name="qk_prep",
    )(p, p, p, q_gain.reshape(-1, 1, hd), k_gain.reshape(-1, 1, hd), jnp.asarray(cos_np), jnp.asarray(sin_np))


ONES_ROWS = 16

def _attn_kernel(*refs, group, hd, tk, n_chunks):
    qt_ref, kc_ref, vtc_ref = refs[:3]
    o_ref, m_s, acc_s, s_a, s_b = refs[-5:]
    tq = qt_ref.shape[1]
    qs = jnp.concatenate([qt_ref[g * hd:(g + 1) * hd, :] for g in range(group)], axis=1)
    m_s[...] = jnp.full(m_s.shape, -1e30, F32)
    acc_s[...] = jnp.zeros(acc_s.shape, F32)

    def scores(k):
        return jnp.dot(k, qs, preferred_element_type=F32)

    def softmax_pv(s, vt):
        vt1 = jnp.concatenate([vt, jnp.ones((ONES_ROWS, vt.shape[1]), BF16)], axis=0)
        m_prev = m_s[...]
        m_new = jnp.maximum(m_prev, jnp.max(s, axis=0, keepdims=True))
        alpha = jnp.exp(m_prev - m_new)
        p = jnp.exp((s - m_new).astype(BF16))
        acc_s[...] = alpha * acc_s[...] + jnp.dot(vt1, p, preferred_element_type=F32)
        m_s[...] = m_new

    s_ctx = scores(kc_ref[...])
    if not n_chunks:
        softmax_pv(s_ctx, vtc_ref[...])
    else:
        kl_ref, vtl_ref = refs[3:5]

        def keys(c):
            return kl_ref[pl.ds(pl.multiple_of(c * tk, tk), tk), :]

        def vals(c):
            return vtl_ref[:, pl.ds(pl.multiple_of(c * tk, tk), tk)]

        first = n_chunks % 2
        if n_chunks - first:
            s_a[...] = scores(keys(first))
        softmax_pv(s_ctx, vtc_ref[...])
        if first:
            softmax_pv(scores(keys(0)), vals(0))
        if n_chunks - first:

            def body(j, carry):
                c = first + 2 * j
                s_b[...] = scores(keys(c + 1))
                softmax_pv(s_a[...], vals(c))
                s_a[...] = scores(keys(jnp.minimum(c + 2, n_chunks - 1)))
                softmax_pv(s_b[...], vals(c + 1))
                return carry

            lax.fori_loop(0, (n_chunks - first) // 2, body, 0)
    o = acc_s[0:hd, :] / acc_s[hd:hd + 1, :]
    for g in range(group):
        o_ref[:, g * hd:(g + 1) * hd] = o[:, g * tq:(g + 1) * tq].T.astype(o_ref.dtype)


def _attention(st, q_t, k, v_t, hd):
    group = N_HEADS // N_KV_HEADS
    gw = group * hd
    aw = N_HEADS * hd
    tq = _largest_divisor(st.l, 1024, LANES)
    tk = _largest_divisor(st.l, 512, LANES)
    qt = st.l // tq
    cb0 = st.rows_lat // st.c

    def scratch(q_rows):
        nq = group * q_rows
        return [pltpu.VMEM((1, nq), F32), pltpu.VMEM((hd + ONES_ROWS, nq), F32),
                pltpu.VMEM((tk, nq), F32), pltpu.VMEM((tk, nq), F32)]

    yc = pl.pallas_call(
        functools.partial(_attn_kernel, group=group, hd=hd, tk=tk, n_chunks=st.l // tk),
        grid=(st.b, N_KV_HEADS, qt),
        in_specs=[pl.BlockSpec((gw, tq), lambda b, h, t: (h, b * qt + t)),
                  pl.BlockSpec((st.c, hd), lambda b, h, t: (cb0 + b, h)),
                  pl.BlockSpec((hd, st.c), lambda b, h, t: (h, cb0 + b)),
                  pl.BlockSpec((st.l, hd), lambda b, h, t: (b, h)),
                  pl.BlockSpec((hd, st.l), lambda b, h, t: (h, b))],
        out_specs=pl.BlockSpec((tq, gw), lambda b, h, t: (b * qt + t, h)),
        out_shape=jax.ShapeDtypeStruct((st.rows, aw), BF16),
        scratch_shapes=scratch(tq),
        compiler_params=_cparams(3),
        name="attention_latent",
    )(q_t, k, v_t, k, v_t)
    return pl.pallas_call(
        functools.partial(_attn_kernel, group=group, hd=hd, tk=tk, n_chunks=0),
        grid=(st.b, N_KV_HEADS),
        in_specs=[pl.BlockSpec((gw, st.c), lambda b, h: (h, cb0 + b)),
                  pl.BlockSpec((st.c, hd), lambda b, h: (cb0 + b, h)),
                  pl.BlockSpec((hd, st.c), lambda b, h: (h, cb0 + b)),
                  pl.BlockSpec(memory_space=pl.ANY)],
        out_specs=pl.BlockSpec((st.c, gw), lambda b, h: (cb0 + b, h)),
        out_shape=jax.ShapeDtypeStruct((st.rows, aw), BF16),
        input_output_aliases={3: 0},
        scratch_shapes=scratch(st.c),
        compiler_params=_cparams(2),
        name="attention_context",
    )(q_t, k, v_t, yc)


def kernel(x, c, ctx, c_ctx, w_mod, b_mod, norm_gain, final_gain, ffn_w_in, ffn_w_out, w_in, b_gate, conv_w, conv_b, filt_w1, filt_b1, filt_w2, filt_b2, filt_w3, filt_b3, filt_w4, filt_freq, hyena_bias, w_pool, pool_scale, q_gain, k_gain, w_up, w_out):
    b, l, d = x.shape
    lc = ctx.shape[1]
    depth = w_mod.shape[0]
    width = hyena_bias.shape[2]
    hd = q_gain.shape[1]
    aw, kvw = N_HEADS * hd, N_KV_HEADS * hd
    a1 = (HYENA_ORDER + 1) * width
    q0 = a1 + width
    k0 = q0 + aw
    v0 = k0 + kvw
    g0 = v0 + kvw
    st = _Stream(b, l, lc)

    xs = jnp.concatenate([x.reshape(b * l, d), ctx.reshape(b * lc, d)], axis=0)
    cvec = jnp.zeros((8, d), F32).at[:b].set(c).at[b].set(c_ctx)
    mods = _modulation(cvec, w_mod, b_mod).reshape(depth, 8, N_MOD, d)[:, :b + 1]
    mods = mods.reshape(depth * (b + 1) * N_MOD, 1, d)
    gains = norm_gain.reshape(depth * 3, 1, d)
    b_gate3 = b_gate.reshape(depth * N_BRANCH, 1, d)
    pool_scale3 = pool_scale.reshape(depth, 1, width)
    w_pool_bf = w_pool.astype(BF16)
    ffn_w_out_bf = ffn_w_out.astype(BF16)
    w_up_bf = w_up.astype(BF16)
    w_out_bf = w_out.astype(BF16)
    h_lat =_filter_mlp(l, filt_w1, filt_b1, filt_w2, filt_b2, filt_w3, filt_b3, filt_freq)
    h_ctx = _filter_mlp(lc, filt_w1, filt_b1, filt_w2, filt_b2, filt_w3, filt_b3, filt_freq)

    for i in range(depth):
        h = _adaln(st, xs, gains, mods, i, 0, 0, 1)
        a = _ffn_in(st, h, ffn_w_in, i, 0)
        xs = _mm_residual(st, a, ffn_w_out_bf, (i, 0), xs, mods, i, 2, 0.5, 512)
        h = _adaln(st, xs, gains, mods, i, 1, 3, 4)
        p = _in_proj(st, h, w_in, i)
        cw, cb = conv_w[i], conv_b[i].reshape(1, -1)
        u3 = _short_conv_call(p, cw, cb, width, 0, st.rows_lat, l, st.rows, None)
        u3 = _short_conv_call(p, cw, cb, width, st.rows_lat, b * lc, lc, st.rows, u3)
        kk = _filter_taps(h_lat, filt_w4, i, l, width)
        kk_ctx = _filter_taps(h_ctx, filt_w4, i, lc, width)
        z = _long_conv_gate(st, u3, 0, u3, 1, kk, kk_ctx, 0, hyena_bias[i, 0])
        ya = _long_conv_gate(st, z[None], 0, u3, 2, kk, kk_ctx, 1, hyena_bias[i, 1])
        yb = _pool_call(p, a1 // width, w_pool_bf, pool_scale3, i, width, 0, b, l, st.rows, None)
        yb = _pool_call(p, a1 // width, w_pool_bf, pool_scale3, i, width, st.rows_lat, b, lc, st.rows, yb)
        q_t, k, v_t = _qk_prep(st, p, q0, k0, aw, kvw, q_gain, k_gain, i, hd)
        yc = _attention(st, q_t, k, v_t, hd)
        merged = _merge(st, ya, yb, yc, p, g0, b_gate3, w_up_bf, i)
        xs = _mm_residual(st, merged, w_out_bf, (i,), xs, mods, i, 5, 1.0, 512)
        h = _adaln(st, xs, gains, mods, i, 2, 6, 7)
        a = _ffn_in(st, h, ffn_w_in, i, 1)
        xs = _mm_residual(st, a, ffn_w_out_bf, (i, 1), xs, mods, i, 8, 0.5, 512)

    out = pl.pallas_call(
        _rmsnorm_kernel,
        grid=(st.n_lat_tiles,),
        in_specs=[pl.BlockSpec((st.tm, d), lambda i: (i, 0)),
                  pl.BlockSpec((1, d), lambda i: (0, 0))],
        out_specs=pl.BlockSpec((st.tm, d), lambda i: (i, 0)),
        out_shape=jax.ShapeDtypeStruct((b * l, d), F32),
        compiler_params=_cparams(1),
        name="final_rmsnorm",
    )(xs, final_gain.reshape(1, d))
    return out.reshape(b, l, d)
```

```python
import functools
import math

import numpy as np
import jax
import jax.numpy as jnp
from jax import lax
from jax.experimental import pallas as pl
from jax.experimental.pallas import tpu as pltpu

F32 = jnp.float32
BF16 = jnp.bfloat16

N_HEADS = 16
N_KV_HEADS = 4
GRID_W = 64
ROPE_THETA = 10000.0
HYENA_ORDER = 2
HYENA_TARGET = 1e-2
HYENA_FAST_DECAY = 0.3
HYENA_SLOW_DECAY = 1.5
POOL_WINDOWS = (2, 4, 8, 16)
N_MOD = 9
N_BRANCH = 3
EPS = 1e-6
HIGHEST = lax.Precision.HIGHEST

VMEM_LIMIT_V7X = 56 * 1024 * 1024


def _cparams(n_axes):
    return pltpu.CompilerParams(
        dimension_semantics=("arbitrary",) * n_axes, vmem_limit_bytes=VMEM_LIMIT_V7X)


def _largest_divisor(total, pref, align):
    if total <= pref:
        return total
    t = (pref // align) * align
    while t > align and total % t:
        t -= align
    assert total % t == 0, (total, pref, align)
    return t


def _silu(v):
    return v * jax.nn.sigmoid(v)


def _mod_kernel(c_ref, w_ref, b_ref, o_ref):
    s = _silu(c_ref[...]).astype(BF16)
    w = w_ref[0].astype(BF16)
    o_ref[0] = jnp.dot(s, w, preferred_element_type=F32) + b_ref[0]


def _modulation(cvec, w_mod, b_mod):
    depth, d, nm = w_mod.shape
    tn = _largest_divisor(nm, 1024, 128)
    return pl.pallas_call(
        _mod_kernel,
        grid=(depth, nm // tn),
        in_specs=[pl.BlockSpec((8, d), lambda l, n: (0, 0)),
                  pl.BlockSpec((1, d, tn), lambda l, n: (l, 0, n)),
                  pl.BlockSpec((1, 1, tn), lambda l, n: (l, 0, n))],
        out_specs=pl.BlockSpec((1, 8, tn), lambda l, n: (l, 0, n)),
        out_shape=jax.ShapeDtypeStruct((depth, 8, nm), F32),
        compiler_params=_cparams(2),
        name="modulation",
    )(cvec, w_mod, b_mod.reshape(depth, 1, nm))


def _adaln_kernel(x_ref, g_ref, sh_ref, sc_ref, o_ref):
    x = x_ref[...]
    y = x * lax.rsqrt(jnp.mean(x * x, axis=-1, keepdims=True) + EPS)
    y = y * g_ref[0]
    o_ref[...] = (y * (1.0 + sc_ref[0]) + sh_ref[0]).astype(o_ref.dtype)


def _rmsnorm_kernel(x_ref, g_ref, o_ref):
    x = x_ref[...]
    y = x * lax.rsqrt(jnp.mean(x * x, axis=-1, keepdims=True) + EPS)
    o_ref[...] = y * g_ref[...]


class _Stream:
    def __init__(self, batch, seq, ctx_len):
        self.b, self.l, self.c = batch, seq, ctx_len
        self.rows_lat = batch * seq
        self.rows = batch * (seq + ctx_len)
        self.tm = _largest_divisor(math.gcd(seq, batch * ctx_len), 512, 16)
        self.tiles_per_seq = seq // self.tm
        self.n_lat_tiles = self.rows_lat // self.tm
        self.n_tiles = self.rows // self.tm
        self.tm_mm = _largest_divisor(self.rows, 1088, 16)
        self.n_mm_tiles = self.rows // self.tm_mm

    def group(self, i):
        return jnp.where(i < self.n_lat_tiles, i // self.tiles_per_seq, self.b)

    def per_row(self, tile, tile_rows, group_vals):
        row = tile * tile_rows + lax.broadcasted_iota(jnp.int32, (tile_rows, 1), 0)
        out = group_vals[self.b]
        for g in reversed(range(self.b)):
            out = jnp.where(row < (g + 1) * self.l, group_vals[g], out)
        return out


def _adaln(st, x, gains, mods, layer, j, m_shift, m_scale):
    d = x.shape[1]
    base = layer * (st.b + 1) * N_MOD

    def mod_map(m):
        return lambda i: (base + st.group(i) * N_MOD + m, 0, 0)

    return pl.pallas_call(
        _adaln_kernel,
        grid=(st.n_tiles,),
        in_specs=[pl.BlockSpec((st.tm, d), lambda i: (i, 0)),
                  pl.BlockSpec((1, 1, d), lambda i: (layer * 3 + j, 0, 0)),
                  pl.BlockSpec((1, 1, d), mod_map(m_shift)),
                  pl.BlockSpec((1, 1, d), mod_map(m_scale))],
        out_specs=pl.BlockSpec((st.tm, d), lambda i: (i, 0)),
        out_shape=jax.ShapeDtypeStruct(x.shape, BF16),
        compiler_params=_cparams(1),
        name="adaln",
    )(x, gains, mods, mods)


def _ffn_in_kernel(h_ref, wg_ref, wu_ref, o_ref, wg_s, wu_s):
    @pl.when(pl.program_id(1) == 0)
    def _():
        wg_s[...] = wg_ref[...].astype(BF16)
        wu_s[...] = wu_ref[...].astype(BF16)

    h = h_ref[...]
    g = jnp.dot(h, wg_s[...], preferred_element_type=F32)
    u = jnp.dot(h, wu_s[...], preferred_element_type=F32)
    o_ref[...] = (_silu(g) * u).astype(o_ref.dtype)


def _ffn_in(st, h, ffn_w_in, layer, j):
    d = h.shape[1]
    f = ffn_w_in.shape[3] // 2
    tn = _largest_divisor(f, 512, 128)
    nt = f // tn
    return pl.pallas_call(
        _ffn_in_kernel,
        grid=(nt, st.n_mm_tiles),
        in_specs=[pl.BlockSpec((st.tm_mm, d), lambda n, m: (m, 0)),
                  pl.BlockSpec((None, None, d, tn), lambda n, m: (layer, j, 0, n)),
                  pl.BlockSpec((None, None, d, tn), lambda n, m: (layer, j, 0, n + nt))],
        out_specs=pl.BlockSpec((st.tm_mm, tn), lambda n, m: (m, n)),
        out_shape=jax.ShapeDtypeStruct((st.rows, f), BF16),
        scratch_shapes=[pltpu.VMEM((d, tn), BF16), pltpu.VMEM((d, tn), BF16)],
        compiler_params=_cparams(2),
        name="ffn_in",
    )(h, ffn_w_in, ffn_w_in)


def _mm_res_kernel(a_ref, w_ref, x_ref, *rest, st, gate_scale):
    gate_refs, o_ref = rest[:-1], rest[-1]
    gate = st.per_row(pl.program_id(0), a_ref.shape[0], [g[0] for g in gate_refs])
    y = jnp.dot(a_ref[...], w_ref[...], preferred_element_type=F32)
    o_ref[...] = x_ref[...] + (gate_scale * gate) * y


def _mm_residual(st, a, w_bf, w_index, x, mods, layer, m_gate, gate_scale, tn_pref):
    k = a.shape[1]
    d = x.shape[1]
    tn = _largest_divisor(d, tn_pref, 128)
    base = layer * (st.b + 1) * N_MOD
    lead = (None,) * len(w_index)

    def gate_spec(g):
        return pl.BlockSpec((1, 1, tn), lambda m, n: (base + g * N_MOD + m_gate, 0, n))

    return pl.pallas_call(
        functools.partial(_mm_res_kernel, st=st, gate_scale=gate_scale),
        grid=(st.n_mm_tiles, d // tn),
        in_specs=[pl.BlockSpec((st.tm_mm, k), lambda m, n: (m, 0)),
                  pl.BlockSpec(lead + (k, tn), lambda m, n: tuple(w_index) + (0, n)),
                  pl.BlockSpec((st.tm_mm, tn), lambda m, n: (m, n))]
                 + [gate_spec(g) for g in range(st.b + 1)],
        out_specs=pl.BlockSpec((st.tm_mm, tn), lambda m, n: (m, n)),
        out_shape=jax.ShapeDtypeStruct(x.shape, F32),
        compiler_params=_cparams(2),
        name="matmul_residual",
    )(a, w_bf, x, *([mods] * (st.b + 1)))


def _mm_kernel(h_ref, w_ref, o_ref, w_s):
    @pl.when(pl.program_id(1) == 0)
    def _():
        w_s[...] = w_ref[...].astype(BF16)

    o_ref[...] = jnp.dot(h_ref[...], w_s[...], preferred_element_type=F32).astype(o_ref.dtype)


def _in_proj(st, h, w_in, layer):
    d = h.shape[1]
    n_in = w_in.shape[2]
    tn = _largest_divisor(n_in, 1024, 128)
    return pl.pallas_call(
        _mm_kernel,
        grid=(n_in // tn, st.n_mm_tiles),
        in_specs=[pl.BlockSpec((st.tm_mm, d), lambda n, m: (m, 0)),
                  pl.BlockSpec((None, d, tn), lambda n, m: (layer, 0, n))],
        out_specs=pl.BlockSpec((st.tm_mm, tn), lambda n, m: (m, n)),
        out_shape=jax.ShapeDtypeStruct((st.rows, n_in), BF16),
        scratch_shapes=[pltpu.VMEM((d, tn), BF16)],
        compiler_params=_cparams(2),
        name="in_proj",
    )(h, w_in)


def _merge_kernel(ya_ref, yb_ref, yc_ref, pa_ref, pb_ref, pc_ref, ba_ref, bb_ref, bc_ref, w_ref, o_ref):
    acc = None
    for k, (y_ref, p_ref, b_ref) in enumerate(
            ((ya_ref, pa_ref, ba_ref), (yb_ref, pb_ref, bb_ref), (yc_ref, pc_ref, bc_ref))):
        gate = jax.nn.sigmoid(p_ref[...].astype(F32) + b_ref[0])
        t = gate * jnp.dot(y_ref[...], w_ref[k], preferred_element_type=F32)
        acc = t if acc is None else acc + t
    o_ref[...] = acc.astype(o_ref.dtype)


def _merge(st, ya, yb, yc, p, gate_col0, b_gate3, w_up_bf, layer):
    w = ya.shape[1]
    d = w_up_bf.shape[3]
    tn = _largest_divisor(d, 512, 128)
    gb = gate_col0 // tn
    dt = d // tn
    tm = _largest_divisor(st.rows, 544, 16)
    y_spec = pl.BlockSpec((tm, w), lambda m, n: (m, 0))

    def p_spec(k):
        return pl.BlockSpec((tm, tn), lambda m, n: (m, gb + k * dt + n))

    def b_spec(k):
        return pl.BlockSpec((1, 1, tn), lambda m, n: (layer * N_BRANCH + k, 0, n))

    return pl.pallas_call(
        _merge_kernel,
        grid=(st.rows // tm, dt),
        in_specs=[y_spec, y_spec, y_spec, p_spec(0), p_spec(1), p_spec(2),
                  b_spec(0), b_spec(1), b_spec(2),
                  pl.BlockSpec((None, N_BRANCH, w, tn), lambda m, n: (layer, 0, 0, n))],
        out_specs=pl.BlockSpec((tm, tn), lambda m, n: (m, n)),
        out_shape=jax.ShapeDtypeStruct((st.rows, d), BF16),
        compiler_params=_cparams(2),
        name="merge",
    )(ya, yb, yc, p, p, p, b_gate3, b_gate3, b_gate3, w_up_bf)


HALO = 16


def _short_conv_kernel(cur_ref, prev_ref, next_ref, w_ref, b_ref, o_ref, *, blocks_per_seq):
    j = pl.program_id(1) % blocks_per_seq
    u = cur_ref[...].astype(F32)
    t = u.shape[0]
    prev_row = jnp.where(j == 0, 0.0, prev_ref[HALO - 1:HALO, :].astype(F32))
    next_row = jnp.where(j == blocks_per_seq - 1, 0.0, next_ref[0:1, :].astype(F32))
    row = lax.broadcasted_iota(jnp.int32, u.shape, 0)
    up = jnp.where(row == 0, prev_row, pltpu.roll(u, 1, axis=0))
    un = jnp.where(row == t - 1, next_row, pltpu.roll(u, t - 1, axis=0))
    w = w_ref[...]
    o_ref[...] = (b_ref[...] + up * w[0:1] + u * w[1:2] + un * w[2:3]).astype(o_ref.dtype)


def _short_conv_call(p, conv_w, conv_b, width, row0, n_rows, seq_len, out_rows, prior):
    ts = _largest_divisor(seq_len, 512, HALO)
    ct = _largest_divisor(width, 1024, 128)
    blocks_per_seq = seq_len // ts
    rb0 = row0 // ts
    hb = ts // HALO
    n_hblocks = out_rows // HALO
    cpb = width // ct
    n_parts = HYENA_ORDER + 1

    def cur_map(c, r):
        return (rb0 + r, c)

    def prev_map(c, r):
        return (jnp.maximum((rb0 + r) * hb - 1, 0), c)

    def next_map(c, r):
        return (jnp.minimum((rb0 + r + 1) * hb, n_hblocks - 1), c)

    in_specs = [pl.BlockSpec((ts, ct), cur_map),
                pl.BlockSpec((HALO, ct), prev_map),
                pl.BlockSpec((HALO, ct), next_map),
                pl.BlockSpec((3, ct), lambda c, r: (0, c)),
                pl.BlockSpec((1, ct), lambda c, r: (0, c))]
    args = [p, p, p, conv_w, conv_b]
    aliases = {}
    kernel = functools.partial(_short_conv_kernel, blocks_per_seq=blocks_per_seq)
    if prior is not None:
        in_specs.append(pl.BlockSpec(memory_space=pl.ANY))
        args.append(prior)
        aliases = {5: 0}
        body = kernel
        kernel = lambda c, pv, nx, w, b, _prior, o: body(c, pv, nx, w, b, o)
    return pl.pallas_call(
        kernel,
        grid=(n_parts * cpb, n_rows // ts),
        in_specs=in_specs,
        out_specs=pl.BlockSpec((None, ts, ct), lambda c, r: (c // cpb, rb0 + r, c % cpb)),
        out_shape=jax.ShapeDtypeStruct((n_parts, out_rows, width), BF16),
        input_output_aliases=aliases,
        compiler_params=_cparams(2),
        name="short_conv",
    )(*args)


@functools.lru_cache(maxsize=None)
def _filter_positions(seq_len, emb_dim, pad_dim):
    bands = (emb_dim - 1) // 2
    j = np.arange(seq_len, dtype=np.float64)
    t = j / (seq_len - 1)
    wpos = 2.0 * np.pi * j / seq_len
    f = np.linspace(1e-4, bands - 1, bands)
    z = np.concatenate([t[:, None], np.cos(f[None] * wpos[:, None]), -np.sin(f[None] * wpos[:, None])], axis=1)
    lag = np.concatenate([np.arange(seq_len), [0], np.arange(seq_len - 1, 0, -1)])
    z2 = np.zeros((2 * seq_len, pad_dim), np.float64)
    z2[:, :emb_dim] = z[lag]
    return z2.astype(np.float32)


def _filter_mlp_kernel(zt_ref, w1t_ref, w2t_ref, w3t_ref, bf_ref, o_ref):
    bf = bf_ref[0]
    h = jnp.dot(w1t_ref[0], zt_ref[...], precision=HIGHEST, preferred_element_type=F32)
    h = jnp.sin(bf[:, 3:4] * (h + bf[:, 0:1]))
    h = jnp.dot(w2t_ref[0], h, precision=HIGHEST, preferred_element_type=F32)
    h = jnp.sin(bf[:, 4:5] * (h + bf[:, 1:2]))
    h = jnp.dot(w3t_ref[0], h, precision=HIGHEST, preferred_element_type=F32)
    o_ref[0] = jnp.sin(bf[:, 5:6] * (h + bf[:, 2:3]))


def _filter_mlp(seq_len, w1, b1, w2, b2, w3, b3, freq):
    depth, emb, hid = w1.shape
    pad = -(-emb // 8) * 8
    n = 2 * seq_len
    zt = jnp.asarray(np.ascontiguousarray(_filter_positions(seq_len, emb, pad).T))
    w1t = jnp.swapaxes(jnp.pad(w1, ((0, 0), (0, pad - emb), (0, 0))), 1, 2)
    bf = jnp.concatenate([b1[:, :, None], b2[:, :, None], b3[:, :, None], jnp.swapaxes(freq, 1, 2)], axis=2)
    tr = _largest_divisor(n, 2048, LANES)
    wspec = lambda k: pl.BlockSpec((1, hid, k), lambda l, r: (l, 0, 0))
    h_t = pl.pallas_call(
        _filter_mlp_kernel,
        grid=(depth, n // tr),
        in_specs=[pl.BlockSpec((pad, tr), lambda l, r: (0, r)),
                  wspec(pad), wspec(hid), wspec(hid),
                  pl.BlockSpec((1, hid, 6), lambda l, r: (l, 0, 0))],
        out_specs=pl.BlockSpec((1, hid, tr), lambda l, r: (l, 0, r)),
        out_shape=jax.ShapeDtypeStruct((depth, hid, n), F32),
        compiler_params=_cparams(2),
        name="filter_mlp",
    )(zt, w1t, jnp.swapaxes(w2, 1, 2), jnp.swapaxes(w3, 1, 2), bf)
    return jnp.swapaxes(h_t, 1, 2)


def _filter_taps_kernel(h_ref, wf_ref, wb_ref, delta_ref, o_ref, *, seq_len):
    l = seq_len
    delta = delta_ref[...]
    lag = lax.broadcasted_iota(jnp.int32, (l, 1), 0)
    t_top = lag.astype(F32) / (l - 1.0)
    t_bot = jnp.where(lag == 0, 0, l - lag).astype(F32) / (l - 1.0)
    top = jnp.dot(h_ref[0, 0:l, :].astype(BF16), wf_ref[0].astype(BF16), preferred_element_type=F32)
    bot = jnp.dot(h_ref[0, l:2 * l, :].astype(BF16), wb_ref[0].astype(BF16), preferred_element_type=F32)
    top = top * jnp.exp(-t_top * delta)
    bot = bot * jnp.exp(-t_bot * delta)
    top = top + jnp.where(lag == 0, bot[0:1, :], 0.0)
    bot = jnp.where(lag == 0, 0.0, bot)
    ss = jnp.sum(top * top, axis=0, keepdims=True) + jnp.sum(bot * bot, axis=0, keepdims=True)
    scale = lax.rsqrt(ss + EPS)
    o_ref[0, 0:l, :] = (top * scale).astype(o_ref.dtype)
    o_ref[0, l:2 * l, :] = (bot * scale).astype(o_ref.dtype)


def _filter_taps(h_all, filt_w4, layer, seq_len, width):
    hid = h_all.shape[2]
    n = 2 * seq_len
    ct = 128
    cpb = width // ct
    min_decay = math.log(HYENA_TARGET) / HYENA_SLOW_DECAY
    max_decay = math.log(HYENA_TARGET) / HYENA_FAST_DECAY
    delta = jnp.asarray(np.abs(np.linspace(min_decay, max_decay, width)).astype(np.float32)).reshape(1, width)
    return pl.pallas_call(
        functools.partial(_filter_taps_kernel, seq_len=seq_len),
        grid=(HYENA_ORDER, cpb),
        in_specs=[pl.BlockSpec((1, n, hid), lambda o, c: (layer, 0, 0)),
                  pl.BlockSpec((1, hid, ct), lambda o, c: (layer, 0, (2 * o) * cpb + c)),
                  pl.BlockSpec((1, hid, ct), lambda o, c: (layer, 0, (2 * o + 1) * cpb + c)),
                  pl.BlockSpec((1, ct), lambda o, c: (0, c))],
        out_specs=pl.BlockSpec((1, n, ct), lambda o, c: (o, 0, c)),
        out_shape=jax.ShapeDtypeStruct((HYENA_ORDER, n, width), BF16),
        compiler_params=_cparams(2),
        name="filter_taps",
    )(h_all, filt_w4, filt_w4, delta)


def _real_form(e):
    return np.block([[e.real, -e.imag], [e.imag, e.real]])


@functools.lru_cache(maxsize=None)
def _dft_tables(n1, n2):
    n = n1 * n2
    j2 = np.arange(n2)[:, None, None]
    k1 = np.arange(n1)[None, :, None]
    j1 = np.arange(n1)[None, None, :]
    e = np.exp(-2j * np.pi * ((j1 * k1) / n1 + (j2 * k1) / n))
    eh = e[:, :, :n1 // 2]
    t_data = np.concatenate(
        [np.concatenate([eh.real, -eh.imag], axis=2), np.concatenate([eh.imag, eh.real], axis=2)], axis=1)
    t_filt = np.concatenate([e.real, e.imag], axis=1)
    hinv = np.conj(np.transpose(eh, (0, 2, 1))) / n
    t_inv = np.concatenate(
        [np.concatenate([hinv.real, -hinv.imag], axis=2), np.concatenate([hinv.imag, hinv.real], axis=2)], axis=1)
    f2 = np.exp(-2j * np.pi * np.outer(np.arange(n2), np.arange(n2)) / n2)
    as32 = lambda a: np.ascontiguousarray(a, dtype=np.float32)
    return as32(t_data), as32(t_filt), as32(t_inv), as32(_real_form(f2)), as32(_real_form(np.conj(f2)))


def _split_n(n):
    n2 = 1
    while n2 < 128 and n // (n2 * 2) >= 64:
        n2 *= 2
    return n // n2, n2


GS = 16
LANES = 128


def _swap_leading(x):
    return pltpu.einshape("abc->bac", x)


def _fft_s1_kernel(z_ref, t_ref, o_ref, s_ref):
    gs = z_ref.shape[1]
    s_ref[...] = _swap_leading(z_ref[...])
    for g in range(gs):
        o_ref[g] = jnp.dot(t_ref[g], s_ref[g], preferred_element_type=F32).astype(o_ref.dtype)


def _fft_mid_kernel(ar_ref, ai_ref, fr_ref, fi_ref, f2_ref, f2i_ref, o_ref, s_a, s_f):
    n2, gs, _ = ar_ref.shape
    s_a[:, 0:n2, :] = _swap_leading(ar_ref[...])
    s_a[:, n2:2 * n2, :] = _swap_leading(ai_ref[...])
    s_f[:, 0:n2, :] = _swap_leading(fr_ref[...])
    s_f[:, n2:2 * n2, :] = _swap_leading(fi_ref[...])
    f2 = f2_ref[...]
    f2i = f2i_ref[...]

    def spectra(g):
        return (jnp.dot(f2, s_a[g], preferred_element_type=F32),
                jnp.dot(f2, s_f[g], preferred_element_type=F32))

    nxt = spectra(0)
    for g in range(gs):
        u, k = nxt
        if g + 1 < gs:
            nxt = spectra(g + 1)
        ur, ui, kr, ki = u[:n2], u[n2:], k[:n2], k[n2:]
        y = jnp.concatenate([ur * kr - ui * ki, ur * ki + ui * kr], axis=0).astype(BF16)
        o_ref[g] = jnp.dot(f2i, y, preferred_element_type=F32).astype(o_ref.dtype)


def _fft_s3_kernel(cr_ref, ci_ref, t_ref, u_ref, x_ref, bias_ref, o_ref, s_c, s_y):
    n1, gs, _ = cr_ref.shape
    s_c[:, 0:n1, :] = _swap_leading(cr_ref[...])
    s_c[:, n1:2 * n1, :] = _swap_leading(ci_ref[...])
    for g in range(gs):
        s_y[g] = jnp.dot(t_ref[g], s_c[g], preferred_element_type=F32)
    y = _swap_leading(s_y[...])
    u = u_ref[...].astype(F32)
    o_ref[...] = (x_ref[...].astype(F32) * (y + u * bias_ref[...])).astype(o_ref.dtype)


def _hyena_small_kernel(u_ref, k_ref, x_ref, bias_ref, td_ref, tf_ref, ti_ref, _latent_rows_ref, o_ref, *, n):
    u = u_ref[...]
    a = jnp.dot(td_ref[...], u, preferred_element_type=F32)
    k = jnp.dot(tf_ref[...], k_ref[...], preferred_element_type=F32)
    ar, ai, kr, ki = a[:n], a[n:], k[:n], k[n:]
    y = jnp.concatenate([ar * kr - ai * ki, ar * ki + ai * kr], axis=0).astype(BF16)
    y = jnp.dot(ti_ref[...], y, preferred_element_type=F32)
    o_ref[...] = (x_ref[...].astype(F32) * (y + u.astype(F32) * bias_ref[...])).astype(o_ref.dtype)


def _bf16_table(a):
    return jnp.asarray(a).astype(BF16)


def _long_conv_gate(st, u3, u_part, x_arr, x_part, kk, kk_ctx, order, bias):
    assert st.b == 2, "the two batch elements are packed as one complex signal"
    _, rows, width = u3.shape
    l, lc = st.l, st.c
    n = 2 * l
    n1, n2 = _split_n(n)
    t_data, t_filt, t_inv, f2, f2i = (_bf16_table(a) for a in _dft_tables(n1, n2))
    assert n1 % GS == 0 and n2 % GS == 0 and width % LANES == 0
    bias2 = bias.reshape(1, width)
    slabs = rows // n2

    def stage1(z4, part, table, name):
        return pl.pallas_call(
            _fft_s1_kernel,
            grid=(n2 // GS,),
            in_specs=[pl.BlockSpec((None, n1, GS, width), lambda j: (part, 0, j, 0)),
                      pl.BlockSpec((GS, 2 * n1, n1), lambda j: (j, 0, 0))],
            out_specs=pl.BlockSpec((GS, 2 * n1, width), lambda j: (j, 0, 0)),
            out_shape=jax.ShapeDtypeStruct((n2, 2 * n1, width), BF16),
            scratch_shapes=[pltpu.VMEM((GS, n1, width), BF16)],
            compiler_params=_cparams(1),
            name=name,
        )(z4, table)

    a_filt = stage1(kk.reshape(HYENA_ORDER, n1, n2, width), order, t_filt, "fft_stage1_filter")
    u_v = u3.reshape(u3.shape[0], slabs, n2, width)
    a_data = stage1(u_v, u_part, t_data, "fft_stage1_data")

    ctm = _largest_divisor(width, 256, LANES)
    kb_im = n1 // GS
    re_spec = pl.BlockSpec((n2, GS, ctm), lambda c, k: (0, k, c))
    im_spec = pl.BlockSpec((n2, GS, ctm), lambda c, k: (0, kb_im + k, c))
    mat_spec = pl.BlockSpec((2 * n2, 2 * n2), lambda c, k: (0, 0))
    c_mid = pl.pallas_call(
        _fft_mid_kernel,
        grid=(width // ctm, n1 // GS),
        in_specs=[re_spec, im_spec, re_spec, im_spec, mat_spec, mat_spec],
        out_specs=pl.BlockSpec((GS, 2 * n2, ctm), lambda c, k: (k, 0, c)),
        out_shape=jax.ShapeDtypeStruct((n1, 2 * n2, width), BF16),
        scratch_shapes=[pltpu.VMEM((GS, 2 * n2, ctm), BF16)] * 2,
        compiler_params=_cparams(2),
        name="fft_mid",
    )(a_data, a_data, a_filt, a_filt, f2, f2i)

    cts = _largest_divisor(width, 1024, LANES)
    jb_im = n2 // GS
    x_v = x_arr.reshape(x_arr.shape[0], slabs, n2, width)
    out = pl.pallas_call(
        _fft_s3_kernel,
        grid=(width // cts, n2 // GS),
        in_specs=[pl.BlockSpec((n1, GS, cts), lambda c, j: (0, j, c)),
                  pl.BlockSpec((n1, GS, cts), lambda c, j: (0, jb_im + j, c)),
                  pl.BlockSpec((GS, n1, 2 * n1), lambda c, j: (j, 0, 0)),
                  pl.BlockSpec((None, n1, GS, cts), lambda c, j: (u_part, 0, j, c)),
                  pl.BlockSpec((None, n1, GS, cts), lambda c, j: (x_part, 0, j, c)),
                  pl.BlockSpec((1, cts), lambda c, j: (0, c))],
        out_specs=pl.BlockSpec((n1, GS, cts), lambda c, j: (0, j, c)),
        out_shape=jax.ShapeDtypeStruct((slabs, n2, width), BF16),
        scratch_shapes=[pltpu.VMEM((GS, 2 * n1, cts), BF16), pltpu.VMEM((GS, n1, cts), F32)],
        compiler_params=_cparams(2),
        name="fft_stage3",
    )(c_mid, c_mid, t_inv, u_v, x_v, bias2)
    out = out.reshape(rows, width)

    nc = 2 * lc
    tc_data, tc_filt, tc_inv, _, _ = (_bf16_table(a) for a in _dft_tables(nc, 1))
    ct = _largest_divisor(width, 512, 128)
    rb = (st.rows_lat) // (2 * lc)
    out = pl.pallas_call(
        functools.partial(_hyena_small_kernel, n=nc),
        grid=(width // ct,),
        in_specs=[pl.BlockSpec((None, 2 * lc, ct), lambda c: (u_part, rb, c)),
                  pl.BlockSpec((None, nc, ct), lambda c: (order, 0, c)),
                  pl.BlockSpec((None, 2 * lc, ct), lambda c: (x_part, rb, c)),
                  pl.BlockSpec((1, ct), lambda c: (0, c)),
                  pl.BlockSpec((2 * nc, nc), lambda c: (0, 0)),
                  pl.BlockSpec((2 * nc, nc), lambda c: (0, 0)),
                  pl.BlockSpec((nc, 2 * nc), lambda c: (0, 0)),
                  pl.BlockSpec(memory_space=pl.ANY)],
        out_specs=pl.BlockSpec((2 * lc, ct), lambda c: (rb, c)),
        out_shape=jax.ShapeDtypeStruct((rows, width), BF16),
        input_output_aliases={7: 0},
        compiler_params=_cparams(1),
        name="hyena_context",
    )(u3, kk_ctx, x_arr, bias2, tc_data[0], tc_filt[0], tc_inv[0], out)
    return out


@functools.lru_cache(maxsize=None)
def _pool_tables(t, windows):
    out = np.zeros((4, len(windows), t, 3 * t), np.float64)
    for v in range(4):
        lo_bound = t if v & 1 else 0
        hi_bound = 2 * t - 1 if v & 2 else 3 * t - 1
        for g, w in enumerate(windows):
            for r in range(t):
                pos = t + r
                lo = max(pos - w // 2, lo_bound)
                hi = min(pos + w - 1 - w // 2, hi_bound)
                out[v, g, r, lo:hi + 1] = 1.0 / (hi - lo + 1)
                out[v, g, r, pos] -= 1.0
    return out.astype(np.float32)


def _pool_kernel(prev_ref, cur_ref, next_ref, m_ref, w_ref, s_ref, *rest, n_groups, gw):
    o_ref = rest[-1]
    for g in range(n_groups):
        sl = slice(g * gw, (g + 1) * gw)
        u = jnp.concatenate([prev_ref[:, sl], cur_ref[:, sl], next_ref[:, sl]], axis=0)
        d = jnp.dot(m_ref[0, g], u, preferred_element_type=F32)
        y = jnp.dot(d.astype(BF16), w_ref[g], preferred_element_type=F32)
        o_ref[:, sl] = (y * s_ref[:, sl]).astype(o_ref.dtype)


def _pool_call(p, col_block, w_pool_bf, pool_scale3, layer, width, row0, n_seq, seq_len, out_rows, prior):
    t = _largest_divisor(seq_len, 256, 16)
    nblk = seq_len // t
    rb0 = row0 // t
    n_groups = len(POOL_WINDOWS)
    gw = width // n_groups
    tables = _bf16_table(_pool_tables(t, POOL_WINDOWS))

    def blk(delta):
        return lambda s, j: (rb0 + s * nblk + jnp.clip(j + delta, 0, nblk - 1), col_block)

    def variant(s, j):
        return ((j == 0).astype(jnp.int32) + 2 * (j == nblk - 1).astype(jnp.int32), 0, 0, 0)

    in_specs = [pl.BlockSpec((t, width), blk(-1)),
                pl.BlockSpec((t, width), blk(0)),
                pl.BlockSpec((t, width), blk(1)),
                pl.BlockSpec((1, n_groups, t, 3 * t), variant),
                pl.BlockSpec((None, n_groups, gw, gw), lambda s, j: (layer, 0, 0, 0)),
                pl.BlockSpec((None, 1, width), lambda s, j: (layer, 0, 0))]
    args = [p, p, p, tables, w_pool_bf, pool_scale3]
    aliases = {}
    if prior is not None:
        in_specs.append(pl.BlockSpec(memory_space=pl.ANY))
        args.append(prior)
        aliases = {6: 0}
    return pl.pallas_call(
        functools.partial(_pool_kernel, n_groups=n_groups, gw=gw),
        grid=(n_seq, nblk),
        in_specs=in_specs,
        out_specs=pl.BlockSpec((t, width), lambda s, j: (rb0 + s * nblk + j, 0)),
        out_shape=jax.ShapeDtypeStruct((out_rows, width), BF16),
        input_output_aliases=aliases,
        compiler_params=_cparams(2),
        name="pool",
    )(*args)


@functools.lru_cache(maxsize=None)
def _rope_tables(seq_len, grid_w, hd, pad_rows):
    axis = hd // 2
    pos = np.arange(seq_len)
    inv = ROPE_THETA ** (-np.arange(0, axis, 2, dtype=np.float64) / axis)
    ang_r = (pos // grid_w)[:, None] * inv[None]
    ang_c = (pos % grid_w)[:, None] * inv[None]
    cos = np.concatenate([np.cos(ang_r), np.cos(ang_r), np.cos(ang_c), np.cos(ang_c)], axis=1)
    sin = np.concatenate([-np.sin(ang_r), np.sin(ang_r), -np.sin(ang_c), np.sin(ang_c)], axis=1)
    cos = np.concatenate([cos, np.ones((pad_rows, hd))], axis=0)
    sin = np.concatenate([sin, np.zeros((pad_rows, hd))], axis=0)
    return cos.astype(np.float32), sin.astype(np.float32)


def _qk_prep_kernel(q_ref, k_ref, v_ref, qg_ref, kg_ref, cos_ref, sin_ref, qto_ref, ko_ref, vto_ref,
                    *, hd, q_scale):
    cos = cos_ref[...]
    sin = sin_ref[...]
    ones = jnp.ones((hd, hd), BF16)
    src = lax.broadcasted_iota(jnp.int32, (hd, hd), 0)
    dst = lax.broadcasted_iota(jnp.int32, (hd, hd), 1)
    partner_of = jnp.where((dst % (hd // 2)) < (hd // 4), dst + hd // 4, dst - hd // 4)
    swap = jnp.where(src == partner_of, 1.0, 0.0).astype(BF16)

    def normed_rotated(x_ref, h, g_ref, mul):
        x = x_ref[:, h * hd:(h + 1) * hd].astype(F32)
        ss = jnp.dot((x * x).astype(BF16), ones, preferred_element_type=F32)
        y = x * lax.rsqrt(ss * (1.0 / hd) + EPS) * g_ref[...]
        partner = jnp.dot(y.astype(BF16), swap, preferred_element_type=F32)
        return (y * cos + partner * sin) * mul

    for h in range(q_ref.shape[1] // hd):
        qto_ref[h * hd:(h + 1) * hd, :] = normed_rotated(q_ref, h, qg_ref, q_scale).T.astype(qto_ref.dtype)
    for h in range(k_ref.shape[1] // hd):
        sl = slice(h * hd, (h + 1) * hd)
        ko_ref[:, sl] = normed_rotated(k_ref, h, kg_ref, 1.0).astype(ko_ref.dtype)
        vto_ref[sl, :] = v_ref[:, sl].astype(F32).T.astype(vto_ref.dtype)


def _qk_prep(st, p, q_col0, k_col0, aw, kvw, q_gain, k_gain, layer, hd):
    cos_np, sin_np = _rope_tables(st.l, GRID_W, hd, st.tm)
    tab_map = lambda i: (jnp.where(i < st.n_lat_tiles, i % st.tiles_per_seq, st.tiles_per_seq), 0)
    return pl.pallas_call(
        functools.partial(_qk_prep_kernel, hd=hd, q_scale=hd ** -0.5),
        grid=(st.n_tiles,),
        in_specs=[pl.BlockSpec((st.tm, aw), lambda i: (i, q_col0 // aw)),
                  pl.BlockSpec((st.tm, kvw), lambda i: (i, k_col0 // kvw)),
                  pl.BlockSpec((st.tm, kvw), lambda i: (i, k_col0 // kvw + 1)),
                  pl.BlockSpec((None, 1, hd), lambda i: (layer, 0, 0)),
                  pl.BlockSpec((None, 1, hd), lambda i: (layer, 0, 0)),
                  pl.BlockSpec((st.tm, hd), tab_map),
                  pl.BlockSpec((st.tm, hd), tab_map)],
        out_specs=[pl.BlockSpec((aw, st.tm), lambda i: (0, i)),
                   pl.BlockSpec((st.tm, kvw), lambda i: (i, 0)),
                   pl.BlockSpec((kvw, st.tm), lambda i: (0, i))],
        out_shape=[jax.ShapeDtypeStruct((aw, st.rows), BF16),
                   jax.ShapeDtypeStruct((st.rows, kvw), BF16),
                   jax.ShapeDtypeStruct((kvw, st.rows), BF16)],
        compiler_params=_cparams(1),
        name="qk_prep",
    )(p, p, p, q_gain.reshape(-1, 1, hd), k_gain.reshape(-1, 1, hd), jnp.asarray(cos_np), jnp.asarray(sin_np))


ONES_ROWS = 16

def _attn_kernel(*refs, group, hd, tk, n_chunks):
    qt_ref, kc_ref, vtc_ref = refs[:3]
    o_ref, m_s, acc_s, s_a, s_b = refs[-5:]
    tq = qt_ref.shape[1]
    qs = jnp.concatenate([qt_ref[g * hd:(g + 1) * hd, :] for g in range(group)], axis=1)
    m_s[...] = jnp.full(m_s.shape, -1e30, F32)
    acc_s[...] = jnp.zeros(acc_s.shape, F32)

    def scores(k):
        return jnp.dot(k, qs, preferred_element_type=F32)

    def softmax_pv(s, vt):
        vt1 = jnp.concatenate([vt, jnp.ones((ONES_ROWS, vt.shape[1]), BF16)], axis=0)
        m_prev = m_s[...]
        m_new = jnp.maximum(m_prev, jnp.max(s, axis=0, keepdims=True))
        alpha = jnp.exp(m_prev - m_new)
        p = jnp.exp((s - m_new).astype(BF16))
        acc_s[...] = alpha * acc_s[...] + jnp.dot(vt1, p, preferred_element_type=F32)
        m_s[...] = m_new

    s_ctx = scores(kc_ref[...])
    if not n_chunks:
        softmax_pv(s_ctx, vtc_ref[...])
    else:
        kl_ref, vtl_ref = refs[3:5]

        def keys(c):
            return kl_ref[pl.ds(pl.multiple_of(c * tk, tk), tk), :]

        def vals(c):
            return vtl_ref[:, pl.ds(pl.multiple_of(c * tk, tk), tk)]

        first = n_chunks % 2
        if n_chunks - first:
            s_a[...] = scores(keys(first))
        softmax_pv(s_ctx, vtc_ref[...])
        if first:
            softmax_pv(scores(keys(0)), vals(0))
        if n_chunks - first:

            def body(j, carry):
                c = first + 2 * j
                s_b[...] = scores(keys(c + 1))
                softmax_pv(s_a[...], vals(c))
                s_a[...] = scores(keys(jnp.minimum(c + 2, n_chunks - 1)))
                softmax_pv(s_b[...], vals(c + 1))
                return carry

            lax.fori_loop(0, (n_chunks - first) // 2, body, 0)
    o = acc_s[0:hd, :] / acc_s[hd:hd + 1, :]
    for g in range(group):
        o_ref[:, g * hd:(g + 1) * hd] = o[:, g * tq:(g + 1) * tq].T.astype(o_ref.dtype)


def _attention(st, q_t, k, v_t, hd):
    group = N_HEADS // N_KV_HEADS
    gw = group * hd
    aw = N_HEADS * hd
    tq = _largest_divisor(st.l, 1024, LANES)
    tk = _largest_divisor(st.l, 512, LANES)
    qt = st.l // tq
    cb0 = st.rows_lat // st.c

    def scratch(q_rows):
        nq = group * q_rows
        return [pltpu.VMEM((1, nq), F32), pltpu.VMEM((hd + ONES_ROWS, nq), F32),
                pltpu.VMEM((tk, nq), F32), pltpu.VMEM((tk, nq), F32)]

    yc = pl.pallas_call(
        functools.partial(_attn_kernel, group=group, hd=hd, tk=tk, n_chunks=st.l // tk),
        grid=(st.b, N_KV_HEADS, qt),
        in_specs=[pl.BlockSpec((gw, tq), lambda b, h, t: (h, b * qt + t)),
                  pl.BlockSpec((st.c, hd), lambda b, h, t: (cb0 + b, h)),
                  pl.BlockSpec((hd, st.c), lambda b, h, t: (h, cb0 + b)),
                  pl.BlockSpec((st.l, hd), lambda b, h, t: (b, h)),
                  pl.BlockSpec((hd, st.l), lambda b, h, t: (h, b))],
        out_specs=pl.BlockSpec((tq, gw), lambda b, h, t: (b * qt + t, h)),
        out_shape=jax.ShapeDtypeStruct((st.rows, aw), BF16),
        scratch_shapes=scratch(tq),
        compiler_params=_cparams(3),
        name="attention_latent",
    )(q_t, k, v_t, k, v_t)
    return pl.pallas_call(
        functools.partial(_attn_kernel, group=group, hd=hd, tk=tk, n_chunks=0),
        grid=(st.b, N_KV_HEADS),
        in_specs=[pl.BlockSpec((gw, st.c), lambda b, h: (h, cb0 + b)),
                  pl.BlockSpec((st.c, hd), lambda b, h: (cb0 + b, h)),
                  pl.BlockSpec((hd, st.c), lambda b, h: (h, cb0 + b)),
                  pl.BlockSpec(memory_space=pl.ANY)],
        out_specs=pl.BlockSpec((st.c, gw), lambda b, h: (cb0 + b, h)),
        out_shape=jax.ShapeDtypeStruct((st.rows, aw), BF16),
        input_output_aliases={3: 0},
        scratch_shapes=scratch(st.c),
        compiler_params=_cparams(2),
        name="attention_context",
    )(q_t, k, v_t, yc)


def kernel(x, c, ctx, c_ctx, w_mod, b_mod, norm_gain, final_gain, ffn_w_in, ffn_w_out, w_in, b_gate, conv_w, conv_b, filt_w1, filt_b1, filt_w2, filt_b2, filt_w3, filt_b3, filt_w4, filt_freq, hyena_bias, w_pool, pool_scale, q_gain, k_gain, w_up, w_out):
    b, l, d = x.shape
    lc = ctx.shape[1]
    depth = w_mod.shape[0]
    width = hyena_bias.shape[2]
    hd = q_gain.shape[1]
    aw, kvw = N_HEADS * hd, N_KV_HEADS * hd
    a1 = (HYENA_ORDER + 1) * width
    q0 = a1 + width
    k0 = q0 + aw
    v0 = k0 + kvw
    g0 = v0 + kvw
    st = _Stream(b, l, lc)

    xs = jnp.concatenate([x.reshape(b * l, d), ctx.reshape(b * lc, d)], axis=0)
    cvec = jnp.zeros((8, d), F32).at[:b].set(c).at[b].set(c_ctx)
    mods = _modulation(cvec, w_mod, b_mod).reshape(depth, 8, N_MOD, d)[:, :b + 1]
    mods = mods.reshape(depth * (b + 1) * N_MOD, 1, d)
    gains = norm_gain.reshape(depth * 3, 1, d)
    b_gate3 = b_gate.reshape(depth * N_BRANCH, 1, d)
    pool_scale3 = pool_scale.reshape(depth, 1, width)
    w_pool_bf = w_pool.astype(BF16)
    ffn_w_out_bf = ffn_w_out.astype(BF16)
    w_up_bf = w_up.astype(BF16)
    w_out_bf = w_out.astype(BF16)
    h_lat =_filter_mlp(l, filt_w1, filt_b1, filt_w2, filt_b2, filt_w3, filt_b3, filt_freq)
    h_ctx = _filter_mlp(lc, filt_w1, filt_b1, filt_w2, filt_b2, filt_w3, filt_b3, filt_freq)

    for i in range(depth):
        h = _adaln(st, xs, gains, mods, i, 0, 0, 1)
        a = _ffn_in(st, h, ffn_w_in, i, 0)
        xs = _mm_residual(st, a, ffn_w_out_bf, (i, 0), xs, mods, i, 2, 0.5, 512)
        h = _adaln(st, xs, gains, mods, i, 1, 3, 4)
        p = _in_proj(st, h, w_in, i)
        cw, cb = conv_w[i], conv_b[i].reshape(1, -1)
        u3 = _short_conv_call(p, cw, cb, width, 0, st.rows_lat, l, st.rows, None)
        u3 = _short_conv_call(p, cw, cb, width, st.rows_lat, b * lc, lc, st.rows, u3)
        kk = _filter_taps(h_lat, filt_w4, i, l, width)
        kk_ctx = _filter_taps(h_ctx, filt_w4, i, lc, width)
        z = _long_conv_gate(st, u3, 0, u3, 1, kk, kk_ctx, 0, hyena_bias[i, 0])
        ya = _long_conv_gate(st, z[None], 0, u3, 2, kk, kk_ctx, 1, hyena_bias[i, 1])
        yb = _pool_call(p, a1 // width, w_pool_bf, pool_scale3, i, width, 0, b, l, st.rows, None)
        yb = _pool_call(p, a1 // width, w_pool_bf, pool_scale3, i, width, st.rows_lat, b, lc, st.rows, yb)
        q_t, k, v_t = _qk_prep(st, p, q0, k0, aw, kvw, q_gain, k_gain, i, hd)
        yc = _attention(st, q_t, k, v_t, hd)
        merged = _merge(st, ya, yb, yc, p, g0, b_gate3, w_up_bf, i)
        xs = _mm_residual(st, merged, w_out_bf, (i,), xs, mods, i, 5, 1.0, 512)
        h = _adaln(st, xs, gains, mods, i, 2, 6, 7)
        a = _ffn_in(st, h, ffn_w_in, i, 1)
        xs = _mm_residual(st, a, ffn_w_out_bf, (i, 1), xs, mods, i, 8, 0.5, 512)

    out = pl.pallas_call(
        _rmsnorm_kernel,
        grid=(st.n_lat_tiles,),
        in_specs=[pl.BlockSpec((st.tm, d), lambda i: (i, 0)),
                  pl.BlockSpec((1, d), lambda i: (0, 0))],
        out_specs=pl.BlockSpec((st.tm, d), lambda i: (i, 0)),
        out_shape=jax.ShapeDtypeStruct((b * l, d), F32),
        compiler_params=_cparams(1),
        name="final_rmsnorm",
    )(xs, final_gain.reshape(1, d))
    return out.reshape(b, l, d)
```

```python
import functools
import math

import numpy as np
import jax
import jax.numpy as jnp
from jax import lax
from jax.experimental import pallas as pl
from jax.experimental.pallas import tpu as pltpu

F32 = jnp.float32
BF16 = jnp.bfloat16

N_HEADS = 16
N_KV_HEADS = 4
GRID_W = 64
ROPE_THETA = 10000.0
HYENA_ORDER = 2
HYENA_TARGET = 1e-2
HYENA_FAST_DECAY = 0.3
HYENA_SLOW_DECAY = 1.5
POOL_WINDOWS = (2, 4, 8, 16)
N_MOD = 9
N_BRANCH = 3
EPS = 1e-6
HIGHEST = lax.Precision.HIGHEST

VMEM_LIMIT_V7X = 56 * 1024 * 1024


def _cparams(n_axes):
    return pltpu.CompilerParams(
        dimension_semantics=("arbitrary",) * n_axes, vmem_limit_bytes=VMEM_LIMIT_V7X)


def _largest_divisor(total, pref, align):
    if total <= pref:
        return total
    t = (pref // align) * align
    while t > align and total % t:
        t -= align
    assert total % t == 0, (total, pref, align)
    return t


def _sigmoid(v):
    return 0.5 * jnp.tanh(0.5 * v) + 0.5


def _silu(v):
    return v * _sigmoid(v)


def _mod_kernel(c_ref, w_ref, b_ref, o_ref):
    s = _silu(c_ref[...]).astype(BF16)
    w = w_ref[0].astype(BF16)
    o_ref[0] = jnp.dot(s, w, preferred_element_type=F32) + b_ref[0]


def _modulation(cvec, w_mod, b_mod):
    depth, d, nm = w_mod.shape
    tn = _largest_divisor(nm, 1024, 128)
    return pl.pallas_call(
        _mod_kernel,
        grid=(depth, nm // tn),
        in_specs=[pl.BlockSpec((8, d), lambda l, n: (0, 0)),
                  pl.BlockSpec((1, d, tn), lambda l, n: (l, 0, n)),
                  pl.BlockSpec((1, 1, tn), lambda l, n: (l, 0, n))],
        out_specs=pl.BlockSpec((1, 8, tn), lambda l, n: (l, 0, n)),
        out_shape=jax.ShapeDtypeStruct((depth, 8, nm), F32),
        compiler_params=_cparams(2),
        name="modulation",
    )(cvec, w_mod, b_mod.reshape(depth, 1, nm))


def _adaln_kernel(x_ref, g_ref, sh_ref, sc_ref, o_ref):
    x = x_ref[...]
    y = x * lax.rsqrt(jnp.mean(x * x, axis=-1, keepdims=True) + EPS)
    y = y * g_ref[0]
    o_ref[...] = (y * (1.0 + sc_ref[0]) + sh_ref[0]).astype(o_ref.dtype)


def _rmsnorm_kernel(x_ref, g_ref, o_ref):
    x = x_ref[...]
    y = x * lax.rsqrt(jnp.mean(x * x, axis=-1, keepdims=True) + EPS)
    o_ref[...] = y * g_ref[...]


class _Stream:
    def __init__(self, batch, seq, ctx_len):
        self.b, self.l, self.c = batch, seq, ctx_len
        self.rows_lat = batch * seq
        self.rows = batch * (seq + ctx_len)
        self.tm = _largest_divisor(math.gcd(seq, batch * ctx_len), 512, 16)
        self.tiles_per_seq = seq // self.tm
        self.n_lat_tiles = self.rows_lat // self.tm
        self.n_tiles = self.rows // self.tm
        self.tm_mm = _largest_divisor(self.rows, 1088, 16)
        self.n_mm_tiles = self.rows // self.tm_mm

    def group(self, i):
        return jnp.where(i < self.n_lat_tiles, i // self.tiles_per_seq, self.b)

    def per_row(self, tile, tile_rows, group_vals):
        row = tile * tile_rows + lax.broadcasted_iota(jnp.int32, (tile_rows, 1), 0)
        out = group_vals[self.b]
        for g in reversed(range(self.b)):
            out = jnp.where(row < (g + 1) * self.l, group_vals[g], out)
        return out


def _adaln(st, x, gains, mods, layer, j, m_shift, m_scale):
    d = x.shape[1]
    base = layer * (st.b + 1) * N_MOD

    def mod_map(m):
        return lambda i: (base + st.group(i) * N_MOD + m, 0, 0)

    return pl.pallas_call(
        _adaln_kernel,
        grid=(st.n_tiles,),
        in_specs=[pl.BlockSpec((st.tm, d), lambda i: (i, 0)),
                  pl.BlockSpec((1, 1, d), lambda i: (layer * 3 + j, 0, 0)),
                  pl.BlockSpec((1, 1, d), mod_map(m_shift)),
                  pl.BlockSpec((1, 1, d), mod_map(m_scale))],
        out_specs=pl.BlockSpec((st.tm, d), lambda i: (i, 0)),
        out_shape=jax.ShapeDtypeStruct(x.shape, BF16),
        compiler_params=_cparams(1),
        name="adaln",
    )(x, gains, mods, mods)


def _ffn_in_kernel(h_ref, wg_ref, wu_ref, o_ref, wg_s, wu_s):
    @pl.when(pl.program_id(1) == 0)
    def _():
        wg_s[...] = wg_ref[...].astype(BF16)
        wu_s[...] = wu_ref[...].astype(BF16)

    h = h_ref[...]
    g = jnp.dot(h, wg_s[...], preferred_element_type=F32)
    u = jnp.dot(h, wu_s[...], preferred_element_type=F32)
    o_ref[...] = (_silu(g) * u).astype(o_ref.dtype)


def _ffn_in(st, h, ffn_w_in, layer, j):
    d = h.shape[1]
    f = ffn_w_in.shape[3] // 2
    tn = _largest_divisor(f, 512, 128)
    nt = f // tn
    return pl.pallas_call(
        _ffn_in_kernel,
        grid=(nt, st.n_mm_tiles),
        in_specs=[pl.BlockSpec((st.tm_mm, d), lambda n, m: (m, 0)),
                  pl.BlockSpec((None, None, d, tn), lambda n, m: (layer, j, 0, n)),
                  pl.BlockSpec((None, None, d, tn), lambda n, m: (layer, j, 0, n + nt))],
        out_specs=pl.BlockSpec((st.tm_mm, tn), lambda n, m: (m, n)),
        out_shape=jax.ShapeDtypeStruct((st.rows, f), BF16),
        scratch_shapes=[pltpu.VMEM((d, tn), BF16), pltpu.VMEM((d, tn), BF16)],
        compiler_params=_cparams(2),
        name="ffn_in",
    )(h, ffn_w_in, ffn_w_in)


def _mm_res_kernel(a_ref, w_ref, x_ref, *rest, st, gate_scale):
    gate_refs, o_ref = rest[:-1], rest[-1]
    gate = st.per_row(pl.program_id(0), a_ref.shape[0], [g[0] for g in gate_refs])
    y = jnp.dot(a_ref[...], w_ref[...], preferred_element_type=F32)
    o_ref[...] = x_ref[...] + (gate_scale * gate) * y


def _mm_residual(st, a, w_bf, w_index, x, mods, layer, m_gate, gate_scale, tn_pref):
    k = a.shape[1]
    d = x.shape[1]
    tn = _largest_divisor(d, tn_pref, 128)
    base = layer * (st.b + 1) * N_MOD
    lead = (None,) * len(w_index)

    def gate_spec(g):
        return pl.BlockSpec((1, 1, tn), lambda m, n: (base + g * N_MOD + m_gate, 0, n))

    return pl.pallas_call(
        functools.partial(_mm_res_kernel, st=st, gate_scale=gate_scale),
        grid=(st.n_mm_tiles, d // tn),
        in_specs=[pl.BlockSpec((st.tm_mm, k), lambda m, n: (m, 0)),
                  pl.BlockSpec(lead + (k, tn), lambda m, n: tuple(w_index) + (0, n)),
                  pl.BlockSpec((st.tm_mm, tn), lambda m, n: (m, n))]
                 + [gate_spec(g) for g in range(st.b + 1)],
        out_specs=pl.BlockSpec((st.tm_mm, tn), lambda m, n: (m, n)),
        out_shape=jax.ShapeDtypeStruct(x.shape, F32),
        compiler_params=_cparams(2),
        name="matmul_residual",
    )(a, w_bf, x, *([mods] * (st.b + 1)))


def _mm_kernel(h_ref, w_ref, o_ref, w_s):
    @pl.when(pl.program_id(1) == 0)
    def _():
        w_s[...] = w_ref[...].astype(BF16)

    o_ref[...] = jnp.dot(h_ref[...], w_s[...], preferred_element_type=F32).astype(o_ref.dtype)


def _in_proj(st, h, w_in, layer):
    d = h.shape[1]
    n_in = w_in.shape[2]
    tn = _largest_divisor(n_in, 1024, 128)
    return pl.pallas_call(
        _mm_kernel,
        grid=(n_in // tn, st.n_mm_tiles),
        in_specs=[pl.BlockSpec((st.tm_mm, d), lambda n, m: (m, 0)),
                  pl.BlockSpec((None, d, tn), lambda n, m: (layer, 0, n))],
        out_specs=pl.BlockSpec((st.tm_mm, tn), lambda n, m: (m, n)),
        out_shape=jax.ShapeDtypeStruct((st.rows, n_in), BF16),
        scratch_shapes=[pltpu.VMEM((d, tn), BF16)],
        compiler_params=_cparams(2),
        name="in_proj",
    )(h, w_in)


def _merge_kernel(ya_ref, yb_ref, yc_ref, pa_ref, pb_ref, pc_ref, ba_ref, bb_ref, bc_ref, w_ref, o_ref):
    acc = None
    for k, (y_ref, p_ref, b_ref) in enumerate(
            ((ya_ref, pa_ref, ba_ref), (yb_ref, pb_ref, bb_ref), (yc_ref, pc_ref, bc_ref))):
        gate = _sigmoid(p_ref[...].astype(F32) + b_ref[0])
        t = gate * jnp.dot(y_ref[...], w_ref[k], preferred_element_type=F32)
        acc = t if acc is None else acc + t
    o_ref[...] = acc.astype(o_ref.dtype)


def _merge(st, ya, yb, yc, p, gate_col0, b_gate3, w_up_bf, layer):
    w = ya.shape[1]
    d = w_up_bf.shape[3]
    tn = _largest_divisor(d, 512, 128)
    gb = gate_col0 // tn
    dt = d // tn
    tm = _largest_divisor(st.rows, 544, 16)
    y_spec = pl.BlockSpec((tm, w), lambda m, n: (m, 0))

    def p_spec(k):
        return pl.BlockSpec((tm, tn), lambda m, n: (m, gb + k * dt + n))

    def b_spec(k):
        return pl.BlockSpec((1, 1, tn), lambda m, n: (layer * N_BRANCH + k, 0, n))

    return pl.pallas_call(
        _merge_kernel,
        grid=(st.rows // tm, dt),
        in_specs=[y_spec, y_spec, y_spec, p_spec(0), p_spec(1), p_spec(2),
                  b_spec(0), b_spec(1), b_spec(2),
                  pl.BlockSpec((None, N_BRANCH, w, tn), lambda m, n: (layer, 0, 0, n))],
        out_specs=pl.BlockSpec((tm, tn), lambda m, n: (m, n)),
        out_shape=jax.ShapeDtypeStruct((st.rows, d), BF16),
        compiler_params=_cparams(2),
        name="merge",
    )(ya, yb, yc, p, p, p, b_gate3, b_gate3, b_gate3, w_up_bf)


HALO = 16


def _short_conv_kernel(cur_ref, prev_ref, next_ref, w_ref, b_ref, o_ref, *, blocks_per_seq):
    j = pl.program_id(1) % blocks_per_seq
    u = cur_ref[...].astype(F32)
    t = u.shape[0]
    prev_row = jnp.where(j == 0, 0.0, prev_ref[HALO - 1:HALO, :].astype(F32))
    next_row = jnp.where(j == blocks_per_seq - 1, 0.0, next_ref[0:1, :].astype(F32))
    row = lax.broadcasted_iota(jnp.int32, u.shape, 0)
    up = jnp.where(row == 0, prev_row, pltpu.roll(u, 1, axis=0))
    un = jnp.where(row == t - 1, next_row, pltpu.roll(u, t - 1, axis=0))
    w = w_ref[...]
    o_ref[...] = (b_ref[...] + up * w[0:1] + u * w[1:2] + un * w[2:3]).astype(o_ref.dtype)


def _short_conv_call(p, conv_w, conv_b, width, row0, n_rows, seq_len, out_rows, prior):
    ts = _largest_divisor(seq_len, 512, HALO)
    ct = _largest_divisor(width, 1024, 128)
    blocks_per_seq = seq_len // ts
    rb0 = row0 // ts
    hb = ts // HALO
    n_hblocks = out_rows // HALO
    cpb = width // ct
    n_parts = HYENA_ORDER + 1

    def cur_map(c, r):
        return (rb0 + r, c)

    def prev_map(c, r):
        return (jnp.maximum((rb0 + r) * hb - 1, 0), c)

    def next_map(c, r):
        return (jnp.minimum((rb0 + r + 1) * hb, n_hblocks - 1), c)

    in_specs = [pl.BlockSpec((ts, ct), cur_map),
                pl.BlockSpec((HALO, ct), prev_map),
                pl.BlockSpec((HALO, ct), next_map),
                pl.BlockSpec((3, ct), lambda c, r: (0, c)),
                pl.BlockSpec((1, ct), lambda c, r: (0, c))]
    args = [p, p, p, conv_w, conv_b]
    aliases = {}
    kernel = functools.partial(_short_conv_kernel, blocks_per_seq=blocks_per_seq)
    if prior is not None:
        in_specs.append(pl.BlockSpec(memory_space=pl.ANY))
        args.append(prior)
        aliases = {5: 0}
        body = kernel
        kernel = lambda c, pv, nx, w, b, _prior, o: body(c, pv, nx, w, b, o)
    return pl.pallas_call(
        kernel,
        grid=(n_parts * cpb, n_rows // ts),
        in_specs=in_specs,
        out_specs=pl.BlockSpec((None, ts, ct), lambda c, r: (c // cpb, rb0 + r, c % cpb)),
        out_shape=jax.ShapeDtypeStruct((n_parts, out_rows, width), BF16),
        input_output_aliases=aliases,
        compiler_params=_cparams(2),
        name="short_conv",
    )(*args)


@functools.lru_cache(maxsize=None)
def _filter_positions(seq_len, emb_dim, pad_dim):
    bands = (emb_dim - 1) // 2
    j = np.arange(seq_len, dtype=np.float64)
    t = j / (seq_len - 1)
    wpos = 2.0 * np.pi * j / seq_len
    f = np.linspace(1e-4, bands - 1, bands)
    z = np.concatenate([t[:, None], np.cos(f[None] * wpos[:, None]), -np.sin(f[None] * wpos[:, None])], axis=1)
    lag = np.concatenate([np.arange(seq_len), [0], np.arange(seq_len - 1, 0, -1)])
    z2 = np.zeros((2 * seq_len, pad_dim), np.float64)
    z2[:, :emb_dim] = z[lag]
    return z2.astype(np.float32)


def _filter_mlp_kernel(zt_ref, w1t_ref, w2t_ref, w3t_ref, bf_ref, o_ref):
    bf = bf_ref[0]
    h = jnp.dot(w1t_ref[0], zt_ref[...], precision=HIGHEST, preferred_element_type=F32)
    h = jnp.sin(bf[:, 3:4] * (h + bf[:, 0:1]))
    h = jnp.dot(w2t_ref[0], h, precision=HIGHEST, preferred_element_type=F32)
    h = jnp.sin(bf[:, 4:5] * (h + bf[:, 1:2]))
    h = jnp.dot(w3t_ref[0], h, precision=HIGHEST, preferred_element_type=F32)
    o_ref[0] = jnp.sin(bf[:, 5:6] * (h + bf[:, 2:3]))


def _filter_mlp(seq_len, w1, b1, w2, b2, w3, b3, freq):
    depth, emb, hid = w1.shape
    pad = -(-emb // 8) * 8
    n = 2 * seq_len
    zt = jnp.asarray(np.ascontiguousarray(_filter_positions(seq_len, emb, pad).T))
    w1t = jnp.swapaxes(jnp.pad(w1, ((0, 0), (0, pad - emb), (0, 0))), 1, 2)
    bf = jnp.concatenate([b1[:, :, None], b2[:, :, None], b3[:, :, None], jnp.swapaxes(freq, 1, 2)], axis=2)
    tr = _largest_divisor(n, 2048, LANES)
    wspec = lambda k: pl.BlockSpec((1, hid, k), lambda l, r: (l, 0, 0))
    h_t = pl.pallas_call(
        _filter_mlp_kernel,
        grid=(depth, n // tr),
        in_specs=[pl.BlockSpec((pad, tr), lambda l, r: (0, r)),
                  wspec(pad), wspec(hid), wspec(hid),
                  pl.BlockSpec((1, hid, 6), lambda l, r: (l, 0, 0))],
        out_specs=pl.BlockSpec((1, hid, tr), lambda l, r: (l, 0, r)),
        out_shape=jax.ShapeDtypeStruct((depth, hid, n), F32),
        compiler_params=_cparams(2),
        name="filter_mlp",
    )(zt, w1t, jnp.swapaxes(w2, 1, 2), jnp.swapaxes(w3, 1, 2), bf)
    return jnp.swapaxes(h_t, 1, 2)


def _filter_taps_kernel(h_ref, wf_ref, wb_ref, delta_ref, o_ref, *, seq_len):
    l = seq_len
    delta = delta_ref[...]
    lag = lax.broadcasted_iota(jnp.int32, (l, 1), 0)
    t_top = lag.astype(F32) / (l - 1.0)
    t_bot = jnp.where(lag == 0, 0, l - lag).astype(F32) / (l - 1.0)
    top = jnp.dot(h_ref[0, 0:l, :].astype(BF16), wf_ref[0].astype(BF16), preferred_element_type=F32)
    bot = jnp.dot(h_ref[0, l:2 * l, :].astype(BF16), wb_ref[0].astype(BF16), preferred_element_type=F32)
    top = top * jnp.exp(-t_top * delta)
    bot = bot * jnp.exp(-t_bot * delta)
    top = top + jnp.where(lag == 0, bot[0:1, :], 0.0)
    bot = jnp.where(lag == 0, 0.0, bot)
    ss = jnp.sum(top * top, axis=0, keepdims=True) + jnp.sum(bot * bot, axis=0, keepdims=True)
    scale = lax.rsqrt(ss + EPS)
    o_ref[0, 0:l, :] = (top * scale).astype(o_ref.dtype)
    o_ref[0, l:2 * l, :] = (bot * scale).astype(o_ref.dtype)


def _filter_taps(h_all, filt_w4, layer, seq_len, width):
    hid = h_all.shape[2]
    n = 2 * seq_len
    ct = 128
    cpb = width // ct
    min_decay = math.log(HYENA_TARGET) / HYENA_SLOW_DECAY
    max_decay = math.log(HYENA_TARGET) / HYENA_FAST_DECAY
    delta = jnp.asarray(np.abs(np.linspace(min_decay, max_decay, width)).astype(np.float32)).reshape(1, width)
    return pl.pallas_call(
        functools.partial(_filter_taps_kernel, seq_len=seq_len),
        grid=(HYENA_ORDER, cpb),
        in_specs=[pl.BlockSpec((1, n, hid), lambda o, c: (layer, 0, 0)),
                  pl.BlockSpec((1, hid, ct), lambda o, c: (layer, 0, (2 * o) * cpb + c)),
                  pl.BlockSpec((1, hid, ct), lambda o, c: (layer, 0, (2 * o + 1) * cpb + c)),
                  pl.BlockSpec((1, ct), lambda o, c: (0, c))],
        out_specs=pl.BlockSpec((1, n, ct), lambda o, c: (o, 0, c)),
        out_shape=jax.ShapeDtypeStruct((HYENA_ORDER, n, width), BF16),
        compiler_params=_cparams(2),
        name="filter_taps",
    )(h_all, filt_w4, filt_w4, delta)


def _real_form(e):
    return np.block([[e.real, -e.imag], [e.imag, e.real]])


@functools.lru_cache(maxsize=None)
def _dft_tables(n1, n2):
    n = n1 * n2
    j2 = np.arange(n2)[:, None, None]
    k1 = np.arange(n1)[None, :, None]
    j1 = np.arange(n1)[None, None, :]
    e = np.exp(-2j * np.pi * ((j1 * k1) / n1 + (j2 * k1) / n))
    eh = e[:, :, :n1 // 2]
    t_data = np.concatenate(
        [np.concatenate([eh.real, -eh.imag], axis=2), np.concatenate([eh.imag, eh.real], axis=2)], axis=1)
    t_filt = np.concatenate([e.real, e.imag], axis=1)
    hinv = np.conj(np.transpose(eh, (0, 2, 1))) / n
    t_inv = np.concatenate(
        [np.concatenate([hinv.real, -hinv.imag], axis=2), np.concatenate([hinv.imag, hinv.real], axis=2)], axis=1)
    f2 = np.exp(-2j * np.pi * np.outer(np.arange(n2), np.arange(n2)) / n2)
    as32 = lambda a: np.ascontiguousarray(a, dtype=np.float32)
    return as32(t_data), as32(t_filt), as32(t_inv), as32(_real_form(f2)), as32(_real_form(np.conj(f2)))


def _split_n(n):
    n2 = 1
    while n2 < 128 and n // (n2 * 2) >= 64:
        n2 *= 2
    return n // n2, n2


GS = 16
LANES = 128


def _swap_leading(x):
    return pltpu.einshape("abc->bac", x)


def _fft_s1_kernel(z_ref, t_ref, o_ref, s_ref):
    gs = z_ref.shape[1]
    s_ref[...] = _swap_leading(z_ref[...])
    for g in range(gs):
        o_ref[g] = jnp.dot(t_ref[g], s_ref[g], preferred_element_type=F32).astype(o_ref.dtype)


def _fft_mid_kernel(ar_ref, ai_ref, fr_ref, fi_ref, f2_ref, f2i_ref, o_ref, s_a, s_f):
    n2, gs, _ = ar_ref.shape
    s_a[:, 0:n2, :] = _swap_leading(ar_ref[...])
    s_a[:, n2:2 * n2, :] = _swap_leading(ai_ref[...])
    s_f[:, 0:n2, :] = _swap_leading(fr_ref[...])
    s_f[:, n2:2 * n2, :] = _swap_leading(fi_ref[...])
    f2 = f2_ref[...]
    f2i = f2i_ref[...]

    def spectra(g):
        return (jnp.dot(f2, s_a[g], preferred_element_type=F32),
                jnp.dot(f2, s_f[g], preferred_element_type=F32))

    nxt = spectra(0)
    for g in range(gs):
        u, k = nxt
        if g + 1 < gs:
            nxt = spectra(g + 1)
        ur, ui, kr, ki = u[:n2], u[n2:], k[:n2], k[n2:]
        y = jnp.concatenate([ur * kr - ui * ki, ur * ki + ui * kr], axis=0).astype(BF16)
        o_ref[g] = jnp.dot(f2i, y, preferred_element_type=F32).astype(o_ref.dtype)


def _fft_s3_kernel(cr_ref, ci_ref, t_ref, u_ref, x_ref, bias_ref, o_ref, s_c, s_y):
    n1, gs, _ = cr_ref.shape
    s_c[:, 0:n1, :] = _swap_leading(cr_ref[...])
    s_c[:, n1:2 * n1, :] = _swap_leading(ci_ref[...])
    for g in range(gs):
        s_y[g] = jnp.dot(t_ref[g], s_c[g], preferred_element_type=F32)
    y = _swap_leading(s_y[...])
    u = u_ref[...].astype(F32)
    o_ref[...] = (x_ref[...].astype(F32) * (y + u * bias_ref[...])).astype(o_ref.dtype)


def _hyena_small_kernel(u_ref, k_ref, x_ref, bias_ref, td_ref, tf_ref, ti_ref, _latent_rows_ref, o_ref, *, n):
    u = u_ref[...]
    a = jnp.dot(td_ref[...], u, preferred_element_type=F32)
    k = jnp.dot(tf_ref[...], k_ref[...], preferred_element_type=F32)
    ar, ai, kr, ki = a[:n], a[n:], k[:n], k[n:]
    y = jnp.concatenate([ar * kr - ai * ki, ar * ki + ai * kr], axis=0).astype(BF16)
    y = jnp.dot(ti_ref[...], y, preferred_element_type=F32)
    o_ref[...] = (x_ref[...].astype(F32) * (y + u.astype(F32) * bias_ref[...])).astype(o_ref.dtype)


def _bf16_table(a):
    return jnp.asarray(a).astype(BF16)


def _long_conv_gate(st, u3, u_part, x_arr, x_part, kk, kk_ctx, order, bias):
    assert st.b == 2, "the two batch elements are packed as one complex signal"
    _, rows, width = u3.shape
    l, lc = st.l, st.c
    n = 2 * l
    n1, n2 = _split_n(n)
    t_data, t_filt, t_inv, f2, f2i = (_bf16_table(a) for a in _dft_tables(n1, n2))
    assert n1 % GS == 0 and n2 % GS == 0 and width % LANES == 0
    bias2 = bias.reshape(1, width)
    slabs = rows // n2

    def stage1(z4, part, table, name):
        return pl.pallas_call(
            _fft_s1_kernel,
            grid=(n2 // GS,),
            in_specs=[pl.BlockSpec((None, n1, GS, width), lambda j: (part, 0, j, 0)),
                      pl.BlockSpec((GS, 2 * n1, n1), lambda j: (j, 0, 0))],
            out_specs=pl.BlockSpec((GS, 2 * n1, width), lambda j: (j, 0, 0)),
            out_shape=jax.ShapeDtypeStruct((n2, 2 * n1, width), BF16),
            scratch_shapes=[pltpu.VMEM((GS, n1, width), BF16)],
            compiler_params=_cparams(1),
            name=name,
        )(z4, table)

    a_filt = stage1(kk.reshape(HYENA_ORDER, n1, n2, width), order, t_filt, "fft_stage1_filter")
    u_v = u3.reshape(u3.shape[0], slabs, n2, width)
    a_data = stage1(u_v, u_part, t_data, "fft_stage1_data")

    ctm = _largest_divisor(width, 256, LANES)
    kb_im = n1 // GS
    re_spec = pl.BlockSpec((n2, GS, ctm), lambda c, k: (0, k, c))
    im_spec = pl.BlockSpec((n2, GS, ctm), lambda c, k: (0, kb_im + k, c))
    mat_spec = pl.BlockSpec((2 * n2, 2 * n2), lambda c, k: (0, 0))
    c_mid = pl.pallas_call(
        _fft_mid_kernel,
        grid=(width // ctm, n1 // GS),
        in_specs=[re_spec, im_spec, re_spec, im_spec, mat_spec, mat_spec],
        out_specs=pl.BlockSpec((GS, 2 * n2, ctm), lambda c, k: (k, 0, c)),
        out_shape=jax.ShapeDtypeStruct((n1, 2 * n2, width), BF16),
        scratch_shapes=[pltpu.VMEM((GS, 2 * n2, ctm), BF16)] * 2,
        compiler_params=_cparams(2),
        name="fft_mid",
    )(a_data, a_data, a_filt, a_filt, f2, f2i)

    cts = _largest_divisor(width, 1024, LANES)
    jb_im = n2 // GS
    x_v = x_arr.reshape(x_arr.shape[0], slabs, n2, width)
    out = pl.pallas_call(
        _fft_s3_kernel,
        grid=(width // cts, n2 // GS),
        in_specs=[pl.BlockSpec((n1, GS, cts), lambda c, j: (0, j, c)),
                  pl.BlockSpec((n1, GS, cts), lambda c, j: (0, jb_im + j, c)),
                  pl.BlockSpec((GS, n1, 2 * n1), lambda c, j: (j, 0, 0)),
                  pl.BlockSpec((None, n1, GS, cts), lambda c, j: (u_part, 0, j, c)),
                  pl.BlockSpec((None, n1, GS, cts), lambda c, j: (x_part, 0, j, c)),
                  pl.BlockSpec((1, cts), lambda c, j: (0, c))],
        out_specs=pl.BlockSpec((n1, GS, cts), lambda c, j: (0, j, c)),
        out_shape=jax.ShapeDtypeStruct((slabs, n2, width), BF16),
        scratch_shapes=[pltpu.VMEM((GS, 2 * n1, cts), BF16), pltpu.VMEM((GS, n1, cts), F32)],
        compiler_params=_cparams(2),
        name="fft_stage3",
    )(c_mid, c_mid, t_inv, u_v, x_v, bias2)
    out = out.reshape(rows, width)

    nc = 2 * lc
    tc_data, tc_filt, tc_inv, _, _ = (_bf16_table(a) for a in _dft_tables(nc, 1))
    ct = _largest_divisor(width, 512, 128)
    rb = (st.rows_lat) // (2 * lc)
    out = pl.pallas_call(
        functools.partial(_hyena_small_kernel, n=nc),
        grid=(width // ct,),
        in_specs=[pl.BlockSpec((None, 2 * lc, ct), lambda c: (u_part, rb, c)),
                  pl.BlockSpec((None, nc, ct), lambda c: (order, 0, c)),
                  pl.BlockSpec((None, 2 * lc, ct), lambda c: (x_part, rb, c)),
                  pl.BlockSpec((1, ct), lambda c: (0, c)),
                  pl.BlockSpec((2 * nc, nc), lambda c: (0, 0)),
                  pl.BlockSpec((2 * nc, nc), lambda c: (0, 0)),
                  pl.BlockSpec((nc, 2 * nc), lambda c: (0, 0)),
                  pl.BlockSpec(memory_space=pl.ANY)],
        out_specs=pl.BlockSpec((2 * lc, ct), lambda c: (rb, c)),
        out_shape=jax.ShapeDtypeStruct((rows, width), BF16),
        input_output_aliases={7: 0},
        compiler_params=_cparams(1),
        name="hyena_context",
    )(u3, kk_ctx, x_arr, bias2, tc_data[0], tc_filt[0], tc_inv[0], out)
    return out


@functools.lru_cache(maxsize=None)
def _pool_tables(t, windows):
    out = np.zeros((4, len(windows), t, 3 * t), np.float64)
    for v in range(4):
        lo_bound = t if v & 1 else 0
        hi_bound = 2 * t - 1 if v & 2 else 3 * t - 1
        for g, w in enumerate(windows):
            for r in range(t):
                pos = t + r
                lo = max(pos - w // 2, lo_bound)
                hi = min(pos + w - 1 - w // 2, hi_bound)
                out[v, g, r, lo:hi + 1] = 1.0 / (hi - lo + 1)
                out[v, g, r, pos] -= 1.0
    return out.astype(np.float32)


def _pool_kernel(prev_ref, cur_ref, next_ref, m_ref, w_ref, s_ref, *rest, n_groups, gw):
    o_ref = rest[-1]
    for g in range(n_groups):
        sl = slice(g * gw, (g + 1) * gw)
        u = jnp.concatenate([prev_ref[:, sl], cur_ref[:, sl], next_ref[:, sl]], axis=0)
        d = jnp.dot(m_ref[0, g], u, preferred_element_type=F32)
        y = jnp.dot(d.astype(BF16), w_ref[g], preferred_element_type=F32)
        o_ref[:, sl] = (y * s_ref[:, sl]).astype(o_ref.dtype)


def _pool_call(p, col_block, w_pool_bf, pool_scale3, layer, width, row0, n_seq, seq_len, out_rows, prior):
    t = _largest_divisor(seq_len, 256, 16)
    nblk = seq_len // t
    rb0 = row0 // t
    n_groups = len(POOL_WINDOWS)
    gw = width // n_groups
    tables = _bf16_table(_pool_tables(t, POOL_WINDOWS))

    def blk(delta):
        return lambda s, j: (rb0 + s * nblk + jnp.clip(j + delta, 0, nblk - 1), col_block)

    def variant(s, j):
        return ((j == 0).astype(jnp.int32) + 2 * (j == nblk - 1).astype(jnp.int32), 0, 0, 0)

    in_specs = [pl.BlockSpec((t, width), blk(-1)),
                pl.BlockSpec((t, width), blk(0)),
                pl.BlockSpec((t, width), blk(1)),
                pl.BlockSpec((1, n_groups, t, 3 * t), variant),
                pl.BlockSpec((None, n_groups, gw, gw), lambda s, j: (layer, 0, 0, 0)),
                pl.BlockSpec((None, 1, width), lambda s, j: (layer, 0, 0))]
    args = [p, p, p, tables, w_pool_bf, pool_scale3]
    aliases = {}
    if prior is not None:
        in_specs.append(pl.BlockSpec(memory_space=pl.ANY))
        args.append(prior)
        aliases = {6: 0}
    return pl.pallas_call(
        functools.partial(_pool_kernel, n_groups=n_groups, gw=gw),
        grid=(n_seq, nblk),
        in_specs=in_specs,
        out_specs=pl.BlockSpec((t, width), lambda s, j: (rb0 + s * nblk + j, 0)),
        out_shape=jax.ShapeDtypeStruct((out_rows, width), BF16),
        input_output_aliases=aliases,
        compiler_params=_cparams(2),
        name="pool",
    )(*args)


@functools.lru_cache(maxsize=None)
def _rope_tables(seq_len, grid_w, hd, pad_rows):
    axis = hd // 2
    pos = np.arange(seq_len)
    inv = ROPE_THETA ** (-np.arange(0, axis, 2, dtype=np.float64) / axis)
    ang_r = (pos // grid_w)[:, None] * inv[None]
    ang_c = (pos % grid_w)[:, None] * inv[None]
    cos = np.concatenate([np.cos(ang_r), np.cos(ang_r), np.cos(ang_c), np.cos(ang_c)], axis=1)
    sin = np.concatenate([-np.sin(ang_r), np.sin(ang_r), -np.sin(ang_c), np.sin(ang_c)], axis=1)
    cos = np.concatenate([cos, np.ones((pad_rows, hd))], axis=0)
    sin = np.concatenate([sin, np.zeros((pad_rows, hd))], axis=0)
    return cos.astype(np.float32), sin.astype(np.float32)


def _qk_prep_kernel(q_ref, k_ref, v_ref, qg_ref, kg_ref, cos_ref, sin_ref, qto_ref, ko_ref, vto_ref,
                    *, hd, q_scale):
    cos = cos_ref[...]
    sin = sin_ref[...]
    ones = jnp.ones((hd, hd), BF16)
    src = lax.broadcasted_iota(jnp.int32, (hd, hd), 0)
    dst = lax.broadcasted_iota(jnp.int32, (hd, hd), 1)
    partner_of = jnp.where((dst % (hd // 2)) < (hd // 4), dst + hd // 4, dst - hd // 4)
    swap = jnp.where(src == partner_of, 1.0, 0.0).astype(BF16)

    def normed_rotated(x_ref, h, g_ref, mul):
        x = x_ref[:, h * hd:(h + 1) * hd].astype(F32)
        ss = jnp.dot((x * x).astype(BF16), ones, preferred_element_type=F32)
        y = x * lax.rsqrt(ss * (1.0 / hd) + EPS) * g_ref[...]
        partner = jnp.dot(y.astype(BF16), swap, preferred_element_type=F32)
        return (y * cos + partner * sin) * mul

    for h in range(q_ref.shape[1] // hd):
        qto_ref[h * hd:(h + 1) * hd, :] = normed_rotated(q_ref, h, qg_ref, q_scale).T.astype(qto_ref.dtype)
    for h in range(k_ref.shape[1] // hd):
        sl = slice(h * hd, (h + 1) * hd)
        ko_ref[:, sl] = normed_rotated(k_ref, h, kg_ref, 1.0).astype(ko_ref.dtype)
        vto_ref[sl, :] = v_ref[:, sl].astype(F32).T.astype(vto_ref.dtype)


def _qk_prep(st, p, q_col0, k_col0, aw, kvw, q_gain, k_gain, layer, hd):
    cos_np, sin_np = _rope_tables(st.l, GRID_W, hd, st.tm)
    tab_map = lambda i: (jnp.where(i < st.n_lat_tiles, i % st.tiles_per_seq, st.tiles_per_seq), 0)
    return pl.pallas_call(
        functools.partial(_qk_prep_kernel, hd=hd, q_scale=hd ** -0.5),
        grid=(st.n_tiles,),
        in_specs=[pl.BlockSpec((st.tm, aw), lambda i: (i, q_col0 // aw)),
                  pl.BlockSpec((st.tm, kvw), lambda i: (i, k_col0 // kvw)),
                  pl.BlockSpec((st.tm, kvw), lambda i: (i, k_col0 // kvw + 1)),
                  pl.BlockSpec((None, 1, hd), lambda i: (layer, 0, 0)),
                  pl.BlockSpec((None, 1, hd), lambda i: (layer, 0, 0)),
                  pl.BlockSpec((st.tm, hd), tab_map),
                  pl.BlockSpec((st.tm, hd), tab_map)],
        out_specs=[pl.BlockSpec((aw, st.tm), lambda i: (0, i)),
                   pl.BlockSpec((st.tm, kvw), lambda i: (i, 0)),
                   pl.BlockSpec((kvw, st.tm), lambda i: (0, i))],
        out_shape=[jax.ShapeDtypeStruct((aw, st.rows), BF16),
                   jax.ShapeDtypeStruct((st.rows, kvw), BF16),
                   jax.ShapeDtypeStruct((kvw, st.rows), BF16)],
        compiler_params=_cparams(1),
        name="qk_prep",
    )(p, p, p, q_gain.reshape(-1, 1, hd), k_gain.reshape(-1, 1, hd), jnp.asarray(cos_np), jnp.asarray(sin_np))


ONES_ROWS = 16

def _attn_kernel(*refs, group, hd, tk, n_chunks):
    qt_ref, kc_ref, vtc_ref = refs[:3]
    o_ref, m_s, acc_s, s_a, s_b = refs[-5:]
    tq = qt_ref.shape[1]
    qs = jnp.concatenate([qt_ref[g * hd:(g + 1) * hd, :] for g in range(group)], axis=1)
    m_s[...] = jnp.full(m_s.shape, -1e30, F32)
    acc_s[...] = jnp.zeros(acc_s.shape, F32)

    def scores(k):
        return jnp.dot(k, qs, preferred_element_type=F32)

    def softmax_pv(s, vt):
        vt1 = jnp.concatenate([vt, jnp.ones((ONES_ROWS, vt.shape[1]), BF16)], axis=0)
        m_prev = m_s[...]
        m_new = jnp.maximum(m_prev, jnp.max(s, axis=0, keepdims=True))
        alpha = jnp.exp(m_prev - m_new)
        p = jnp.exp((s - m_new).astype(BF16))
        acc_s[...] = alpha * acc_s[...] + jnp.dot(vt1, p, preferred_element_type=F32)
        m_s[...] = m_new

    s_ctx = scores(kc_ref[...])
    if not n_chunks:
        softmax_pv(s_ctx, vtc_ref[...])
    else:
        kl_ref, vtl_ref = refs[3:5]

        def keys(c):
            return kl_ref[pl.ds(pl.multiple_of(c * tk, tk), tk), :]

        def vals(c):
            return vtl_ref[:, pl.ds(pl.multiple_of(c * tk, tk), tk)]

        first = n_chunks % 2
        if n_chunks - first:
            s_a[...] = scores(keys(first))
        softmax_pv(s_ctx, vtc_ref[...])
        if first:
            softmax_pv(scores(keys(0)), vals(0))
        if n_chunks - first:

            def body(j, carry):
                c = first + 2 * j
                s_b[...] = scores(keys(c + 1))
                softmax_pv(s_a[...], vals(c))
                s_a[...] = scores(keys(jnp.minimum(c + 2, n_chunks - 1)))
                softmax_pv(s_b[...], vals(c + 1))
                return carry

            lax.fori_loop(0, (n_chunks - first) // 2, body, 0)
    o = acc_s[0:hd, :] / acc_s[hd:hd + 1, :]
    for g in range(group):
        o_ref[:, g * hd:(g + 1) * hd] = o[:, g * tq:(g + 1) * tq].T.astype(o_ref.dtype)


def _attention(st, q_t, k, v_t, hd):
    group = N_HEADS // N_KV_HEADS
    gw = group * hd
    aw = N_HEADS * hd
    tq = _largest_divisor(st.l, 2048, LANES)
    tk = _largest_divisor(st.l, 512, LANES)
    qt = st.l // tq
    cb0 = st.rows_lat // st.c

    def scratch(q_rows):
        nq = group * q_rows
        return [pltpu.VMEM((1, nq), F32), pltpu.VMEM((hd + ONES_ROWS, nq), F32),
                pltpu.VMEM((tk, nq), F32), pltpu.VMEM((tk, nq), F32)]

    yc = pl.pallas_call(
        functools.partial(_attn_kernel, group=group, hd=hd, tk=tk, n_chunks=st.l // tk),
        grid=(st.b, N_KV_HEADS, qt),
        in_specs=[pl.BlockSpec((gw, tq), lambda b, h, t: (h, b * qt + t)),
                  pl.BlockSpec((st.c, hd), lambda b, h, t: (cb0 + b, h)),
                  pl.BlockSpec((hd, st.c), lambda b, h, t: (h, cb0 + b)),
                  pl.BlockSpec((st.l, hd), lambda b, h, t: (b, h)),
                  pl.BlockSpec((hd, st.l), lambda b, h, t: (h, b))],
        out_specs=pl.BlockSpec((tq, gw), lambda b, h, t: (b * qt + t, h)),
        out_shape=jax.ShapeDtypeStruct((st.rows, aw), BF16),
        scratch_shapes=scratch(tq),
        compiler_params=_cparams(3),
        name="attention_latent",
    )(q_t, k, v_t, k, v_t)
    return pl.pallas_call(
        functools.partial(_attn_kernel, group=group, hd=hd, tk=tk, n_chunks=0),
        grid=(st.b, N_KV_HEADS),
        in_specs=[pl.BlockSpec((gw, st.c), lambda b, h: (h, cb0 + b)),
                  pl.BlockSpec((st.c, hd), lambda b, h: (cb0 + b, h)),
                  pl.BlockSpec((hd, st.c), lambda b, h: (h, cb0 + b)),
                  pl.BlockSpec(memory_space=pl.ANY)],
        out_specs=pl.BlockSpec((st.c, gw), lambda b, h: (cb0 + b, h)),
        out_shape=jax.ShapeDtypeStruct((st.rows, aw), BF16),
        input_output_aliases={3: 0},
        scratch_shapes=scratch(st.c),
        compiler_params=_cparams(2),
        name="attention_context",
    )(q_t, k, v_t, yc)


def kernel(x, c, ctx, c_ctx, w_mod, b_mod, norm_gain, final_gain, ffn_w_in, ffn_w_out, w_in, b_gate, conv_w, conv_b, filt_w1, filt_b1, filt_w2, filt_b2, filt_w3, filt_b3, filt_w4, filt_freq, hyena_bias, w_pool, pool_scale, q_gain, k_gain, w_up, w_out):
    b, l, d = x.shape
    lc = ctx.shape[1]
    depth = w_mod.shape[0]
    width = hyena_bias.shape[2]
    hd = q_gain.shape[1]
    aw, kvw = N_HEADS * hd, N_KV_HEADS * hd
    a1 = (HYENA_ORDER + 1) * width
    q0 = a1 + width
    k0 = q0 + aw
    v0 = k0 + kvw
    g0 = v0 + kvw
    st = _Stream(b, l, lc)

    xs = jnp.concatenate([x.reshape(b * l, d), ctx.reshape(b * lc, d)], axis=0)
    cvec = jnp.zeros((8, d), F32).at[:b].set(c).at[b].set(c_ctx)
    mods = _modulation(cvec, w_mod, b_mod).reshape(depth, 8, N_MOD, d)[:, :b + 1]
    mods = mods.reshape(depth * (b + 1) * N_MOD, 1, d)
    gains = norm_gain.reshape(depth * 3, 1, d)
    b_gate3 = b_gate.reshape(depth * N_BRANCH, 1, d)
    pool_scale3 = pool_scale.reshape(depth, 1, width)
    w_pool_bf = w_pool.astype(BF16)
    ffn_w_out_bf = ffn_w_out.astype(BF16)
    w_up_bf = w_up.astype(BF16)
    w_out_bf = w_out.astype(BF16)
    h_lat =_filter_mlp(l, filt_w1, filt_b1, filt_w2, filt_b2, filt_w3, filt_b3, filt_freq)
    h_ctx = _filter_mlp(lc, filt_w1, filt_b1, filt_w2, filt_b2, filt_w3, filt_b3, filt_freq)

    for i in range(depth):
        h = _adaln(st, xs, gains, mods, i, 0, 0, 1)
        a = _ffn_in(st, h, ffn_w_in, i, 0)
        xs = _mm_residual(st, a, ffn_w_out_bf, (i, 0), xs, mods, i, 2, 0.5, 512)
        h = _adaln(st, xs, gains, mods, i, 1, 3, 4)
        p = _in_proj(st, h, w_in, i)
        cw, cb = conv_w[i], conv_b[i].reshape(1, -1)
        u3 = _short_conv_call(p, cw, cb, width, 0, st.rows_lat, l, st.rows, None)
        u3 = _short_conv_call(p, cw, cb, width, st.rows_lat, b * lc, lc, st.rows, u3)
        kk = _filter_taps(h_lat, filt_w4, i, l, width)
        kk_ctx = _filter_taps(h_ctx, filt_w4, i, lc, width)
        z = _long_conv_gate(st, u3, 0, u3, 1, kk, kk_ctx, 0, hyena_bias[i, 0])
        ya = _long_conv_gate(st, z[None], 0, u3, 2, kk, kk_ctx, 1, hyena_bias[i, 1])
        yb = _pool_call(p, a1 // width, w_pool_bf, pool_scale3, i, width, 0, b, l, st.rows, None)
        yb = _pool_call(p, a1 // width, w_pool_bf, pool_scale3, i, width, st.rows_lat, b, lc, st.rows, yb)
        q_t, k, v_t = _qk_prep(st, p, q0, k0, aw, kvw, q_gain, k_gain, i, hd)
        yc = _attention(st, q_t, k, v_t, hd)
        merged = _merge(st, ya, yb, yc, p, g0, b_gate3, w_up_bf, i)
        xs = _mm_residual(st, merged, w_out_bf, (i,), xs, mods, i, 5, 1.0, 512)
        h = _adaln(st, xs, gains, mods, i, 2, 6, 7)
        a = _ffn_in(st, h, ffn_w_in, i, 1)
        xs = _mm_residual(st, a, ffn_w_out_bf, (i, 1), xs, mods, i, 8, 0.5, 512)

    out = pl.pallas_call(
        _rmsnorm_kernel,
        grid=(st.n_lat_tiles,),
        in_specs=[pl.BlockSpec((st.tm, d), lambda i: (i, 0)),
                  pl.BlockSpec((1, d), lambda i: (0, 0))],
        out_specs=pl.BlockSpec((st.tm, d), lambda i: (i, 0)),
        out_shape=jax.ShapeDtypeStruct((b * l, d), F32),
        compiler_params=_cparams(1),
        name="final_rmsnorm",
    )(xs, final_gain.reshape(1, d))
    return out.reshape(b, l, d)
```

```python
import functools
import math

import numpy as np
import jax
import jax.numpy as jnp
from jax import lax
from jax.experimental import pallas as pl
from jax.experimental.pallas import tpu as pltpu

F32 = jnp.float32
BF16 = jnp.bfloat16

N_HEADS = 16
N_KV_HEADS = 4
GRID_W = 64
ROPE_THETA = 10000.0
HYENA_ORDER = 2
HYENA_TARGET = 1e-2
HYENA_FAST_DECAY = 0.3
HYENA_SLOW_DECAY = 1.5
POOL_WINDOWS = (2, 4, 8, 16)
N_MOD = 9
N_BRANCH = 3
EPS = 1e-6
HIGHEST = lax.Precision.HIGHEST

VMEM_LIMIT_V7X = 56 * 1024 * 1024


def _cparams(n_axes):
    return pltpu.CompilerParams(
        dimension_semantics=("arbitrary",) * n_axes, vmem_limit_bytes=VMEM_LIMIT_V7X)


def _largest_divisor(total, pref, align):
    if total <= pref:
        return total
    t = (pref // align) * align
    while t > align and total % t:
        t -= align
    assert total % t == 0, (total, pref, align)
    return t


def _sigmoid(v):
    return 0.5 * jnp.tanh(0.5 * v) + 0.5


def _silu(v):
    return v * _sigmoid(v)


def _mod_kernel(c_ref, w_ref, b_ref, o_ref):
    s = _silu(c_ref[...]).astype(BF16)
    w = w_ref[0].astype(BF16)
    o_ref[0] = jnp.dot(s, w, preferred_element_type=F32) + b_ref[0]


def _modulation(cvec, w_mod, b_mod):
    depth, d, nm = w_mod.shape
    tn = _largest_divisor(nm, 1024, 128)
    return pl.pallas_call(
        _mod_kernel,
        grid=(depth, nm // tn),
        in_specs=[pl.BlockSpec((8, d), lambda l, n: (0, 0)),
                  pl.BlockSpec((1, d, tn), lambda l, n: (l, 0, n)),
                  pl.BlockSpec((1, 1, tn), lambda l, n: (l, 0, n))],
        out_specs=pl.BlockSpec((1, 8, tn), lambda l, n: (l, 0, n)),
        out_shape=jax.ShapeDtypeStruct((depth, 8, nm), F32),
        compiler_params=_cparams(2),
        name="modulation",
    )(cvec, w_mod, b_mod.reshape(depth, 1, nm))


def _adaln_kernel(x_ref, g_ref, sh_ref, sc_ref, o_ref):
    x = x_ref[...]
    y = x * lax.rsqrt(jnp.mean(x * x, axis=-1, keepdims=True) + EPS)
    y = y * g_ref[0]
    o_ref[...] = (y * (1.0 + sc_ref[0]) + sh_ref[0]).astype(o_ref.dtype)


def _rmsnorm_kernel(x_ref, g_ref, o_ref):
    x = x_ref[...]
    y = x * lax.rsqrt(jnp.mean(x * x, axis=-1, keepdims=True) + EPS)
    o_ref[...] = y * g_ref[...]


class _Stream:
    def __init__(self, batch, seq, ctx_len):
        self.b, self.l, self.c = batch, seq, ctx_len
        self.rows_lat = batch * seq
        self.rows = batch * (seq + ctx_len)
        self.tm = _largest_divisor(math.gcd(seq, batch * ctx_len), 512, 16)
        self.tiles_per_seq = seq // self.tm
        self.n_lat_tiles = self.rows_lat // self.tm
        self.n_tiles = self.rows // self.tm
        self.tm_mm = _largest_divisor(self.rows, 1088, 16)
        self.n_mm_tiles = self.rows // self.tm_mm

    def group(self, i):
        return jnp.where(i < self.n_lat_tiles, i // self.tiles_per_seq, self.b)

    def per_row(self, tile, tile_rows, group_vals):
        row = tile * tile_rows + lax.broadcasted_iota(jnp.int32, (tile_rows, 1), 0)
        out = group_vals[self.b]
        for g in reversed(range(self.b)):
            out = jnp.where(row < (g + 1) * self.l, group_vals[g], out)
        return out


def _adaln(st, x, gains, mods, layer, j, m_shift, m_scale):
    d = x.shape[1]
    base = layer * (st.b + 1) * N_MOD

    def mod_map(m):
        return lambda i: (base + st.group(i) * N_MOD + m, 0, 0)

    return pl.pallas_call(
        _adaln_kernel,
        grid=(st.n_tiles,),
        in_specs=[pl.BlockSpec((st.tm, d), lambda i: (i, 0)),
                  pl.BlockSpec((1, 1, d), lambda i: (layer * 3 + j, 0, 0)),
                  pl.BlockSpec((1, 1, d), mod_map(m_shift)),
                  pl.BlockSpec((1, 1, d), mod_map(m_scale))],
        out_specs=pl.BlockSpec((st.tm, d), lambda i: (i, 0)),
        out_shape=jax.ShapeDtypeStruct(x.shape, BF16),
        compiler_params=_cparams(1),
        name="adaln",
    )(x, gains, mods, mods)


def _ffn_in_kernel(h_ref, wg_ref, wu_ref, o_ref, wg_s, wu_s):
    @pl.when(pl.program_id(1) == 0)
    def _():
        wg_s[...] = wg_ref[...].astype(BF16)
        wu_s[...] = wu_ref[...].astype(BF16)

    h = h_ref[...]
    g = jnp.dot(h, wg_s[...], preferred_element_type=F32)
    u = jnp.dot(h, wu_s[...], preferred_element_type=F32)
    o_ref[...] = (_silu(g) * u).astype(o_ref.dtype)


def _ffn_in(st, h, ffn_w_in, layer, j):
    d = h.shape[1]
    f = ffn_w_in.shape[3] // 2
    tn = _largest_divisor(f, 256, 128)
    nt = f // tn
    tm = _largest_divisor(st.rows, 2176, 16)
    return pl.pallas_call(
        _ffn_in_kernel,
        grid=(nt, st.rows // tm),
        in_specs=[pl.BlockSpec((tm, d), lambda n, m: (m, 0)),
                  pl.BlockSpec((None, None, d, tn), lambda n, m: (layer, j, 0, n)),
                  pl.BlockSpec((None, None, d, tn), lambda n, m: (layer, j, 0, n + nt))],
        out_specs=pl.BlockSpec((tm, tn), lambda n, m: (m, n)),
        out_shape=jax.ShapeDtypeStruct((st.rows, f), BF16),
        scratch_shapes=[pltpu.VMEM((d, tn), BF16), pltpu.VMEM((d, tn), BF16)],
        compiler_params=_cparams(2),
        name="ffn_in",
    )(h, ffn_w_in, ffn_w_in)


def _mm_res_kernel(a_ref, w_ref, x_ref, *rest, st, gate_scale):
    gate_refs, o_ref = rest[:-1], rest[-1]
    gate = st.per_row(pl.program_id(0), a_ref.shape[0], [g[0] for g in gate_refs])
    y = jnp.dot(a_ref[...], w_ref[...], preferred_element_type=F32)
    o_ref[...] = x_ref[...] + (gate_scale * gate) * y


def _mm_residual(st, a, w_bf, w_index, x, mods, layer, m_gate, gate_scale, tn_pref):
    k = a.shape[1]
    d = x.shape[1]
    tn = _largest_divisor(d, tn_pref, 128)
    base = layer * (st.b + 1) * N_MOD
    lead = (None,) * len(w_index)

    def gate_spec(g):
        return pl.BlockSpec((1, 1, tn), lambda m, n: (base + g * N_MOD + m_gate, 0, n))

    return pl.pallas_call(
        functools.partial(_mm_res_kernel, st=st, gate_scale=gate_scale),
        grid=(st.n_mm_tiles, d // tn),
        in_specs=[pl.BlockSpec((st.tm_mm, k), lambda m, n: (m, 0)),
                  pl.BlockSpec(lead + (k, tn), lambda m, n: tuple(w_index) + (0, n)),
                  pl.BlockSpec((st.tm_mm, tn), lambda m, n: (m, n))]
                 + [gate_spec(g) for g in range(st.b + 1)],
        out_specs=pl.BlockSpec((st.tm_mm, tn), lambda m, n: (m, n)),
        out_shape=jax.ShapeDtypeStruct(x.shape, F32),
        compiler_params=_cparams(2),
        name="matmul_residual",
    )(a, w_bf, x, *([mods] * (st.b + 1)))


def _mm_kernel(h_ref, w_ref, o_ref, w_s):
    @pl.when(pl.program_id(1) == 0)
    def _():
        w_s[...] = w_ref[...].astype(BF16)

    o_ref[...] = jnp.dot(h_ref[...], w_s[...], preferred_element_type=F32).astype(o_ref.dtype)


def _in_proj(st, h, w_in, layer):
    d = h.shape[1]
    n_in = w_in.shape[2]
    tn = _largest_divisor(n_in, 512, 128)
    tm = _largest_divisor(st.rows, 2176, 16)
    return pl.pallas_call(
        _mm_kernel,
        grid=(n_in // tn, st.rows // tm),
        in_specs=[pl.BlockSpec((tm, d), lambda n, m: (m, 0)),
                  pl.BlockSpec((None, d, tn), lambda n, m: (layer, 0, n))],
        out_specs=pl.BlockSpec((tm, tn), lambda n, m: (m, n)),
        out_shape=jax.ShapeDtypeStruct((st.rows, n_in), BF16),
        scratch_shapes=[pltpu.VMEM((d, tn), BF16)],
        compiler_params=_cparams(2),
        name="in_proj",
    )(h, w_in)


def _merge_kernel(ya_ref, yb_ref, yc_ref, pa_ref, pb_ref, pc_ref, ba_ref, bb_ref, bc_ref, w_ref, o_ref):
    acc = None
    for k, (y_ref, p_ref, b_ref) in enumerate(
            ((ya_ref, pa_ref, ba_ref), (yb_ref, pb_ref, bb_ref), (yc_ref, pc_ref, bc_ref))):
        gate = _sigmoid(p_ref[...].astype(F32) + b_ref[0])
        t = gate * jnp.dot(y_ref[...], w_ref[k], preferred_element_type=F32)
        acc = t if acc is None else acc + t
    o_ref[...] = acc.astype(o_ref.dtype)


def _merge(st, ya, yb, yc, p, gate_col0, b_gate3, w_up_bf, layer):
    w = ya.shape[1]
    d = w_up_bf.shape[3]
    tn = _largest_divisor(d, 512, 128)
    gb = gate_col0 // tn
    dt = d // tn
    tm = _largest_divisor(st.rows, 544, 16)
    y_spec = pl.BlockSpec((tm, w), lambda m, n: (m, 0))

    def p_spec(k):
        return pl.BlockSpec((tm, tn), lambda m, n: (m, gb + k * dt + n))

    def b_spec(k):
        return pl.BlockSpec((1, 1, tn), lambda m, n: (layer * N_BRANCH + k, 0, n))

    return pl.pallas_call(
        _merge_kernel,
        grid=(st.rows // tm, dt),
        in_specs=[y_spec, y_spec, y_spec, p_spec(0), p_spec(1), p_spec(2),
                  b_spec(0), b_spec(1), b_spec(2),
                  pl.BlockSpec((None, N_BRANCH, w, tn), lambda m, n: (layer, 0, 0, n))],
        out_specs=pl.BlockSpec((tm, tn), lambda m, n: (m, n)),
        out_shape=jax.ShapeDtypeStruct((st.rows, d), BF16),
        compiler_params=_cparams(2),
        name="merge",
    )(ya, yb, yc, p, p, p, b_gate3, b_gate3, b_gate3, w_up_bf)


HALO = 16


def _short_conv_kernel(cur_ref, prev_ref, next_ref, w_ref, b_ref, o_ref, *, blocks_per_seq):
    j = pl.program_id(1) % blocks_per_seq
    u = cur_ref[...].astype(F32)
    t = u.shape[0]
    prev_row = jnp.where(j == 0, 0.0, prev_ref[HALO - 1:HALO, :].astype(F32))
    next_row = jnp.where(j == blocks_per_seq - 1, 0.0, next_ref[0:1, :].astype(F32))
    row = lax.broadcasted_iota(jnp.int32, u.shape, 0)
    up = jnp.where(row == 0, prev_row, pltpu.roll(u, 1, axis=0))
    un = jnp.where(row == t - 1, next_row, pltpu.roll(u, t - 1, axis=0))
    w = w_ref[...]
    o_ref[...] = (b_ref[...] + up * w[0:1] + u * w[1:2] + un * w[2:3]).astype(o_ref.dtype)


def _short_conv_call(p, conv_w, conv_b, width, row0, n_rows, seq_len, out_rows, prior):
    ts = _largest_divisor(seq_len, 512, HALO)
    ct = _largest_divisor(width, 1024, 128)
    blocks_per_seq = seq_len // ts
    rb0 = row0 // ts
    hb = ts // HALO
    n_hblocks = out_rows // HALO
    cpb = width // ct
    n_parts = HYENA_ORDER + 1

    def cur_map(c, r):
        return (rb0 + r, c)

    def prev_map(c, r):
        return (jnp.maximum((rb0 + r) * hb - 1, 0), c)

    def next_map(c, r):
        return (jnp.minimum((rb0 + r + 1) * hb, n_hblocks - 1), c)

    in_specs = [pl.BlockSpec((ts, ct), cur_map),
                pl.BlockSpec((HALO, ct), prev_map),
                pl.BlockSpec((HALO, ct), next_map),
                pl.BlockSpec((3, ct), lambda c, r: (0, c)),
                pl.BlockSpec((1, ct), lambda c, r: (0, c))]
    args = [p, p, p, conv_w, conv_b]
    aliases = {}
    kernel = functools.partial(_short_conv_kernel, blocks_per_seq=blocks_per_seq)
    if prior is not None:
        in_specs.append(pl.BlockSpec(memory_space=pl.ANY))
        args.append(prior)
        aliases = {5: 0}
        body = kernel
        kernel = lambda c, pv, nx, w, b, _prior, o: body(c, pv, nx, w, b, o)
    return pl.pallas_call(
        kernel,
        grid=(n_parts * cpb, n_rows // ts),
        in_specs=in_specs,
        out_specs=pl.BlockSpec((None, ts, ct), lambda c, r: (c // cpb, rb0 + r, c % cpb)),
        out_shape=jax.ShapeDtypeStruct((n_parts, out_rows, width), BF16),
        input_output_aliases=aliases,
        compiler_params=_cparams(2),
        name="short_conv",
    )(*args)


@functools.lru_cache(maxsize=None)
def _filter_positions(seq_len, emb_dim, pad_dim):
    bands = (emb_dim - 1) // 2
    j = np.arange(seq_len, dtype=np.float64)
    t = j / (seq_len - 1)
    wpos = 2.0 * np.pi * j / seq_len
    f = np.linspace(1e-4, bands - 1, bands)
    z = np.concatenate([t[:, None], np.cos(f[None] * wpos[:, None]), -np.sin(f[None] * wpos[:, None])], axis=1)
    lag = np.concatenate([np.arange(seq_len), [0], np.arange(seq_len - 1, 0, -1)])
    z2 = np.zeros((2 * seq_len, pad_dim), np.float64)
    z2[:, :emb_dim] = z[lag]
    return z2.astype(np.float32)


def _filter_mlp_kernel(zt_ref, w1t_ref, w2t_ref, w3t_ref, bf_ref, o_ref):
    bf = bf_ref[0]
    h = jnp.dot(w1t_ref[0], zt_ref[...], precision=HIGHEST, preferred_element_type=F32)
    h = jnp.sin(bf[:, 3:4] * (h + bf[:, 0:1]))
    h = jnp.dot(w2t_ref[0], h, precision=HIGHEST, preferred_element_type=F32)
    h = jnp.sin(bf[:, 4:5] * (h + bf[:, 1:2]))
    h = jnp.dot(w3t_ref[0], h, precision=HIGHEST, preferred_element_type=F32)
    o_ref[0] = jnp.sin(bf[:, 5:6] * (h + bf[:, 2:3]))


def _filter_mlp(seq_len, w1, b1, w2, b2, w3, b3, freq):
    depth, emb, hid = w1.shape
    pad = -(-emb // 8) * 8
    n = 2 * seq_len
    zt = jnp.asarray(np.ascontiguousarray(_filter_positions(seq_len, emb, pad).T))
    w1t = jnp.swapaxes(jnp.pad(w1, ((0, 0), (0, pad - emb), (0, 0))), 1, 2)
    bf = jnp.concatenate([b1[:, :, None], b2[:, :, None], b3[:, :, None], jnp.swapaxes(freq, 1, 2)], axis=2)
    tr = _largest_divisor(n, 2048, LANES)
    wspec = lambda k: pl.BlockSpec((1, hid, k), lambda l, r: (l, 0, 0))
    h_t = pl.pallas_call(
        _filter_mlp_kernel,
        grid=(depth, n // tr),
        in_specs=[pl.BlockSpec((pad, tr), lambda l, r: (0, r)),
                  wspec(pad), wspec(hid), wspec(hid),
                  pl.BlockSpec((1, hid, 6), lambda l, r: (l, 0, 0))],
        out_specs=pl.BlockSpec((1, hid, tr), lambda l, r: (l, 0, r)),
        out_shape=jax.ShapeDtypeStruct((depth, hid, n), F32),
        compiler_params=_cparams(2),
        name="filter_mlp",
    )(zt, w1t, jnp.swapaxes(w2, 1, 2), jnp.swapaxes(w3, 1, 2), bf)
    return jnp.swapaxes(h_t, 1, 2)


def _filter_taps_kernel(h_ref, wf_ref, wb_ref, delta_ref, o_ref, *, seq_len):
    l = seq_len
    delta = delta_ref[...]
    lag = lax.broadcasted_iota(jnp.int32, (l, 1), 0)
    t_top = lag.astype(F32) / (l - 1.0)
    t_bot = jnp.where(lag == 0, 0, l - lag).astype(F32) / (l - 1.0)
    top = jnp.dot(h_ref[0, 0:l, :].astype(BF16), wf_ref[0].astype(BF16), preferred_element_type=F32)
    bot = jnp.dot(h_ref[0, l:2 * l, :].astype(BF16), wb_ref[0].astype(BF16), preferred_element_type=F32)
    top = top * jnp.exp(-t_top * delta)
    bot = bot * jnp.exp(-t_bot * delta)
    top = top + jnp.where(lag == 0, bot[0:1, :], 0.0)
    bot = jnp.where(lag == 0, 0.0, bot)
    ss = jnp.sum(top * top, axis=0, keepdims=True) + jnp.sum(bot * bot, axis=0, keepdims=True)
    scale = lax.rsqrt(ss + EPS)
    o_ref[0, 0:l, :] = (top * scale).astype(o_ref.dtype)
    o_ref[0, l:2 * l, :] = (bot * scale).astype(o_ref.dtype)


def _filter_taps(h_all, filt_w4, layer, seq_len, width):
    hid = h_all.shape[2]
    n = 2 * seq_len
    ct = 128
    cpb = width // ct
    min_decay = math.log(HYENA_TARGET) / HYENA_SLOW_DECAY
    max_decay = math.log(HYENA_TARGET) / HYENA_FAST_DECAY
    delta = jnp.asarray(np.abs(np.linspace(min_decay, max_decay, width)).astype(np.float32)).reshape(1, width)
    return pl.pallas_call(
        functools.partial(_filter_taps_kernel, seq_len=seq_len),
        grid=(HYENA_ORDER, cpb),
        in_specs=[pl.BlockSpec((1, n, hid), lambda o, c: (layer, 0, 0)),
                  pl.BlockSpec((1, hid, ct), lambda o, c: (layer, 0, (2 * o) * cpb + c)),
                  pl.BlockSpec((1, hid, ct), lambda o, c: (layer, 0, (2 * o + 1) * cpb + c)),
                  pl.BlockSpec((1, ct), lambda o, c: (0, c))],
        out_specs=pl.BlockSpec((1, n, ct), lambda o, c: (o, 0, c)),
        out_shape=jax.ShapeDtypeStruct((HYENA_ORDER, n, width), BF16),
        compiler_params=_cparams(2),
        name="filter_taps",
    )(h_all, filt_w4, filt_w4, delta)


def _real_form(e):
    return np.block([[e.real, -e.imag], [e.imag, e.real]])


@functools.lru_cache(maxsize=None)
def _dft_tables(n1, n2):
    n = n1 * n2
    j2 = np.arange(n2)[:, None, None]
    k1 = np.arange(n1)[None, :, None]
    j1 = np.arange(n1)[None, None, :]
    e = np.exp(-2j * np.pi * ((j1 * k1) / n1 + (j2 * k1) / n))
    eh = e[:, :, :n1 // 2]
    t_data = np.concatenate(
        [np.concatenate([eh.real, -eh.imag], axis=2), np.concatenate([eh.imag, eh.real], axis=2)], axis=1)
    t_filt = np.concatenate([e.real, e.imag], axis=1)
    hinv = np.conj(np.transpose(eh, (0, 2, 1))) / n
    t_inv = np.concatenate(
        [np.concatenate([hinv.real, -hinv.imag], axis=2), np.concatenate([hinv.imag, hinv.real], axis=2)], axis=1)
    f2 = np.exp(-2j * np.pi * np.outer(np.arange(n2), np.arange(n2)) / n2)
    as32 = lambda a: np.ascontiguousarray(a, dtype=np.float32)
    return as32(t_data), as32(t_filt), as32(t_inv), as32(_real_form(f2)), as32(_real_form(np.conj(f2)))


def _split_n(n):
    n2 = 1
    while n2 < 128 and n // (n2 * 2) >= 64:
        n2 *= 2
    return n // n2, n2


GS = 16
LANES = 128


def _swap_leading(x):
    return pltpu.einshape("abc->bac", x)


def _fft_s1_kernel(z_ref, t_ref, o_ref, s_ref):
    gs = z_ref.shape[1]
    s_ref[...] = _swap_leading(z_ref[...])
    for g in range(gs):
        o_ref[g] = jnp.dot(t_ref[g], s_ref[g], preferred_element_type=F32).astype(o_ref.dtype)


def _fft_mid_kernel(ar_ref, ai_ref, fr_ref, fi_ref, f2_ref, f2i_ref, o_ref, s_a, s_f):
    n2, gs, _ = ar_ref.shape
    s_a[:, 0:n2, :] = _swap_leading(ar_ref[...])
    s_a[:, n2:2 * n2, :] = _swap_leading(ai_ref[...])
    s_f[:, 0:n2, :] = _swap_leading(fr_ref[...])
    s_f[:, n2:2 * n2, :] = _swap_leading(fi_ref[...])
    f2 = f2_ref[...]
    f2i = f2i_ref[...]

    def spectra(g):
        return (jnp.dot(f2, s_a[g], preferred_element_type=F32),
                jnp.dot(f2, s_f[g], preferred_element_type=F32))

    nxt = spectra(0)
    for g in range(gs):
        u, k = nxt
        if g + 1 < gs:
            nxt = spectra(g + 1)
        ur, ui, kr, ki = u[:n2], u[n2:], k[:n2], k[n2:]
        y = jnp.concatenate([ur * kr - ui * ki, ur * ki + ui * kr], axis=0).astype(BF16)
        o_ref[g] = jnp.dot(f2i, y, preferred_element_type=F32).astype(o_ref.dtype)


def _fft_s3_kernel(cr_ref, ci_ref, t_ref, u_ref, x_ref, bias_ref, o_ref, s_c, s_y):
    n1, gs, _ = cr_ref.shape
    s_c[:, 0:n1, :] = _swap_leading(cr_ref[...])
    s_c[:, n1:2 * n1, :] = _swap_leading(ci_ref[...])
    for g in range(gs):
        s_y[g] = jnp.dot(t_ref[g], s_c[g], preferred_element_type=F32)
    y = _swap_leading(s_y[...])
    u = u_ref[...].astype(F32)
    o_ref[...] = (x_ref[...].astype(F32) * (y + u * bias_ref[...])).astype(o_ref.dtype)


def _hyena_small_kernel(u_ref, k_ref, x_ref, bias_ref, td_ref, tf_ref, ti_ref, _latent_rows_ref, o_ref, *, n):
    u = u_ref[...]
    a = jnp.dot(td_ref[...], u, preferred_element_type=F32)
    k = jnp.dot(tf_ref[...], k_ref[...], preferred_element_type=F32)
    ar, ai, kr, ki = a[:n], a[n:], k[:n], k[n:]
    y = jnp.concatenate([ar * kr - ai * ki, ar * ki + ai * kr], axis=0).astype(BF16)
    y = jnp.dot(ti_ref[...], y, preferred_element_type=F32)
    o_ref[...] = (x_ref[...].astype(F32) * (y + u.astype(F32) * bias_ref[...])).astype(o_ref.dtype)


def _bf16_table(a):
    return jnp.asarray(a).astype(BF16)


def _long_conv_gate(st, u3, u_part, x_arr, x_part, kk, kk_ctx, order, bias):
    assert st.b == 2, "the two batch elements are packed as one complex signal"
    _, rows, width = u3.shape
    l, lc = st.l, st.c
    n = 2 * l
    n1, n2 = _split_n(n)
    t_data, t_filt, t_inv, f2, f2i = (_bf16_table(a) for a in _dft_tables(n1, n2))
    assert n1 % GS == 0 and n2 % GS == 0 and width % LANES == 0
    bias2 = bias.reshape(1, width)
    slabs = rows // n2

    def stage1(z4, part, table, name):
        return pl.pallas_call(
            _fft_s1_kernel,
            grid=(n2 // GS,),
            in_specs=[pl.BlockSpec((None, n1, GS, width), lambda j: (part, 0, j, 0)),
                      pl.BlockSpec((GS, 2 * n1, n1), lambda j: (j, 0, 0))],
            out_specs=pl.BlockSpec((GS, 2 * n1, width), lambda j: (j, 0, 0)),
            out_shape=jax.ShapeDtypeStruct((n2, 2 * n1, width), BF16),
            scratch_shapes=[pltpu.VMEM((GS, n1, width), BF16)],
            compiler_params=_cparams(1),
            name=name,
        )(z4, table)

    a_filt = stage1(kk.reshape(HYENA_ORDER, n1, n2, width), order, t_filt, "fft_stage1_filter")
    u_v = u3.reshape(u3.shape[0], slabs, n2, width)
    a_data = stage1(u_v, u_part, t_data, "fft_stage1_data")

    ctm = _largest_divisor(width, 256, LANES)
    kb_im = n1 // GS
    re_spec = pl.BlockSpec((n2, GS, ctm), lambda c, k: (0, k, c))
    im_spec = pl.BlockSpec((n2, GS, ctm), lambda c, k: (0, kb_im + k, c))
    mat_spec = pl.BlockSpec((2 * n2, 2 * n2), lambda c, k: (0, 0))
    c_mid = pl.pallas_call(
        _fft_mid_kernel,
        grid=(width // ctm, n1 // GS),
        in_specs=[re_spec, im_spec, re_spec, im_spec, mat_spec, mat_spec],
        out_specs=pl.BlockSpec((GS, 2 * n2, ctm), lambda c, k: (k, 0, c)),
        out_shape=jax.ShapeDtypeStruct((n1, 2 * n2, width), BF16),
        scratch_shapes=[pltpu.VMEM((GS, 2 * n2, ctm), BF16)] * 2,
        compiler_params=_cparams(2),
        name="fft_mid",
    )(a_data, a_data, a_filt, a_filt, f2, f2i)

    cts = _largest_divisor(width, 1024, LANES)
    jb_im = n2 // GS
    x_v = x_arr.reshape(x_arr.shape[0], slabs, n2, width)
    out = pl.pallas_call(
        _fft_s3_kernel,
        grid=(width // cts, n2 // GS),
        in_specs=[pl.BlockSpec((n1, GS, cts), lambda c, j: (0, j, c)),
                  pl.BlockSpec((n1, GS, cts), lambda c, j: (0, jb_im + j, c)),
                  pl.BlockSpec((GS, n1, 2 * n1), lambda c, j: (j, 0, 0)),
                  pl.BlockSpec((None, n1, GS, cts), lambda c, j: (u_part, 0, j, c)),
                  pl.BlockSpec((None, n1, GS, cts), lambda c, j: (x_part, 0, j, c)),
                  pl.BlockSpec((1, cts), lambda c, j: (0, c))],
        out_specs=pl.BlockSpec((n1, GS, cts), lambda c, j: (0, j, c)),
        out_shape=jax.ShapeDtypeStruct((slabs, n2, width), BF16),
        scratch_shapes=[pltpu.VMEM((GS, 2 * n1, cts), BF16), pltpu.VMEM((GS, n1, cts), F32)],
        compiler_params=_cparams(2),
        name="fft_stage3",
    )(c_mid, c_mid, t_inv, u_v, x_v, bias2)
    out = out.reshape(rows, width)

    nc = 2 * lc
    tc_data, tc_filt, tc_inv, _, _ = (_bf16_table(a) for a in _dft_tables(nc, 1))
    ct = _largest_divisor(width, 512, 128)
    rb = (st.rows_lat) // (2 * lc)
    out = pl.pallas_call(
        functools.partial(_hyena_small_kernel, n=nc),
        grid=(width // ct,),
        in_specs=[pl.BlockSpec((None, 2 * lc, ct), lambda c: (u_part, rb, c)),
                  pl.BlockSpec((None, nc, ct), lambda c: (order, 0, c)),
                  pl.BlockSpec((None, 2 * lc, ct), lambda c: (x_part, rb, c)),
                  pl.BlockSpec((1, ct), lambda c: (0, c)),
                  pl.BlockSpec((2 * nc, nc), lambda c: (0, 0)),
                  pl.BlockSpec((2 * nc, nc), lambda c: (0, 0)),
                  pl.BlockSpec((nc, 2 * nc), lambda c: (0, 0)),
                  pl.BlockSpec(memory_space=pl.ANY)],
        out_specs=pl.BlockSpec((2 * lc, ct), lambda c: (rb, c)),
        out_shape=jax.ShapeDtypeStruct((rows, width), BF16),
        input_output_aliases={7: 0},
        compiler_params=_cparams(1),
        name="hyena_context",
    )(u3, kk_ctx, x_arr, bias2, tc_data[0], tc_filt[0], tc_inv[0], out)
    return out


@functools.lru_cache(maxsize=None)
def _pool_tables(t, windows):
    out = np.zeros((4, len(windows), t, 3 * t), np.float64)
    for v in range(4):
        lo_bound = t if v & 1 else 0
        hi_bound = 2 * t - 1 if v & 2 else 3 * t - 1
        for g, w in enumerate(windows):
            for r in range(t):
                pos = t + r
                lo = max(pos - w // 2, lo_bound)
                hi = min(pos + w - 1 - w // 2, hi_bound)
                out[v, g, r, lo:hi + 1] = 1.0 / (hi - lo + 1)
                out[v, g, r, pos] -= 1.0
    return out.astype(np.float32)


def _pool_kernel(prev_ref, cur_ref, next_ref, m_ref, w_ref, s_ref, *rest, n_groups, gw):
    o_ref = rest[-1]
    for g in range(n_groups):
        sl = slice(g * gw, (g + 1) * gw)
        u = jnp.concatenate([prev_ref[:, sl], cur_ref[:, sl], next_ref[:, sl]], axis=0)
        d = jnp.dot(m_ref[0, g], u, preferred_element_type=F32)
        y = jnp.dot(d.astype(BF16), w_ref[g], preferred_element_type=F32)
        o_ref[:, sl] = (y * s_ref[:, sl]).astype(o_ref.dtype)


def _pool_call(p, col_block, w_pool_bf, pool_scale3, layer, width, row0, n_seq, seq_len, out_rows, prior):
    t = _largest_divisor(seq_len, 256, 16)
    nblk = seq_len // t
    rb0 = row0 // t
    n_groups = len(POOL_WINDOWS)
    gw = width // n_groups
    tables = _bf16_table(_pool_tables(t, POOL_WINDOWS))

    def blk(delta):
        return lambda s, j: (rb0 + s * nblk + jnp.clip(j + delta, 0, nblk - 1), col_block)

    def variant(s, j):
        return ((j == 0).astype(jnp.int32) + 2 * (j == nblk - 1).astype(jnp.int32), 0, 0, 0)

    in_specs = [pl.BlockSpec((t, width), blk(-1)),
                pl.BlockSpec((t, width), blk(0)),
                pl.BlockSpec((t, width), blk(1)),
                pl.BlockSpec((1, n_groups, t, 3 * t), variant),
                pl.BlockSpec((None, n_groups, gw, gw), lambda s, j: (layer, 0, 0, 0)),
                pl.BlockSpec((None, 1, width), lambda s, j: (layer, 0, 0))]
    args = [p, p, p, tables, w_pool_bf, pool_scale3]
    aliases = {}
    if prior is not None:
        in_specs.append(pl.BlockSpec(memory_space=pl.ANY))
        args.append(prior)
        aliases = {6: 0}
    return pl.pallas_call(
        functools.partial(_pool_kernel, n_groups=n_groups, gw=gw),
        grid=(n_seq, nblk),
        in_specs=in_specs,
        out_specs=pl.BlockSpec((t, width), lambda s, j: (rb0 + s * nblk + j, 0)),
        out_shape=jax.ShapeDtypeStruct((out_rows, width), BF16),
        input_output_aliases=aliases,
        compiler_params=_cparams(2),
        name="pool",
    )(*args)


@functools.lru_cache(maxsize=None)
def _rope_tables(seq_len, grid_w, hd, pad_rows):
    axis = hd // 2
    pos = np.arange(seq_len)
    inv = ROPE_THETA ** (-np.arange(0, axis, 2, dtype=np.float64) / axis)
    ang_r = (pos // grid_w)[:, None] * inv[None]
    ang_c = (pos % grid_w)[:, None] * inv[None]
    cos = np.concatenate([np.cos(ang_r), np.cos(ang_r), np.cos(ang_c), np.cos(ang_c)], axis=1)
    sin = np.concatenate([-np.sin(ang_r), np.sin(ang_r), -np.sin(ang_c), np.sin(ang_c)], axis=1)
    cos = np.concatenate([cos, np.ones((pad_rows, hd))], axis=0)
    sin = np.concatenate([sin, np.zeros((pad_rows, hd))], axis=0)
    return cos.astype(np.float32), sin.astype(np.float32)


def _qk_prep_kernel(q_ref, k_ref, v_ref, qg_ref, kg_ref, cos_ref, sin_ref, qto_ref, ko_ref, vto_ref,
                    *, hd, q_scale):
    cos = cos_ref[...]
    sin = sin_ref[...]
    ones = jnp.ones((hd, hd), BF16)
    src = lax.broadcasted_iota(jnp.int32, (hd, hd), 0)
    dst = lax.broadcasted_iota(jnp.int32, (hd, hd), 1)
    partner_of = jnp.where((dst % (hd // 2)) < (hd // 4), dst + hd // 4, dst - hd // 4)
    swap = jnp.where(src == partner_of, 1.0, 0.0).astype(BF16)

    def normed_rotated(x_ref, h, g_ref, mul):
        x = x_ref[:, h * hd:(h + 1) * hd].astype(F32)
        ss = jnp.dot((x * x).astype(BF16), ones, preferred_element_type=F32)
        y = x * lax.rsqrt(ss * (1.0 / hd) + EPS) * g_ref[...]
        partner = jnp.dot(y.astype(BF16), swap, preferred_element_type=F32)
        return (y * cos + partner * sin) * mul

    for h in range(q_ref.shape[1] // hd):
        qto_ref[h * hd:(h + 1) * hd, :] = normed_rotated(q_ref, h, qg_ref, q_scale).T.astype(qto_ref.dtype)
    for h in range(k_ref.shape[1] // hd):
        sl = slice(h * hd, (h + 1) * hd)
        ko_ref[:, sl] = normed_rotated(k_ref, h, kg_ref, 1.0).astype(ko_ref.dtype)
        vto_ref[sl, :] = v_ref[:, sl].astype(F32).T.astype(vto_ref.dtype)


def _qk_prep(st, p, q_col0, k_col0, aw, kvw, q_gain, k_gain, layer, hd):
    cos_np, sin_np = _rope_tables(st.l, GRID_W, hd, st.tm)
    tab_map = lambda i: (jnp.where(i < st.n_lat_tiles, i % st.tiles_per_seq, st.tiles_per_seq), 0)
    return pl.pallas_call(
        functools.partial(_qk_prep_kernel, hd=hd, q_scale=hd ** -0.5),
        grid=(st.n_tiles,),
        in_specs=[pl.BlockSpec((st.tm, aw), lambda i: (i, q_col0 // aw)),
                  pl.BlockSpec((st.tm, kvw), lambda i: (i, k_col0 // kvw)),
                  pl.BlockSpec((st.tm, kvw), lambda i: (i, k_col0 // kvw + 1)),
                  pl.BlockSpec((None, 1, hd), lambda i: (layer, 0, 0)),
                  pl.BlockSpec((None, 1, hd), lambda i: (layer, 0, 0)),
                  pl.BlockSpec((st.tm, hd), tab_map),
                  pl.BlockSpec((st.tm, hd), tab_map)],
        out_specs=[pl.BlockSpec((aw, st.tm), lambda i: (0, i)),
                   pl.BlockSpec((st.tm, kvw), lambda i: (i, 0)),
                   pl.BlockSpec((kvw, st.tm), lambda i: (0, i))],
        out_shape=[jax.ShapeDtypeStruct((aw, st.rows), BF16),
                   jax.ShapeDtypeStruct((st.rows, kvw), BF16),
                   jax.ShapeDtypeStruct((kvw, st.rows), BF16)],
        compiler_params=_cparams(1),
        name="qk_prep",
    )(p, p, p, q_gain.reshape(-1, 1, hd), k_gain.reshape(-1, 1, hd), jnp.asarray(cos_np), jnp.asarray(sin_np))


ONES_ROWS = 16

def _attn_kernel(*refs, group, hd, tk, n_chunks):
    qt_ref, kc_ref, vtc_ref = refs[:3]
    o_ref, m_s, acc_s, s_a, s_b = refs[-5:]
    tq = qt_ref.shape[1]
    qs = jnp.concatenate([qt_ref[g * hd:(g + 1) * hd, :] for g in range(group)], axis=1)
    m_s[...] = jnp.full(m_s.shape, -1e30, F32)
    acc_s[...] = jnp.zeros(acc_s.shape, F32)

    def scores(k):
        return jnp.dot(k, qs, preferred_element_type=F32)

    def softmax_pv(s, vt):
        vt1 = jnp.concatenate([vt, jnp.ones((ONES_ROWS, vt.shape[1]), BF16)], axis=0)
        m_prev = m_s[...]
        m_new = jnp.maximum(m_prev, jnp.max(s, axis=0, keepdims=True))
        alpha = jnp.exp(m_prev - m_new)
        p = jnp.exp((s - m_new).astype(BF16))
        acc_s[...] = alpha * acc_s[...] + jnp.dot(vt1, p, preferred_element_type=F32)
        m_s[...] = m_new

    s_ctx = scores(kc_ref[...])
    if not n_chunks:
        softmax_pv(s_ctx, vtc_ref[...])
    else:
        kl_ref, vtl_ref = refs[3:5]

        def keys(c):
            return kl_ref[pl.ds(pl.multiple_of(c * tk, tk), tk), :]

        def vals(c):
            return vtl_ref[:, pl.ds(pl.multiple_of(c * tk, tk), tk)]

        first = n_chunks % 2
        if n_chunks - first:
            s_a[...] = scores(keys(first))
        softmax_pv(s_ctx, vtc_ref[...])
        if first:
            softmax_pv(scores(keys(0)), vals(0))
        if n_chunks - first:

            def body(j, carry):
                c = first + 2 * j
                s_b[...] = scores(keys(c + 1))
                softmax_pv(s_a[...], vals(c))
                s_a[...] = scores(keys(jnp.minimum(c + 2, n_chunks - 1)))
                softmax_pv(s_b[...], vals(c + 1))
                return carry

            lax.fori_loop(0, (n_chunks - first) // 2, body, 0)
    o = acc_s[0:hd, :] / acc_s[hd:hd + 1, :]
    for g in range(group):
        o_ref[:, g * hd:(g + 1) * hd] = o[:, g * tq:(g + 1) * tq].T.astype(o_ref.dtype)


def _attention(st, q_t, k, v_t, hd):
    group = N_HEADS // N_KV_HEADS
    gw = group * hd
    aw = N_HEADS * hd
    tq = _largest_divisor(st.l, 2048, LANES)
    tk = _largest_divisor(st.l, 512, LANES)
    qt = st.l // tq
    cb0 = st.rows_lat // st.c

    def scratch(q_rows):
        nq = group * q_rows
        return [pltpu.VMEM((1, nq), F32), pltpu.VMEM((hd + ONES_ROWS, nq), F32),
                pltpu.VMEM((tk, nq), F32), pltpu.VMEM((tk, nq), F32)]

    yc = pl.pallas_call(
        functools.partial(_attn_kernel, group=group, hd=hd, tk=tk, n_chunks=st.l // tk),
        grid=(st.b, N_KV_HEADS, qt),
        in_specs=[pl.BlockSpec((gw, tq), lambda b, h, t: (h, b * qt + t)),
                  pl.BlockSpec((st.c, hd), lambda b, h, t: (cb0 + b, h)),
                  pl.BlockSpec((hd, st.c), lambda b, h, t: (h, cb0 + b)),
                  pl.BlockSpec((st.l, hd), lambda b, h, t: (b, h)),
                  pl.BlockSpec((hd, st.l), lambda b, h, t: (h, b))],
        out_specs=pl.BlockSpec((tq, gw), lambda b, h, t: (b * qt + t, h)),
        out_shape=jax.ShapeDtypeStruct((st.rows, aw), BF16),
        scratch_shapes=scratch(tq),
        compiler_params=_cparams(3),
        name="attention_latent",
    )(q_t, k, v_t, k, v_t)
    return pl.pallas_call(
        functools.partial(_attn_kernel, group=group, hd=hd, tk=tk, n_chunks=0),
        grid=(st.b, N_KV_HEADS),
        in_specs=[pl.BlockSpec((gw, st.c), lambda b, h: (h, cb0 + b)),
                  pl.BlockSpec((st.c, hd), lambda b, h: (cb0 + b, h)),
                  pl.BlockSpec((hd, st.c), lambda b, h: (h, cb0 + b)),
                  pl.BlockSpec(memory_space=pl.ANY)],
        out_specs=pl.BlockSpec((st.c, gw), lambda b, h: (cb0 + b, h)),
        out_shape=jax.ShapeDtypeStruct((st.rows, aw), BF16),
        input_output_aliases={3: 0},
        scratch_shapes=scratch(st.c),
        compiler_params=_cparams(2),
        name="attention_context",
    )(q_t, k, v_t, yc)


def kernel(x, c, ctx, c_ctx, w_mod, b_mod, norm_gain, final_gain, ffn_w_in, ffn_w_out, w_in, b_gate, conv_w, conv_b, filt_w1, filt_b1, filt_w2, filt_b2, filt_w3, filt_b3, filt_w4, filt_freq, hyena_bias, w_pool, pool_scale, q_gain, k_gain, w_up, w_out):
    b, l, d = x.shape
    lc = ctx.shape[1]
    depth = w_mod.shape[0]
    width = hyena_bias.shape[2]
    hd = q_gain.shape[1]
    aw, kvw = N_HEADS * hd, N_KV_HEADS * hd
    a1 = (HYENA_ORDER + 1) * width
    q0 = a1 + width
    k0 = q0 + aw
    v0 = k0 + kvw
    g0 = v0 + kvw
    st = _Stream(b, l, lc)

    xs = jnp.concatenate([x.reshape(b * l, d), ctx.reshape(b * lc, d)], axis=0)
    cvec = jnp.zeros((8, d), F32).at[:b].set(c).at[b].set(c_ctx)
    mods = _modulation(cvec, w_mod, b_mod).reshape(depth, 8, N_MOD, d)[:, :b + 1]
    mods = mods.reshape(depth * (b + 1) * N_MOD, 1, d)
    gains = norm_gain.reshape(depth * 3, 1, d)
    b_gate3 = b_gate.reshape(depth * N_BRANCH, 1, d)
    pool_scale3 = pool_scale.reshape(depth, 1, width)
    w_pool_bf = w_pool.astype(BF16)
    ffn_w_out_bf = ffn_w_out.astype(BF16)
    w_up_bf = w_up.astype(BF16)
    w_out_bf = w_out.astype(BF16)
    h_lat =_filter_mlp(l, filt_w1, filt_b1, filt_w2, filt_b2, filt_w3, filt_b3, filt_freq)
    h_ctx = _filter_mlp(lc, filt_w1, filt_b1, filt_w2, filt_b2, filt_w3, filt_b3, filt_freq)

    for i in range(depth):
        h = _adaln(st, xs, gains, mods, i, 0, 0, 1)
        a = _ffn_in(st, h, ffn_w_in, i, 0)
        xs = _mm_residual(st, a, ffn_w_out_bf, (i, 0), xs, mods, i, 2, 0.5, 512)
        h = _adaln(st, xs, gains, mods, i, 1, 3, 4)
        p = _in_proj(st, h, w_in, i)
        cw, cb = conv_w[i], conv_b[i].reshape(1, -1)
        u3 = _short_conv_call(p, cw, cb, width, 0, st.rows_lat, l, st.rows, None)
        u3 = _short_conv_call(p, cw, cb, width, st.rows_lat, b * lc, lc, st.rows, u3)
        kk = _filter_taps(h_lat, filt_w4, i, l, width)
        kk_ctx = _filter_taps(h_ctx, filt_w4, i, lc, width)
        z = _long_conv_gate(st, u3, 0, u3, 1, kk, kk_ctx, 0, hyena_bias[i, 0])
        ya = _long_conv_gate(st, z[None], 0, u3, 2, kk, kk_ctx, 1, hyena_bias[i, 1])
        yb = _pool_call(p, a1 // width, w_pool_bf, pool_scale3, i, width, 0, b, l, st.rows, None)
        yb = _pool_call(p, a1 // width, w_pool_bf, pool_scale3, i, width, st.rows_lat, b, lc, st.rows, yb)
        q_t, k, v_t = _qk_prep(st, p, q0, k0, aw, kvw, q_gain, k_gain, i, hd)
        yc = _attention(st, q_t, k, v_t, hd)
        merged = _merge(st, ya, yb, yc, p, g0, b_gate3, w_up_bf, i)
        xs = _mm_residual(st, merged, w_out_bf, (i,), xs, mods, i, 5, 1.0, 512)
        h = _adaln(st, xs, gains, mods, i, 2, 6, 7)
        a = _ffn_in(st, h, ffn_w_in, i, 1)
        xs = _mm_residual(st, a, ffn_w_out_bf, (i, 1), xs, mods, i, 8, 0.5, 512)

    out = pl.pallas_call(
        _rmsnorm_kernel,
        grid=(st.n_lat_tiles,),
        in_specs=[pl.BlockSpec((st.tm, d), lambda i: (i, 0)),
                  pl.BlockSpec((1, d), lambda i: (0, 0))],
        out_specs=pl.BlockSpec((st.tm, d), lambda i: (i, 0)),
        out_shape=jax.ShapeDtypeStruct((b * l, d), F32),
        compiler_params=_cparams(1),
        name="final_rmsnorm",
    )(xs, final_gain.reshape(1, d))
    return out.reshape(b, l, d)
```
